```python
import jax, jax.numpy as jnp
from jax import lax
import numpy as np

D_MODEL = 1024
BATCH = 16
SEQ = 2048
DEPTH = 2
DEC_BATCH = 128
DEC_SEQ = 8
PAST_LEN = 16384
PAGE_SIZE = 128

N_HEADS = 8
N_KV_HEADS = 2
HEAD_DIM = 64
GQA_GROUP = N_HEADS // N_KV_HEADS
ATTN_WIDTH = N_HEADS * HEAD_DIM
KV_WIDTH = N_KV_HEADS * HEAD_DIM
POOL_WINDOWS = (2, 4, 8, 16)
POOL_WIDTH = D_MODEL - ATTN_WIDTH
POOL_GROUP = POOL_WIDTH // len(POOL_WINDOWS)
POOL_HIST = max(POOL_WINDOWS) - 1
MIX_WIDTH = ATTN_WIDTH + POOL_WIDTH
IN_WIDTH = ATTN_WIDTH + 2 * KV_WIDTH + POOL_WIDTH
WINDOW = 128
BLOCK = WINDOW
ROT_DIM = HEAD_DIM // 4
ROPE_THETA = 500000.0
D_FF = 4 * D_MODEL
N_META = 16
RMS_EPS = 1e-5

kernel_name = "hymba_pool_swa_sink_hybrid_step"


def _rmsnorm(x, g):
    xf = x.astype(jnp.float32)
    r = lax.rsqrt(jnp.mean(xf * xf, axis=-1, keepdims=True) + RMS_EPS)
    return (xf * r * g.astype(jnp.float32)).astype(x.dtype)


def _project(h, g, w):
    xn = _rmsnorm(h, g)
    p = jnp.einsum('btd,de->bte', xn, w)
    b, t = h.shape[0], h.shape[1]
    q = p[..., :ATTN_WIDTH].reshape(b, t, N_HEADS, HEAD_DIM)
    k = p[..., ATTN_WIDTH:ATTN_WIDTH + KV_WIDTH].reshape(b, t, N_KV_HEADS, HEAD_DIM)
    v = p[..., ATTN_WIDTH + KV_WIDTH:ATTN_WIDTH + 2 * KV_WIDTH].reshape(b, t, N_KV_HEADS, HEAD_DIM)
    u = p[..., ATTN_WIDTH + 2 * KV_WIDTH:]
    return q, k, v, u


def _rope(x, pos):
    inv_freq = ROPE_THETA ** (-jnp.arange(0, ROT_DIM, 2, dtype=jnp.float32) / ROT_DIM)
    ang = pos.astype(jnp.float32)[:, None] * inv_freq[None, :]
    cos = jnp.cos(ang)[None, :, None, :]
    sin = jnp.sin(ang)[None, :, None, :]
    xr = x[..., :ROT_DIM].astype(jnp.float32)
    x1, x2 = xr[..., :ROT_DIM // 2], xr[..., ROT_DIM // 2:]
    rot = jnp.concatenate([x1 * cos - x2 * sin, x2 * cos + x1 * sin], axis=-1)
    return jnp.concatenate([rot.astype(x.dtype), x[..., ROT_DIM:]], axis=-1)


def _band_attention(q, k, v, qpos, kpos, sink):
    scale = HEAD_DIM ** -0.5
    s = jnp.einsum('bnqkgd,bnskd->bnkgqs', q, k).astype(jnp.float32) * scale
    dpos = qpos[:, :, None] - kpos[:, None, :]
    mask = (kpos[:, None, :] >= 0) & (dpos >= 0) & (dpos <= WINDOW)
    s = jnp.where(mask[None, :, None, None], s, -jnp.inf)
    sk = sink.astype(jnp.float32).reshape(N_KV_HEADS, GQA_GROUP)[None, None, :, :, None, None]
    m = jnp.maximum(jnp.max(s, axis=-1, keepdims=True), sk)
    p = jnp.exp(s - m)
    denom = jnp.sum(p, axis=-1, keepdims=True) + jnp.exp(sk - m)
    return jnp.einsum('bnkgqs,bnskd->bnqkgd', (p / denom).astype(v.dtype), v)


def _prompt_attention(q, k, v, sink):
    b, L = q.shape[0], q.shape[1]
    padf = BLOCK - N_META
    lp = L + padf
    nb = lp // BLOCK
    pad = lambda a: jnp.pad(a, ((0, 0), (padf, 0)) + ((0, 0),) * (a.ndim - 2))
    qb = pad(q).reshape(b, nb, BLOCK, N_KV_HEADS, GQA_GROUP, HEAD_DIM)
    kb = pad(k).reshape(b, nb, BLOCK, N_KV_HEADS, HEAD_DIM)
    vb = pad(v).reshape(b, nb, BLOCK, N_KV_HEADS, HEAD_DIM)
    shift = lambda a: jnp.concatenate([jnp.zeros_like(a[:, :1]), a[:, :-1]], axis=1)
    kk = jnp.concatenate([shift(kb), kb], axis=2)
    vv = jnp.concatenate([shift(vb), vb], axis=2)
    posb = (jnp.arange(lp) - padf).reshape(nb, BLOCK)
    kpos = jnp.concatenate([posb - BLOCK, posb], axis=1)
    o = _band_attention(qb, kk, vv, posb, kpos, sink)
    return o.reshape(b, lp, ATTN_WIDTH)[:, padf:]


def _sample_attention(q, k, v, ck, cv, sink):
    b, s = q.shape[0], q.shape[1]
    qb = q.reshape(b, 1, s, N_KV_HEADS, GQA_GROUP, HEAD_DIM)
    kk = jnp.concatenate([ck, k], axis=1)[:, None]
    vv = jnp.concatenate([cv, v], axis=1)[:, None]
    qpos = (PAST_LEN + jnp.arange(s))[None]
    kpos = (PAST_LEN - WINDOW + jnp.arange(WINDOW + s))[None]
    o = _band_attention(qb, kk, vv, qpos, kpos, sink)
    return o.reshape(b, s, ATTN_WIDTH)


def _pool_mix(u_all, pos_all, n_out, pool_w, pool_scale):
    T = u_all.shape[1]
    uf = u_all.astype(jnp.float32)
    cs = jnp.cumsum(uf, axis=1)
    cs0 = jnp.concatenate([jnp.zeros_like(cs[:, :1]), cs], axis=1)
    t_idx = jnp.arange(T - n_out, T)
    outs = []
    for g, w in enumerate(POOL_WINDOWS):
        lo_c, hi_c = g * POOL_GROUP, (g + 1) * POOL_GROUP
        lo = jnp.maximum(t_idx + 1 - w, 0)
        wsum = cs0[:, t_idx + 1, lo_c:hi_c] - cs0[:, lo, lo_c:hi_c]
        cnt = jnp.minimum(pos_all[t_idx] + 1, w).astype(jnp.float32)
        d = wsum / cnt[None, :, None] - uf[:, t_idx, lo_c:hi_c]
        outs.append(jnp.einsum('btc,ce->bte', d, pool_w[g].astype(jnp.float32)))
    out = jnp.concatenate(outs, axis=-1) * pool_scale.astype(jnp.float32)
    return out.astype(u_all.dtype)


def _mlp(h, g, w_up, w_down):
    xn = _rmsnorm(h, g)
    a = jax.nn.relu(jnp.einsum('btd,df->btf', xn, w_up))
    return jnp.einsum('btf,fd->btd', a * a, w_down)


def setup_inputs(seed: int = 0) -> dict:
    key = jax.random.key(seed)
    ks = jax.random.split(key, 17)
    n = lambda k, s, sc=1.0: jax.random.normal(k, s, jnp.float32) * sc
    return {
        "x_prompt": n(ks[0], (BATCH, SEQ, D_MODEL)),
        "x_sample": n(ks[1], (DEC_BATCH, DEC_SEQ, D_MODEL)),
        "cache_k": n(ks[2], (DEPTH, DEC_BATCH, WINDOW, N_KV_HEADS, HEAD_DIM)),
        "cache_v": n(ks[3], (DEPTH, DEC_BATCH, WINDOW, N_KV_HEADS, HEAD_DIM)),
        "state_pool": n(ks[4], (DEPTH, DEC_BATCH, POOL_HIST, POOL_WIDTH)),
        "meta_tokens": n(ks[5], (N_META, D_MODEL)),
        "ln1": 1.0 + n(ks[6], (DEPTH, D_MODEL), 0.02),
        "w_in": n(ks[7], (DEPTH, D_MODEL, IN_WIDTH), D_MODEL ** -0.5),
        "attn_sinks": n(ks[8], (DEPTH, N_HEADS), 0.5),
        "pool_w": n(ks[9], (DEPTH, len(POOL_WINDOWS), POOL_GROUP, POOL_GROUP), POOL_GROUP ** -0.5),
        "pool_scale": 1.0 + n(ks[10], (DEPTH, POOL_WIDTH), 0.1),
        "w_out": n(ks[11], (DEPTH, MIX_WIDTH, D_MODEL), MIX_WIDTH ** -0.5),
        "ln2": 1.0 + n(ks[12], (DEPTH, D_MODEL), 0.02),
        "w_up": n(ks[13], (DEPTH, D_MODEL, D_FF), D_MODEL ** -0.5),
        "w_down": n(ks[14], (DEPTH, D_FF, D_MODEL), D_FF ** -0.5),
        "ln_f": 1.0 + n(ks[15], (D_MODEL,), 0.02),
    }


def reference(x_prompt, x_sample, cache_k, cache_v, state_pool, meta_tokens, ln1, w_in, attn_sinks,
              pool_w, pool_scale, w_out, ln2, w_up, w_down, ln_f):
    b = x_prompt.shape[0]
    meta = jnp.broadcast_to(meta_tokens.astype(x_prompt.dtype)[None], (b, N_META, D_MODEL))
    hp = jnp.concatenate([meta, x_prompt], axis=1)
    L = hp.shape[1]
    pos_p = jnp.arange(L)
    hs = x_sample
    s = x_sample.shape[1]
    pos_s = PAST_LEN + jnp.arange(s)
    pos_pool_s = PAST_LEN - POOL_HIST + jnp.arange(POOL_HIST + s)
    pk, pv, pu, sk, sv, su = [], [], [], [], [], []
    for l in range(DEPTH):
        q, k, v, u = _project(hp, ln1[l], w_in[l])
        q = _rope(q, pos_p)
        k = _rope(k, pos_p)
        a = _prompt_attention(q, k, v, attn_sinks[l])
        pm = _pool_mix(u, pos_p, L, pool_w[l], pool_scale[l])
        hp = hp + jnp.einsum('bte,ed->btd', jnp.concatenate([a, pm], axis=-1), w_out[l])
        hp = hp + _mlp(hp, ln2[l], w_up[l], w_down[l])
        pk.append(k[:, -WINDOW:])
        pv.append(v[:, -WINDOW:])
        pu.append(u[:, -POOL_HIST:])
        q, k, v, u = _project(hs, ln1[l], w_in[l])
        q = _rope(q, pos_s)
        k = _rope(k, pos_s)
        a = _sample_attention(q, k, v, cache_k[l], cache_v[l], attn_sinks[l])
        u_all = jnp.concatenate([state_pool[l].astype(u.dtype), u], axis=1)
        pm = _pool_mix(u_all, pos_pool_s, s, pool_w[l], pool_scale[l])
        hs = hs + jnp.einsum('bte,ed->btd', jnp.concatenate([a, pm], axis=-1), w_out[l])
        hs = hs + _mlp(hs, ln2[l], w_up[l], w_down[l])
        sk.append(jnp.concatenate([cache_k[l].astype(k.dtype), k], axis=1)[:, -WINDOW:])
        sv.append(jnp.concatenate([cache_v[l].astype(v.dtype), v], axis=1)[:, -WINDOW:])
        su.append(u_all[:, -POOL_HIST:])
    y_prompt = _rmsnorm(hp, ln_f)[:, N_META:]
    y_sample = _rmsnorm(hs, ln_f)
    return (y_prompt, y_sample, jnp.stack(pk), jnp.stack(pv), jnp.stack(pu), jnp.stack(sk), jnp.stack(sv), jnp.stack(su))
```

```python
import functools

import jax
import jax.numpy as jnp
from jax import lax
from jax.experimental import pallas as pl
from jax.experimental.pallas import tpu as pltpu

D_MODEL = 1024
N_HEADS = 8
N_KV_HEADS = 2
HEAD_DIM = 64
ATTN_WIDTH = N_HEADS * HEAD_DIM
KV_WIDTH = N_KV_HEADS * HEAD_DIM
POOL_WINDOWS = (2, 4, 8, 16)
POOL_WIDTH = D_MODEL - ATTN_WIDTH
POOL_GROUP = POOL_WIDTH // len(POOL_WINDOWS)
POOL_HIST = max(POOL_WINDOWS) - 1
IN_WIDTH = ATTN_WIDTH + 2 * KV_WIDTH + POOL_WIDTH
WINDOW = 128
ROT_DIM = HEAD_DIM // 4
ROPE_THETA = 500000.0
D_FF = 4 * D_MODEL
N_META = 16
RMS_EPS = 1e-5
PAST_LEN = 16384
DEC_SEQ = 8

LANES = 128
HIST_ROWS = POOL_HIST + 1
Q_BLOCKS = ATTN_WIDTH // LANES
SCALE = HEAD_DIM ** -0.5
NEG_INF = float("-inf")

PROMPT_TILE = 256
SAMPLE_SEQS = 32
MLP_TILE = 512
VMEM_LIMIT = 56 * 1024 * 1024

F32 = jnp.float32
BF16 = jnp.bfloat16


def _rmsnorm(x, g):
    r = lax.rsqrt(jnp.mean(x * x, axis=-1, keepdims=True) + RMS_EPS)
    return x * r * g


def _rope(x, rope):
    c = rope[:, 0:LANES]
    s1 = rope[:, LANES:2 * LANES]
    s2 = rope[:, 2 * LANES:3 * LANES]
    half = ROT_DIM // 2
    return x * c + pltpu.roll(x, LANES - half, 1) * s1 + pltpu.roll(x, half, 1) * s2


def _project(h, ln_ref, win_ref, rope):
    xn = _rmsnorm(h, ln_ref[...]).astype(BF16)
    p = jnp.dot(xn, win_ref[...], preferred_element_type=F32)
    qb = [_rope(p[:, j * LANES:(j + 1) * LANES], rope) * SCALE for j in range(Q_BLOCKS)]
    k = _rope(p[:, ATTN_WIDTH:ATTN_WIDTH + KV_WIDTH], rope)
    v = p[:, ATTN_WIDTH + KV_WIDTH:ATTN_WIDTH + 2 * KV_WIDTH]
    u = p[:, ATTN_WIDTH + 2 * KV_WIDTH:]
    return qb, k, v, u


def _stack_heads(qrows, lo):
    zero = jnp.zeros_like(qrows[0])
    parts = [jnp.where(lo, q, zero) for q in qrows] + [jnp.where(lo, zero, q) for q in qrows]
    return jnp.concatenate(parts, axis=0).astype(BF16)


def _softmax_sink(s, sink):
    m = jnp.maximum(jnp.max(s, axis=-1, keepdims=True), sink)
    e = jnp.exp(s - m)
    l = jnp.sum(e, axis=-1, keepdims=True) + jnp.exp(sink - m)
    return (e * (1.0 / l)).astype(BF16)


def _pool_project(d_groups, pw_ref, ps_ref):
    outs = [jnp.dot(d.astype(BF16), pw_ref[g], preferred_element_type=F32)
            for g, d in enumerate(d_groups)]
    return jnp.concatenate(outs, axis=-1) * ps_ref[...]


def _mixer_long_kernel(sink_ref, h_ref, rope_ref, kin_ref, vin_ref, uin_ref, ln_ref, win_ref,
                       pw_ref, ps_ref, wout_ref, ho_ref, kt_ref, vt_ref, ut_ref,
                       kbuf, vbuf, ubuf, *, tile, base_pos):
    t = pl.program_id(1)

    @pl.when(t == 0)
    def _():
        kbuf[0:WINDOW, :] = kin_ref[0].astype(BF16)
        vbuf[0:WINDOW, :] = vin_ref[0].astype(BF16)
        ubuf[0:HIST_ROWS, :] = uin_ref[0]

    h = h_ref[0]
    qb, k, v, u = _project(h, ln_ref, win_ref, rope_ref[...])
    kbuf[WINDOW:WINDOW + tile, :] = k.astype(BF16)
    vbuf[WINDOW:WINDOW + tile, :] = v.astype(BF16)
    ubuf[HIST_ROWS:HIST_ROWS + tile, :] = u
    kt_ref[0] = k[tile - WINDOW:, :]
    vt_ref[0] = v[tile - WINDOW:, :]
    ut_ref[0] = u[tile - HIST_ROWS:, :]

    p0 = base_pos + t * tile
    lo = lax.broadcasted_iota(jnp.int32, (1, LANES), 1) < HEAD_DIM
    qi = lax.broadcasted_iota(jnp.int32, (WINDOW, 2 * WINDOW), 0)
    kc = lax.broadcasted_iota(jnp.int32, (WINDOW, 2 * WINDOW), 1)
    band = (kc >= qi) & (kc <= qi + WINDOW)

    attn_rows = []
    for r in range(tile // WINDOW):
        rows = slice(r * WINDOW, (r + 1) * WINDOW)
        q = _stack_heads([qb[j][rows] for j in range(Q_BLOCKS)], lo)
        kb = kbuf[r * WINDOW:(r + 2) * WINDOW, :]
        vb = vbuf[r * WINDOW:(r + 2) * WINDOW, :]
        s = lax.dot_general(q, kb, (((1,), (1,)), ((), ())), preferred_element_type=F32)
        kpos = p0 + (r - 1) * WINDOW + kc
        bias = jnp.where(band & (kpos >= 0), 0.0, NEG_INF)
        probs = [_softmax_sink(s[hh * WINDOW:(hh + 1) * WINDOW] + bias, sink_ref[hh])
                 for hh in range(N_HEADS)]
        o = jnp.dot(jnp.concatenate(probs, axis=0), vb, preferred_element_type=F32)
        attn_rows.append(jnp.concatenate(
            [jnp.where(lo, o[j * WINDOW:(j + 1) * WINDOW], o[(j + Q_BLOCKS) * WINDOW:(j + Q_BLOCKS + 1) * WINDOW])
             for j in range(Q_BLOCKS)], axis=-1))
    attn = jnp.concatenate(attn_rows, axis=0)

    pos = p0 + lax.broadcasted_iota(jnp.int32, (tile, 1), 0)
    d_groups = []
    for g, w in enumerate(POOL_WINDOWS):
        cols = slice(g * POOL_GROUP, (g + 1) * POOL_GROUP)
        ug = u[:, cols]
        wsum = ug
        for back in range(1, w):
            wsum = wsum + ubuf[HIST_ROWS - back:HIST_ROWS - back + tile, cols]
        cnt = jnp.clip(pos + 1, 1, w).astype(F32)
        d_groups.append(wsum / cnt - ug)
    pm = _pool_project(d_groups, pw_ref, ps_ref)

    mix = jnp.concatenate([attn, pm], axis=-1).astype(BF16)
    ho_ref[0] = h + jnp.dot(mix, wout_ref[...], preferred_element_type=F32)

    kbuf[0:WINDOW, :] = kbuf[tile:tile + WINDOW, :]
    vbuf[0:WINDOW, :] = vbuf[tile:tile + WINDOW, :]
    ubuf[0:HIST_ROWS, :] = ubuf[tile:tile + HIST_ROWS, :]


def _const_spec(shape):
    nd = len(shape)
    return pl.BlockSpec(shape, lambda *_: (0,) * nd, pipeline_mode=pl.Buffered(1))


def _mixer_long(h, rope, kin, vin, uin, sinks, ln, win, pw, ps, wout, *, tile, base_pos, name):
    n_seq, seq_len, _ = h.shape
    shared_init = kin.shape[0] == 1
    init_idx = (lambda s, t: (0, 0, 0)) if shared_init else (lambda s, t: (s, 0, 0))
    kern = functools.partial(_mixer_long_kernel, tile=tile, base_pos=base_pos)
    return pl.pallas_call(
        kern,
        grid=(n_seq, seq_len // tile),
        in_specs=[
            pl.BlockSpec(memory_space=pltpu.SMEM),
            pl.BlockSpec((1, tile, D_MODEL), lambda s, t: (s, t, 0)),
            pl.BlockSpec((tile, 3 * LANES), lambda s, t: (t, 0)),
            pl.BlockSpec((1, WINDOW, KV_WIDTH), init_idx),
            pl.BlockSpec((1, WINDOW, KV_WIDTH), init_idx),
            pl.BlockSpec((1, HIST_ROWS, POOL_WIDTH), init_idx),
            _const_spec((1, D_MODEL)),
            _const_spec((D_MODEL, IN_WIDTH)),
            _const_spec((len(POOL_WINDOWS), POOL_GROUP, POOL_GROUP)),
            _const_spec((1, POOL_WIDTH)),
            _const_spec((D_MODEL, D_MODEL)),
        ],
        out_specs=[
            pl.BlockSpec((1, tile, D_MODEL), lambda s, t: (s, t, 0)),
            pl.BlockSpec((1, WINDOW, KV_WIDTH), lambda s, t: (s, 0, 0)),
            pl.BlockSpec((1, WINDOW, KV_WIDTH), lambda s, t: (s, 0, 0)),
            pl.BlockSpec((1, HIST_ROWS, POOL_WIDTH), lambda s, t: (s, 0, 0)),
        ],
        out_shape=[
            jax.ShapeDtypeStruct((n_seq, seq_len, D_MODEL), F32),
            jax.ShapeDtypeStruct((n_seq, WINDOW, KV_WIDTH), F32),
            jax.ShapeDtypeStruct((n_seq, WINDOW, KV_WIDTH), F32),
            jax.ShapeDtypeStruct((n_seq, HIST_ROWS, POOL_WIDTH), F32),
        ],
        scratch_shapes=[
            pltpu.VMEM((WINDOW + tile, KV_WIDTH), BF16),
            pltpu.VMEM((WINDOW + tile, KV_WIDTH), BF16),
            pltpu.VMEM((HIST_ROWS + tile, POOL_WIDTH), F32),
        ],
        compiler_params=pltpu.CompilerParams(
            dimension_semantics=("arbitrary", "arbitrary"), vmem_limit_bytes=VMEM_LIMIT),
        name=name,
    )(sinks, h, rope, kin, vin, uin, ln, win, pw, ps, wout)


def _mixer_short_kernel(sink_ref, h_ref, rope_ref, ck_ref, cv_ref, st_ref, ln_ref, win_ref,
                        pw_ref, ps_ref, wout_ref, ho_ref, kn_ref, vn_ref, un_ref,
                        qbuf, abuf, ubuf, *, seqs):
    rows = seqs * DEC_SEQ
    h = h_ref[...]
    qb, k, v, u = _project(h, ln_ref, win_ref, rope_ref[...])
    for j in range(Q_BLOCKS):
        qbuf[:, j * LANES:(j + 1) * LANES] = qb[j]
    kn_ref[...] = k
    vn_ref[...] = v
    un_ref[...] = u
    ubuf[:, 0:HIST_ROWS, :] = st_ref[...]
    ubuf[:, HIST_ROWS:HIST_ROWS + DEC_SEQ, :] = u.reshape(seqs, DEC_SEQ, POOL_WIDTH)

    stacked = N_HEADS * DEC_SEQ
    keys = WINDOW + 2 * DEC_SEQ
    lo = lax.broadcasted_iota(jnp.int32, (1, LANES), 1) < HEAD_DIM
    srow = lax.broadcasted_iota(jnp.int32, (stacked, keys), 0)
    kc = lax.broadcasted_iota(jnp.int32, (stacked, keys), 1)
    qi = jnp.bitwise_and(srow, DEC_SEQ - 1)
    bias = jnp.where((kc >= qi) & (kc <= qi + WINDOW), 0.0, NEG_INF)
    hrow = jnp.right_shift(lax.broadcasted_iota(jnp.int32, (stacked, 1), 0), DEC_SEQ.bit_length() - 1)
    sink = jnp.zeros((stacked, 1), F32)
    for hh in range(N_HEADS):
        sink = jnp.where(hrow == hh, sink_ref[hh], sink)
    zpad = jnp.zeros((DEC_SEQ, KV_WIDTH), F32)

    def seq_body(i, carry):
        r0 = pl.multiple_of(i * DEC_SEQ, DEC_SEQ)
        qrows = qbuf[pl.ds(r0, DEC_SEQ), :]
        q = _stack_heads([qrows[:, j * LANES:(j + 1) * LANES] for j in range(Q_BLOCKS)], lo)
        kall = jnp.concatenate([ck_ref[i], kn_ref[pl.ds(r0, DEC_SEQ), :], zpad], axis=0).astype(BF16)
        vall = jnp.concatenate([cv_ref[i], vn_ref[pl.ds(r0, DEC_SEQ), :], zpad], axis=0).astype(BF16)
        s = lax.dot_general(q, kall, (((1,), (1,)), ((), ())), preferred_element_type=F32)
        prob = _softmax_sink(s + bias, sink)
        o = jnp.dot(prob, vall, preferred_element_type=F32)
        abuf[pl.ds(r0, DEC_SEQ), :] = jnp.concatenate(
            [jnp.where(lo, o[j * DEC_SEQ:(j + 1) * DEC_SEQ],
                       o[(j + Q_BLOCKS) * DEC_SEQ:(j + Q_BLOCKS + 1) * DEC_SEQ])
             for j in range(Q_BLOCKS)], axis=-1)
        return carry

    lax.fori_loop(0, seqs, seq_body, 0)

    d_groups = []
    for g, w in enumerate(POOL_WINDOWS):
        cols = slice(g * POOL_GROUP, (g + 1) * POOL_GROUP)
        wsum = ubuf[:, HIST_ROWS:HIST_ROWS + DEC_SEQ, cols]
        for back in range(1, w):
            wsum = wsum + ubuf[:, HIST_ROWS - back:HIST_ROWS - back + DEC_SEQ, cols]
        cnt = float(min(PAST_LEN + 1, w))
        d_groups.append((wsum / cnt).reshape(rows, POOL_GROUP) - u[:, cols])
    pm = _pool_project(d_groups, pw_ref, ps_ref)

    mix = jnp.concatenate([abuf[...], pm], axis=-1).astype(BF16)
    ho_ref[...] = h + jnp.dot(mix, wout_ref[...], preferred_element_type=F32)


def _mixer_short(h, rope, ck, cv, st, sinks, ln, win, pw, ps, wout, *, seqs, name):
    n_rows = h.shape[0]
    rows = seqs * DEC_SEQ
    kern = functools.partial(_mixer_short_kernel, seqs=seqs)
    return pl.pallas_call(
        kern,
        grid=(n_rows // rows,),
        in_specs=[
            pl.BlockSpec(memory_space=pltpu.SMEM),
            pl.BlockSpec((rows, D_MODEL), lambda i: (i, 0)),
            _const_spec((rows, 3 * LANES)),
            pl.BlockSpec((seqs, WINDOW, KV_WIDTH), lambda i: (i, 0, 0)),
            pl.BlockSpec((seqs, WINDOW, KV_WIDTH), lambda i: (i, 0, 0)),
            pl.BlockSpec((seqs, HIST_ROWS, POOL_WIDTH), lambda i: (i, 0, 0)),
            _const_spec((1, D_MODEL)),
            _const_spec((D_MODEL, IN_WIDTH)),
            _const_spec((len(POOL_WINDOWS), POOL_GROUP, POOL_GROUP)),
            _const_spec((1, POOL_WIDTH)),
            _const_spec((D_MODEL, D_MODEL)),
        ],
        out_specs=[
            pl.BlockSpec((rows, D_MODEL), lambda i: (i, 0)),
            pl.BlockSpec((rows, KV_WIDTH), lambda i: (i, 0)),
            pl.BlockSpec((rows, KV_WIDTH), lambda i: (i, 0)),
            pl.BlockSpec((rows, POOL_WIDTH), lambda i: (i, 0)),
        ],
        out_shape=[
            jax.ShapeDtypeStruct((n_rows, D_MODEL), F32),
            jax.ShapeDtypeStruct((n_rows, KV_WIDTH), F32),
            jax.ShapeDtypeStruct((n_rows, KV_WIDTH), F32),
            jax.ShapeDtypeStruct((n_rows, POOL_WIDTH), F32),
        ],
        scratch_shapes=[
            pltpu.VMEM((rows, ATTN_WIDTH), F32),
            pltpu.VMEM((rows, ATTN_WIDTH), F32),
            pltpu.VMEM((seqs, HIST_ROWS + DEC_SEQ, POOL_WIDTH), F32),
        ],
        compiler_params=pltpu.CompilerParams(
            dimension_semantics=("arbitrary",), vmem_limit_bytes=VMEM_LIMIT),
        name=name,
    )(sinks, h, rope, ck, cv, st, ln, win, pw, ps, wout)


def _mlp_kernel(h_ref, ln_ref, wup_ref, wdown_ref, lnf_ref, o_ref, *, final_norm):
    h = h_ref[...]
    xn = _rmsnorm(h, ln_ref[...]).astype(BF16)
    a = jnp.maximum(jnp.dot(xn, wup_ref[...], preferred_element_type=F32), 0.0)
    out = h + jnp.dot((a * a).astype(BF16), wdown_ref[...], preferred_element_type=F32)
    if final_norm:
        out = _rmsnorm(out, lnf_ref[...])
    o_ref[...] = out


def _mlp(h, ln, wup, wdown, lnf, *, final_norm, name):
    n_rows = h.shape[0]
    tile = min(MLP_TILE, n_rows)
    kern = functools.partial(_mlp_kernel, final_norm=final_norm)
    return pl.pallas_call(
        kern,
        grid=(n_rows // tile,),
        in_specs=[
            pl.BlockSpec((tile, D_MODEL), lambda i: (i, 0)),
            _const_spec((1, D_MODEL)),
            _const_spec((D_MODEL, D_FF)),
            _const_spec((D_FF, D_MODEL)),
            _const_spec((1, D_MODEL)),
        ],
        out_specs=pl.BlockSpec((tile, D_MODEL), lambda i: (i, 0)),
        out_shape=jax.ShapeDtypeStruct((n_rows, D_MODEL), F32),
        compiler_params=pltpu.CompilerParams(
            dimension_semantics=("arbitrary",), vmem_limit_bytes=VMEM_LIMIT),
        name=name,
    )(h, ln, wup, wdown, lnf)


def _rope_table(pos):
    n = pos.shape[0]
    half = ROT_DIM // 2
    inv_freq = ROPE_THETA ** (-jnp.arange(0, ROT_DIM, 2, dtype=F32) / ROT_DIM)
    ang = pos.astype(F32)[:, None] * inv_freq[None, :]
    cos, sin = jnp.cos(ang), jnp.sin(ang)
    rest = HEAD_DIM - ROT_DIM
    zh = jnp.zeros((n, half), F32)
    c = jnp.concatenate([cos, cos, jnp.ones((n, rest), F32)], axis=-1)
    s1 = jnp.concatenate([-sin, zh, jnp.zeros((n, rest), F32)], axis=-1)
    s2 = jnp.concatenate([zh, sin, jnp.zeros((n, rest), F32)], axis=-1)
    reps = LANES // HEAD_DIM
    return jnp.concatenate([jnp.tile(c, (1, reps)), jnp.tile(s1, (1, reps)), jnp.tile(s2, (1, reps))], axis=-1)


def _head_pair_order():
    idx = []
    for j in range(Q_BLOCKS):
        idx += list(range(j * HEAD_DIM, (j + 1) * HEAD_DIM))
        idx += list(range((j + Q_BLOCKS) * HEAD_DIM, (j + Q_BLOCKS + 1) * HEAD_DIM))
    return jnp.asarray(idx, jnp.int32)


def kernel(x_prompt, x_sample, cache_k, cache_v, state_pool, meta_tokens, ln1, w_in, attn_sinks,
           pool_w, pool_scale, w_out, ln2, w_up, w_down, ln_f):
    batch, seq, _ = x_prompt.shape
    dec_batch, dec_seq, _ = x_sample.shape
    depth = w_in.shape[0]
    assert dec_seq == DEC_SEQ and seq % PROMPT_TILE == 0 and dec_batch % SAMPLE_SEQS == 0

    perm = _head_pair_order()
    pad = WINDOW - N_META
    rope_meta = _rope_table(jnp.arange(WINDOW) - pad)
    rope_prompt = _rope_table(N_META + jnp.arange(seq))
    rope_sample = jnp.tile(_rope_table(PAST_LEN + jnp.arange(DEC_SEQ)), (SAMPLE_SEQS, 1))

    hm = jnp.concatenate([jnp.zeros((pad, D_MODEL), F32), meta_tokens.astype(F32)], axis=0)[None]
    hp = x_prompt
    hs = x_sample.reshape(dec_batch * dec_seq, D_MODEL)
    zero_kv = jnp.zeros((1, WINDOW, KV_WIDTH), F32)
    zero_u = jnp.zeros((1, HIST_ROWS, POOL_WIDTH), F32)
    lnf = ln_f.reshape(1, D_MODEL)

    pk, pv, pu, sk, sv, su = [], [], [], [], [], []
    for l in range(depth):
        last = l == depth - 1
        win = jnp.concatenate([w_in[l][:, :ATTN_WIDTH][:, perm], w_in[l][:, ATTN_WIDTH:]], axis=1).astype(BF16)
        wout = jnp.concatenate([w_out[l][:ATTN_WIDTH][perm], w_out[l][ATTN_WIDTH:]], axis=0).astype(BF16)
        wup = w_up[l].astype(BF16)
        wdown = w_down[l].astype(BF16)
        pw = pool_w[l].astype(BF16)
        ps = pool_scale[l].reshape(1, POOL_WIDTH)
        l1 = ln1[l].reshape(1, D_MODEL)
        l2 = ln2[l].reshape(1, D_MODEL)
        sinks = attn_sinks[l].astype(F32)
        mixer_w = (sinks, l1, win, pw, ps, wout)

        hm_mid, km, vm, um = _mixer_long(hm, rope_meta, zero_kv, zero_kv, zero_u, *mixer_w,
                                         tile=WINDOW, base_pos=-pad, name=f"mixer_meta_{l}")
        hp_mid, kt, vt, ut = _mixer_long(hp, rope_prompt, km, vm, um, *mixer_w,
                                         tile=PROMPT_TILE, base_pos=N_META, name=f"mixer_prompt_{l}")
        st = jnp.pad(state_pool[l], ((0, 0), (1, 0), (0, 0)))
        ck = cache_k[l].reshape(dec_batch, WINDOW, KV_WIDTH)
        cv = cache_v[l].reshape(dec_batch, WINDOW, KV_WIDTH)
        hs_mid, kn, vn, un = _mixer_short(hs, rope_sample, ck, cv, st, *mixer_w,
                                          seqs=SAMPLE_SEQS, name=f"mixer_sample_{l}")

        if not last:
            hm = _mlp(hm_mid[0], l2, wup, wdown, lnf, final_norm=False, name=f"mlp_meta_{l}")[None]
        hp = _mlp(hp_mid.reshape(batch * seq, D_MODEL), l2, wup, wdown, lnf, final_norm=last,
                  name=f"mlp_prompt_{l}").reshape(batch, seq, D_MODEL)
        hs = _mlp(hs_mid, l2, wup, wdown, lnf, final_norm=last, name=f"mlp_sample_{l}")

        pk.append(kt.reshape(batch, WINDOW, N_KV_HEADS, HEAD_DIM))
        pv.append(vt.reshape(batch, WINDOW, N_KV_HEADS, HEAD_DIM))
        pu.append(ut[:, 1:, :])
        kn = kn.reshape(dec_batch, dec_seq, KV_WIDTH)
        vn = vn.reshape(dec_batch, dec_seq, KV_WIDTH)
        un = un.reshape(dec_batch, dec_seq, POOL_WIDTH)
        sk.append(jnp.concatenate([ck[:, dec_seq:], kn], axis=1).reshape(dec_batch, WINDOW, N_KV_HEADS, HEAD_DIM))
        sv.append(jnp.concatenate([cv[:, dec_seq:], vn], axis=1).reshape(dec_batch, WINDOW, N_KV_HEADS, HEAD_DIM))
        su.append(jnp.concatenate([state_pool[l][:, dec_seq:], un], axis=1))

    y_sample = hs.reshape(dec_batch, dec_seq, D_MODEL)
    return (hp, y_sample, jnp.stack(pk), jnp.stack(pv), jnp.stack(pu),
            jnp.stack(sk), jnp.stack(sv), jnp.stack(su))
```

```python
import functools

import jax
import jax.numpy as jnp
from jax import lax
from jax.experimental import pallas as pl
from jax.experimental.pallas import tpu as pltpu

D_MODEL = 1024
N_HEADS = 8
N_KV_HEADS = 2
HEAD_DIM = 64
ATTN_WIDTH = N_HEADS * HEAD_DIM
KV_WIDTH = N_KV_HEADS * HEAD_DIM
POOL_WINDOWS = (2, 4, 8, 16)
POOL_WIDTH = D_MODEL - ATTN_WIDTH
POOL_GROUP = POOL_WIDTH // len(POOL_WINDOWS)
POOL_HIST = max(POOL_WINDOWS) - 1
IN_WIDTH = ATTN_WIDTH + 2 * KV_WIDTH + POOL_WIDTH
WINDOW = 128
ROT_DIM = HEAD_DIM // 4
ROPE_THETA = 500000.0
D_FF = 4 * D_MODEL
N_META = 16
RMS_EPS = 1e-5
PAST_LEN = 16384
DEC_SEQ = 8

LANES = 128
HIST_ROWS = POOL_HIST + 1
Q_BLOCKS = ATTN_WIDTH // LANES
SCALE = HEAD_DIM ** -0.5
NEG_INF = float("-inf")

PROMPT_TILE = 256
SAMPLE_SEQS = 32
SAMPLE_UNROLL = 4
MLP_TILE = 512
VMEM_LIMIT = 56 * 1024 * 1024

F32 = jnp.float32
BF16 = jnp.bfloat16


def _rmsnorm(x, g):
    r = lax.rsqrt(jnp.mean(x * x, axis=-1, keepdims=True) + RMS_EPS)
    return x * r * g


def _rope(x, rope):
    c = rope[:, 0:LANES]
    s1 = rope[:, LANES:2 * LANES]
    s2 = rope[:, 2 * LANES:3 * LANES]
    half = ROT_DIM // 2
    return x * c + pltpu.roll(x, LANES - half, 1) * s1 + pltpu.roll(x, half, 1) * s2


def _project(h, ln_ref, win_ref, rope):
    xn = _rmsnorm(h, ln_ref[...]).astype(BF16)
    p = jnp.dot(xn, win_ref[...], preferred_element_type=F32)
    qb = [_rope(p[:, j * LANES:(j + 1) * LANES], rope) * SCALE for j in range(Q_BLOCKS)]
    k = _rope(p[:, ATTN_WIDTH:ATTN_WIDTH + KV_WIDTH], rope)
    v = p[:, ATTN_WIDTH + KV_WIDTH:ATTN_WIDTH + 2 * KV_WIDTH]
    u = p[:, ATTN_WIDTH + 2 * KV_WIDTH:]
    return qb, k, v, u


def _stack_heads(qrows, lo):
    zero = jnp.zeros_like(qrows[0])
    parts = [jnp.where(lo, q, zero) for q in qrows] + [jnp.where(lo, zero, q) for q in qrows]
    return jnp.concatenate(parts, axis=0).astype(BF16)


def _softmax_sink(s, sink):
    m = jnp.maximum(jnp.max(s, axis=-1, keepdims=True), sink)
    e = jnp.exp(s - m)
    l = jnp.sum(e, axis=-1, keepdims=True) + jnp.exp(sink - m)
    return (e * (1.0 / l)).astype(BF16)


def _window_sum(x, w, axis):
    span = 1
    while span < w:
        x = x + pltpu.roll(x, span, axis)
        span *= 2
    return x


def _pool_project(d_groups, pw_ref, ps_ref):
    outs = [jnp.dot(d.astype(BF16), pw_ref[g], preferred_element_type=F32)
            for g, d in enumerate(d_groups)]
    return jnp.concatenate(outs, axis=-1) * ps_ref[...]


def _mixer_long_kernel(sink_ref, h_ref, rope_ref, kin_ref, vin_ref, uin_ref, ln_ref, win_ref,
                       pw_ref, ps_ref, wout_ref, ho_ref, kt_ref, vt_ref, ut_ref,
                       kbuf, vbuf, ubuf, *, tile, base_pos, layer):
    t = pl.program_id(1)

    @pl.when(t == 0)
    def _():
        kbuf[0:WINDOW, :] = kin_ref[0].astype(BF16)
        vbuf[0:WINDOW, :] = vin_ref[0].astype(BF16)
        ubuf[0:HIST_ROWS, :] = uin_ref[0]

    h = h_ref[0]
    qb, k, v, u = _project(h, ln_ref, win_ref, rope_ref[...])
    kbuf[WINDOW:WINDOW + tile, :] = k.astype(BF16)
    vbuf[WINDOW:WINDOW + tile, :] = v.astype(BF16)
    ubuf[HIST_ROWS:HIST_ROWS + tile, :] = u
    kt_ref[0] = k[tile - WINDOW:, :]
    vt_ref[0] = v[tile - WINDOW:, :]
    ut_ref[0] = u[tile - HIST_ROWS:, :]

    p0 = base_pos + t * tile
    lo = lax.broadcasted_iota(jnp.int32, (1, LANES), 1) < HEAD_DIM
    qi = lax.broadcasted_iota(jnp.int32, (WINDOW, 2 * WINDOW), 0)
    kc = lax.broadcasted_iota(jnp.int32, (WINDOW, 2 * WINDOW), 1)
    band = (kc >= qi) & (kc <= qi + WINDOW)

    attn_rows = []
    for r in range(tile // WINDOW):
        rows = slice(r * WINDOW, (r + 1) * WINDOW)
        q = _stack_heads([qb[j][rows] for j in range(Q_BLOCKS)], lo)
        kb = kbuf[r * WINDOW:(r + 2) * WINDOW, :]
        vb = vbuf[r * WINDOW:(r + 2) * WINDOW, :]
        s = lax.dot_general(q, kb, (((1,), (1,)), ((), ())), preferred_element_type=F32)
        kpos = p0 + (r - 1) * WINDOW + kc
        bias = jnp.where(band & (kpos >= 0), 0.0, NEG_INF)
        probs = [_softmax_sink(s[hh * WINDOW:(hh + 1) * WINDOW] + bias, sink_ref[layer, hh])
                 for hh in range(N_HEADS)]
        o = jnp.dot(jnp.concatenate(probs, axis=0), vb, preferred_element_type=F32)
        attn_rows.append(jnp.concatenate(
            [jnp.where(lo, o[j * WINDOW:(j + 1) * WINDOW], o[(j + Q_BLOCKS) * WINDOW:(j + Q_BLOCKS + 1) * WINDOW])
             for j in range(Q_BLOCKS)], axis=-1))
    attn = jnp.concatenate(attn_rows, axis=0)

    pos = p0 + lax.broadcasted_iota(jnp.int32, (tile, 1), 0)
    d_groups = []
    for g, w in enumerate(POOL_WINDOWS):
        cols = slice(g * POOL_GROUP, (g + 1) * POOL_GROUP)
        wsum = _window_sum(ubuf[:, cols], w, 0)[HIST_ROWS:]
        cnt = jnp.clip(pos + 1, 1, w).astype(F32)
        d_groups.append(wsum / cnt - u[:, cols])
    pm = _pool_project(d_groups, pw_ref, ps_ref)

    mix = jnp.concatenate([attn, pm], axis=-1).astype(BF16)
    ho_ref[0] = h + jnp.dot(mix, wout_ref[...], preferred_element_type=F32)

    kbuf[0:WINDOW, :] = kbuf[tile:tile + WINDOW, :]
    vbuf[0:WINDOW, :] = vbuf[tile:tile + WINDOW, :]
    ubuf[0:HIST_ROWS, :] = ubuf[tile:tile + HIST_ROWS, :]


def _const_spec(shape):
    nd = len(shape)
    return pl.BlockSpec(shape, lambda *_: (0,) * nd, pipeline_mode=pl.Buffered(1))


def _layer_spec(shape, layer):
    nd = len(shape)
    return pl.BlockSpec((None,) + shape, lambda *_: (layer,) + (0,) * nd, pipeline_mode=pl.Buffered(1))


def _mixer_weight_specs(layer):
    return [
        _layer_spec((1, D_MODEL), layer),
        _layer_spec((D_MODEL, IN_WIDTH), layer),
        _layer_spec((len(POOL_WINDOWS), POOL_GROUP, POOL_GROUP), layer),
        _layer_spec((1, POOL_WIDTH), layer),
        _layer_spec((D_MODEL, D_MODEL), layer),
    ]


def _mixer_long(h, rope, kin, vin, uin, sinks, ln, win, pw, ps, wout, *, tile, base_pos, layer, name):
    n_seq, seq_len, _ = h.shape
    shared_init = kin.shape[0] == 1
    init_idx = (lambda s, t: (0, 0, 0)) if shared_init else (lambda s, t: (s, 0, 0))
    kern = functools.partial(_mixer_long_kernel, tile=tile, base_pos=base_pos, layer=layer)
    return pl.pallas_call(
        kern,
        grid=(n_seq, seq_len // tile),
        in_specs=[
            pl.BlockSpec(memory_space=pltpu.SMEM),
            pl.BlockSpec((1, tile, D_MODEL), lambda s, t: (s, t, 0)),
            pl.BlockSpec((tile, 3 * LANES), lambda s, t: (t, 0)),
            pl.BlockSpec((1, WINDOW, KV_WIDTH), init_idx),
            pl.BlockSpec((1, WINDOW, KV_WIDTH), init_idx),
            pl.BlockSpec((1, HIST_ROWS, POOL_WIDTH), init_idx),
        ] + _mixer_weight_specs(layer),
        out_specs=[
            pl.BlockSpec((1, tile, D_MODEL), lambda s, t: (s, t, 0)),
            pl.BlockSpec((1, WINDOW, KV_WIDTH), lambda s, t: (s, 0, 0)),
            pl.BlockSpec((1, WINDOW, KV_WIDTH), lambda s, t: (s, 0, 0)),
            pl.BlockSpec((1, HIST_ROWS, POOL_WIDTH), lambda s, t: (s, 0, 0)),
        ],
        out_shape=[
            jax.ShapeDtypeStruct((n_seq, seq_len, D_MODEL), F32),
            jax.ShapeDtypeStruct((n_seq, WINDOW, KV_WIDTH), F32),
            jax.ShapeDtypeStruct((n_seq, WINDOW, KV_WIDTH), F32),
            jax.ShapeDtypeStruct((n_seq, HIST_ROWS, POOL_WIDTH), F32),
        ],
        scratch_shapes=[
            pltpu.VMEM((WINDOW + tile, KV_WIDTH), BF16),
            pltpu.VMEM((WINDOW + tile, KV_WIDTH), BF16),
            pltpu.VMEM((HIST_ROWS + tile, POOL_WIDTH), F32),
        ],
        compiler_params=pltpu.CompilerParams(
            dimension_semantics=("arbitrary", "arbitrary"), vmem_limit_bytes=VMEM_LIMIT),
        name=name,
    )(sinks, h, rope, kin, vin, uin, ln, win, pw, ps, wout)


def _mixer_short_kernel(sink_ref, h_ref, rope_ref, ck_ref, cv_ref, st_ref, ln_ref, win_ref,
                        pw_ref, ps_ref, wout_ref, ho_ref, kn_ref, vn_ref, un_ref,
                        qbuf, abuf, ubuf, *, seqs, layer):
    rows = seqs * DEC_SEQ
    h = h_ref[...]
    qb, k, v, u = _project(h, ln_ref, win_ref, rope_ref[...])
    for j in range(Q_BLOCKS):
        qbuf[:, j * LANES:(j + 1) * LANES] = qb[j]
    kn_ref[...] = k
    vn_ref[...] = v
    un_ref[...] = u
    ubuf[:, 0:HIST_ROWS, :] = st_ref[...]
    ubuf[:, HIST_ROWS:HIST_ROWS + DEC_SEQ, :] = u.reshape(seqs, DEC_SEQ, POOL_WIDTH)

    stacked = N_HEADS * DEC_SEQ
    keys = WINDOW + 2 * DEC_SEQ
    lo = lax.broadcasted_iota(jnp.int32, (1, LANES), 1) < HEAD_DIM
    srow = lax.broadcasted_iota(jnp.int32, (stacked, keys), 0)
    kc = lax.broadcasted_iota(jnp.int32, (stacked, keys), 1)
    qi = jnp.bitwise_and(srow, DEC_SEQ - 1)
    bias = jnp.where((kc >= qi) & (kc <= qi + WINDOW), 0.0, NEG_INF)
    hrow = jnp.right_shift(lax.broadcasted_iota(jnp.int32, (stacked, 1), 0), DEC_SEQ.bit_length() - 1)
    sink = jnp.zeros((stacked, 1), F32)
    for hh in range(N_HEADS):
        sink = jnp.where(hrow == hh, sink_ref[layer, hh], sink)
    zpad = jnp.zeros((DEC_SEQ, KV_WIDTH), F32)

    def seq_body(i, carry):
        r0 = pl.multiple_of(i * DEC_SEQ, DEC_SEQ)
        qrows = qbuf[pl.ds(r0, DEC_SEQ), :]
        q = _stack_heads([qrows[:, j * LANES:(j + 1) * LANES] for j in range(Q_BLOCKS)], lo)
        kall = jnp.concatenate([ck_ref[i], kn_ref[pl.ds(r0, DEC_SEQ), :], zpad], axis=0).astype(BF16)
        vall = jnp.concatenate([cv_ref[i], vn_ref[pl.ds(r0, DEC_SEQ), :], zpad], axis=0).astype(BF16)
        s = lax.dot_general(q, kall, (((1,), (1,)), ((), ())), preferred_element_type=F32)
        prob = _softmax_sink(s + bias, sink)
        o = jnp.dot(prob, vall, preferred_element_type=F32)
        abuf[pl.ds(r0, DEC_SEQ), :] = jnp.concatenate(
            [jnp.where(lo, o[j * DEC_SEQ:(j + 1) * DEC_SEQ],
                       o[(j + Q_BLOCKS) * DEC_SEQ:(j + Q_BLOCKS + 1) * DEC_SEQ])
             for j in range(Q_BLOCKS)], axis=-1)
        return carry

    lax.fori_loop(0, seqs, seq_body, 0, unroll=SAMPLE_UNROLL)

    d_groups = []
    for g, w in enumerate(POOL_WINDOWS):
        cols = slice(g * POOL_GROUP, (g + 1) * POOL_GROUP)
        wsum = _window_sum(ubuf[:, :, cols], w, 1)[:, HIST_ROWS:, :]
        cnt = float(min(PAST_LEN + 1, w))
        d_groups.append((wsum / cnt).reshape(rows, POOL_GROUP) - u[:, cols])
    pm = _pool_project(d_groups, pw_ref, ps_ref)

    mix = jnp.concatenate([abuf[...], pm], axis=-1).astype(BF16)
    ho_ref[...] = h + jnp.dot(mix, wout_ref[...], preferred_element_type=F32)


def _mixer_short(h, rope, ck, cv, st, sinks, ln, win, pw, ps, wout, *, seqs, layer, name):
    n_rows = h.shape[0]
    rows = seqs * DEC_SEQ
    kern = functools.partial(_mixer_short_kernel, seqs=seqs, layer=layer)
    return pl.pallas_call(
        kern,
        grid=(n_rows // rows,),
        in_specs=[
            pl.BlockSpec(memory_space=pltpu.SMEM),
            pl.BlockSpec((rows, D_MODEL), lambda i: (i, 0)),
            _const_spec((rows, 3 * LANES)),
            pl.BlockSpec((seqs, WINDOW, KV_WIDTH), lambda i: (i, 0, 0)),
            pl.BlockSpec((seqs, WINDOW, KV_WIDTH), lambda i: (i, 0, 0)),
            pl.BlockSpec((seqs, HIST_ROWS, POOL_WIDTH), lambda i: (i, 0, 0)),
        ] + _mixer_weight_specs(layer),
        out_specs=[
            pl.BlockSpec((rows, D_MODEL), lambda i: (i, 0)),
            pl.BlockSpec((rows, KV_WIDTH), lambda i: (i, 0)),
            pl.BlockSpec((rows, KV_WIDTH), lambda i: (i, 0)),
            pl.BlockSpec((rows, POOL_WIDTH), lambda i: (i, 0)),
        ],
        out_shape=[
            jax.ShapeDtypeStruct((n_rows, D_MODEL), F32),
            jax.ShapeDtypeStruct((n_rows, KV_WIDTH), F32),
            jax.ShapeDtypeStruct((n_rows, KV_WIDTH), F32),
            jax.ShapeDtypeStruct((n_rows, POOL_WIDTH), F32),
        ],
        scratch_shapes=[
            pltpu.VMEM((rows, ATTN_WIDTH), F32),
            pltpu.VMEM((rows, ATTN_WIDTH), F32),
            pltpu.VMEM((seqs, HIST_ROWS + DEC_SEQ, POOL_WIDTH), F32),
        ],
        compiler_params=pltpu.CompilerParams(
            dimension_semantics=("arbitrary",), vmem_limit_bytes=VMEM_LIMIT),
        name=name,
    )(sinks, h, rope, ck, cv, st, ln, win, pw, ps, wout)


def _mlp_kernel(h_ref, ln_ref, wup_ref, wdown_ref, lnf_ref, o_ref, *, final_norm):
    h = h_ref[...]
    xn = _rmsnorm(h, ln_ref[...]).astype(BF16)
    a = jnp.maximum(jnp.dot(xn, wup_ref[...], preferred_element_type=F32), 0.0)
    out = h + jnp.dot((a * a).astype(BF16), wdown_ref[...], preferred_element_type=F32)
    if final_norm:
        out = _rmsnorm(out, lnf_ref[...])
    o_ref[...] = out


def _mlp(h, ln, wup, wdown, lnf, *, final_norm, layer, name):
    n_rows = h.shape[0]
    tile = min(MLP_TILE, n_rows)
    kern = functools.partial(_mlp_kernel, final_norm=final_norm)
    return pl.pallas_call(
        kern,
        grid=(n_rows // tile,),
        in_specs=[
            pl.BlockSpec((tile, D_MODEL), lambda i: (i, 0)),
            _layer_spec((1, D_MODEL), layer),
            _layer_spec((D_MODEL, D_FF), layer),
            _layer_spec((D_FF, D_MODEL), layer),
            _const_spec((1, D_MODEL)),
        ],
        out_specs=pl.BlockSpec((tile, D_MODEL), lambda i: (i, 0)),
        out_shape=jax.ShapeDtypeStruct((n_rows, D_MODEL), F32),
        compiler_params=pltpu.CompilerParams(
            dimension_semantics=("arbitrary",), vmem_limit_bytes=VMEM_LIMIT),
        name=name,
    )(h, ln, wup, wdown, lnf)


def _rope_table(pos):
    n = pos.shape[0]
    half = ROT_DIM // 2
    inv_freq = ROPE_THETA ** (-jnp.arange(0, ROT_DIM, 2, dtype=F32) / ROT_DIM)
    ang = pos.astype(F32)[:, None] * inv_freq[None, :]
    cos, sin = jnp.cos(ang), jnp.sin(ang)
    rest = HEAD_DIM - ROT_DIM
    zh = jnp.zeros((n, half), F32)
    c = jnp.concatenate([cos, cos, jnp.ones((n, rest), F32)], axis=-1)
    s1 = jnp.concatenate([-sin, zh, jnp.zeros((n, rest), F32)], axis=-1)
    s2 = jnp.concatenate([zh, sin, jnp.zeros((n, rest), F32)], axis=-1)
    reps = LANES // HEAD_DIM
    return jnp.concatenate([jnp.tile(c, (1, reps)), jnp.tile(s1, (1, reps)), jnp.tile(s2, (1, reps))], axis=-1)


def _pair_heads(w, axis):
    shape = w.shape
    split = shape[:axis] + (N_HEADS // Q_BLOCKS, Q_BLOCKS, HEAD_DIM) + shape[axis + 1:]
    return jnp.swapaxes(w.reshape(split), axis, axis + 1).reshape(shape)


def kernel(x_prompt, x_sample, cache_k, cache_v, state_pool, meta_tokens, ln1, w_in, attn_sinks,
           pool_w, pool_scale, w_out, ln2, w_up, w_down, ln_f):
    batch, seq, _ = x_prompt.shape
    dec_batch, dec_seq, _ = x_sample.shape
    depth = w_in.shape[0]
    assert dec_seq == DEC_SEQ and seq % PROMPT_TILE == 0 and dec_batch % SAMPLE_SEQS == 0

    pad = WINDOW - N_META
    rope_meta = _rope_table(jnp.arange(WINDOW) - pad)
    rope_prompt = _rope_table(N_META + jnp.arange(seq))
    rope_sample = jnp.tile(_rope_table(PAST_LEN + jnp.arange(DEC_SEQ)), (SAMPLE_SEQS, 1))

    win = jnp.concatenate([_pair_heads(w_in[:, :, :ATTN_WIDTH], 2), w_in[:, :, ATTN_WIDTH:]], axis=2).astype(BF16)
    wout = jnp.concatenate([_pair_heads(w_out[:, :ATTN_WIDTH], 1), w_out[:, ATTN_WIDTH:]], axis=1).astype(BF16)
    wup = w_up.astype(BF16)
    wdown = w_down.astype(BF16)
    pw = pool_w.astype(BF16)
    ps = pool_scale.reshape(depth, 1, POOL_WIDTH)
    l1 = ln1.reshape(depth, 1, D_MODEL)
    l2 = ln2.reshape(depth, 1, D_MODEL)
    lnf = ln_f.reshape(1, D_MODEL)
    sinks = attn_sinks.astype(F32)
    mixer_w = (sinks, l1, win, pw, ps, wout)

    hm = jnp.concatenate([jnp.zeros((pad, D_MODEL), F32), meta_tokens.astype(F32)], axis=0)[None]
    hp = x_prompt
    hs = x_sample.reshape(dec_batch * dec_seq, D_MODEL)
    zero_kv = jnp.zeros((1, WINDOW, KV_WIDTH), F32)
    zero_u = jnp.zeros((1, HIST_ROWS, POOL_WIDTH), F32)

    pk, pv, pu, sk, sv, su = [], [], [], [], [], []
    for l in range(depth):
        last = l == depth - 1
        hm_mid, km, vm, um = _mixer_long(hm, rope_meta, zero_kv, zero_kv, zero_u, *mixer_w, tile=WINDOW,
                                         base_pos=-pad, layer=l, name=f"mixer_meta_{l}")
        hp_mid, kt, vt, ut = _mixer_long(hp, rope_prompt, km, vm, um, *mixer_w, tile=PROMPT_TILE,
                                         base_pos=N_META, layer=l, name=f"mixer_prompt_{l}")
        st = jnp.pad(state_pool[l], ((0, 0), (1, 0), (0, 0)))
        ck = cache_k[l].reshape(dec_batch, WINDOW, KV_WIDTH)
        cv = cache_v[l].reshape(dec_batch, WINDOW, KV_WIDTH)
        hs_mid, kn, vn, un = _mixer_short(hs, rope_sample, ck, cv, st, *mixer_w, seqs=SAMPLE_SEQS,
                                          layer=l, name=f"mixer_sample_{l}")

        mlp_w = (l2, wup, wdown, lnf)
        if not last:
            hm = _mlp(hm_mid[0], *mlp_w, final_norm=False, layer=l, name=f"mlp_meta_{l}")[None]
        hp = _mlp(hp_mid.reshape(batch * seq, D_MODEL), *mlp_w, final_norm=last, layer=l,
                  name=f"mlp_prompt_{l}").reshape(batch, seq, D_MODEL)
        hs = _mlp(hs_mid, *mlp_w, final_norm=last, layer=l, name=f"mlp_sample_{l}")

        pk.append(kt.reshape(batch, WINDOW, N_KV_HEADS, HEAD_DIM))
        pv.append(vt.reshape(batch, WINDOW, N_KV_HEADS, HEAD_DIM))
        pu.append(ut[:, 1:, :])
        kn = kn.reshape(dec_batch, dec_seq, KV_WIDTH)
        vn = vn.reshape(dec_batch, dec_seq, KV_WIDTH)
        un = un.reshape(dec_batch, dec_seq, POOL_WIDTH)
        sk.append(jnp.concatenate([ck[:, dec_seq:], kn], axis=1).reshape(dec_batch, WINDOW, N_KV_HEADS, HEAD_DIM))
        sv.append(jnp.concatenate([cv[:, dec_seq:], vn], axis=1).reshape(dec_batch, WINDOW, N_KV_HEADS, HEAD_DIM))
        su.append(jnp.concatenate([state_pool[l][:, dec_seq:], un], axis=1))

    y_sample = hs.reshape(dec_batch, dec_seq, D_MODEL)
    return (hp, y_sample, jnp.stack(pk), jnp.stack(pv), jnp.stack(pu),
            jnp.stack(sk), jnp.stack(sv), jnp.stack(su))
```

```python
import functools

import jax
import jax.numpy as jnp
from jax import lax
from jax.experimental import pallas as pl
from jax.experimental.pallas import tpu as pltpu

D_MODEL = 1024
N_HEADS = 8
N_KV_HEADS = 2
HEAD_DIM = 64
ATTN_WIDTH = N_HEADS * HEAD_DIM
KV_WIDTH = N_KV_HEADS * HEAD_DIM
POOL_WINDOWS = (2, 4, 8, 16)
POOL_WIDTH = D_MODEL - ATTN_WIDTH
POOL_GROUP = POOL_WIDTH // len(POOL_WINDOWS)
POOL_HIST = max(POOL_WINDOWS) - 1
IN_WIDTH = ATTN_WIDTH + 2 * KV_WIDTH + POOL_WIDTH
WINDOW = 128
ROT_DIM = HEAD_DIM // 4
ROPE_THETA = 500000.0
D_FF = 4 * D_MODEL
N_META = 16
RMS_EPS = 1e-5
PAST_LEN = 16384
DEC_SEQ = 8

LANES = 128
HIST_ROWS = POOL_HIST + 1
Q_BLOCKS = ATTN_WIDTH // LANES
SCALE = HEAD_DIM ** -0.5
NEG_INF = float("-inf")

PROMPT_TILE = 256
SAMPLE_SEQS = 32
SAMPLE_UNROLL = 4
MLP_TILE = 512
VMEM_LIMIT = 56 * 1024 * 1024

F32 = jnp.float32
BF16 = jnp.bfloat16


def _rmsnorm(x, g):
    r = lax.rsqrt(jnp.mean(x * x, axis=-1, keepdims=True) + RMS_EPS)
    return x * r * g


def _rope(x, rope):
    c = rope[:, 0:LANES]
    s1 = rope[:, LANES:2 * LANES]
    s2 = rope[:, 2 * LANES:3 * LANES]
    half = ROT_DIM // 2
    return x * c + pltpu.roll(x, LANES - half, 1) * s1 + pltpu.roll(x, half, 1) * s2


def _in_proj(h, ln_ref, win_ref):
    xn = _rmsnorm(h, ln_ref[...]).astype(BF16)
    return jnp.dot(xn, win_ref[...], preferred_element_type=F32)


def _split_rope(p, rope):
    qb = [_rope(p[:, j * LANES:(j + 1) * LANES], rope) * SCALE for j in range(Q_BLOCKS)]
    k = _rope(p[:, ATTN_WIDTH:ATTN_WIDTH + KV_WIDTH], rope)
    v = p[:, ATTN_WIDTH + KV_WIDTH:ATTN_WIDTH + 2 * KV_WIDTH]
    u = p[:, ATTN_WIDTH + 2 * KV_WIDTH:]
    return qb, k, v, u


def _stack_heads(qrows, lo):
    zero = jnp.zeros_like(qrows[0])
    parts = [jnp.where(lo, q, zero) for q in qrows] + [jnp.where(lo, zero, q) for q in qrows]
    return jnp.concatenate(parts, axis=0).astype(BF16)


def _softmax_sink(s, sink):
    m = jnp.maximum(jnp.max(s, axis=-1, keepdims=True), sink)
    e = jnp.exp(s - m)
    l = jnp.sum(e, axis=-1, keepdims=True) + jnp.exp(sink - m)
    return e.astype(BF16), 1.0 / l


def _window_sum(x, w, axis):
    span = 1
    while span < w:
        x = x + pltpu.roll(x, span, axis)
        span *= 2
    return x


def _pool_project(d_groups, pw_ref, ps_ref):
    outs = [jnp.dot(d.astype(BF16), pw_ref[g], preferred_element_type=F32)
            for g, d in enumerate(d_groups)]
    return jnp.concatenate(outs, axis=-1) * ps_ref[...]


def _project_finish(p, rope_ref, kin_ref, vin_ref, uin_ref, pw_ref, ps_ref,
                    kt_ref, vt_ref, ut_ref, ubuf, dst, src, *, tile, first, p0):
    dq, dk, dv, dm = dst
    _, sk, sv, _ = src
    qb, k, v, u = _split_rope(p, rope_ref[...])
    lo = lax.broadcasted_iota(jnp.int32, (1, LANES), 1) < HEAD_DIM
    for r in range(tile // WINDOW):
        rows = slice(r * WINDOW, (r + 1) * WINDOW)
        dq[r] = _stack_heads([qb[j][rows] for j in range(Q_BLOCKS)], lo)
    dk[0:WINDOW, :] = jnp.where(first, kin_ref[0].astype(BF16), sk[tile:tile + WINDOW, :])
    dv[0:WINDOW, :] = jnp.where(first, vin_ref[0].astype(BF16), sv[tile:tile + WINDOW, :])
    dk[WINDOW:WINDOW + tile, :] = k.astype(BF16)
    dv[WINDOW:WINDOW + tile, :] = v.astype(BF16)
    kt_ref[0] = k[tile - WINDOW:, :]
    vt_ref[0] = v[tile - WINDOW:, :]
    ut_ref[0] = u[tile - HIST_ROWS:, :]

    ubuf[0:HIST_ROWS, :] = jnp.where(first, uin_ref[0], ubuf[tile:tile + HIST_ROWS, :])
    ubuf[HIST_ROWS:HIST_ROWS + tile, :] = u
    pos = p0 + lax.broadcasted_iota(jnp.int32, (tile, 1), 0)
    d_groups = []
    for g, w in enumerate(POOL_WINDOWS):
        cols = slice(g * POOL_GROUP, (g + 1) * POOL_GROUP)
        wsum = _window_sum(ubuf[:, cols], w, 0)[HIST_ROWS:]
        cnt = jnp.clip(pos + 1, 1, w).astype(F32)
        d_groups.append(wsum / cnt - u[:, cols])
    dm[...] = _pool_project(d_groups, pw_ref, ps_ref).astype(BF16)


def _attend_scores(src, *, tile):
    sq, sk, _, _ = src
    return [lax.dot_general(sq[r], sk[r * WINDOW:(r + 2) * WINDOW, :], (((1,), (1,)), ((), ())),
                            preferred_element_type=F32) for r in range(tile // WINDOW)]


def _attend_values(scores, sink_ref, src, *, layer, p0):
    _, _, sv, _ = src
    lo = lax.broadcasted_iota(jnp.int32, (1, LANES), 1) < HEAD_DIM
    qi = lax.broadcasted_iota(jnp.int32, (WINDOW, 2 * WINDOW), 0)
    kc = lax.broadcasted_iota(jnp.int32, (WINDOW, 2 * WINDOW), 1)
    band = (kc >= qi) & (kc <= qi + WINDOW)
    attn_rows = []
    for r, s in enumerate(scores):
        vb = sv[r * WINDOW:(r + 2) * WINDOW, :]
        kpos = p0 + (r - 1) * WINDOW + kc
        bias = jnp.where(band & (kpos >= 0), 0.0, NEG_INF)
        probs, inv = zip(*[_softmax_sink(s[hh * WINDOW:(hh + 1) * WINDOW] + bias, sink_ref[layer, hh])
                           for hh in range(N_HEADS)])
        o = jnp.dot(jnp.concatenate(probs, axis=0), vb, preferred_element_type=F32)
        heads = [o[hh * WINDOW:(hh + 1) * WINDOW] * inv[hh] for hh in range(N_HEADS)]
        attn_rows.append(jnp.concatenate(
            [jnp.where(lo, heads[j], heads[j + Q_BLOCKS]) for j in range(Q_BLOCKS)], axis=-1))
    return jnp.concatenate(attn_rows, axis=0).astype(BF16)


def _mixer_long_kernel(sink_ref, hp_ref, hr_ref, rope_ref, kin_ref, vin_ref, uin_ref, ln_ref, win_ref,
                       pw_ref, ps_ref, wout_ref, ho_ref, kt_ref, vt_ref, ut_ref,
                       qa, ka, va, ma, qb, kb, vb, mb, ubuf, *, tile, base_pos, layer, tiles_per_seq, n_tiles):
    i = pl.program_id(0)
    tp = lax.rem(jnp.minimum(i, n_tiles - 1), tiles_per_seq)
    ts = lax.rem(jnp.maximum(i - 1, 0), tiles_per_seq)
    set_a, set_b = (qa, ka, va, ma), (qb, kb, vb, mb)

    @pl.when(i == 0)
    def _():
        for ref in set_b + (ubuf,):
            ref[...] = jnp.zeros(ref.shape, ref.dtype)

    def step(dst, src):
        scores = _attend_scores(src, tile=tile)
        p = _in_proj(hp_ref[0], ln_ref, win_ref)
        attn = _attend_values(scores, sink_ref, src, layer=layer, p0=base_pos + ts * tile)
        _project_finish(p, rope_ref, kin_ref, vin_ref, uin_ref, pw_ref, ps_ref, kt_ref, vt_ref, ut_ref,
                        ubuf, dst, src, tile=tile, first=tp == 0, p0=base_pos + tp * tile)
        mix = jnp.concatenate([attn, src[3][...]], axis=-1)
        ho_ref[0] = hr_ref[0] + jnp.dot(mix, wout_ref[...], preferred_element_type=F32)

    parity = lax.rem(i, 2)

    @pl.when(parity == 0)
    def _():
        step(set_a, set_b)

    @pl.when(parity == 1)
    def _():
        step(set_b, set_a)


def _const_spec(shape):
    nd = len(shape)
    return pl.BlockSpec(shape, lambda *_: (0,) * nd, pipeline_mode=pl.Buffered(1))


def _layer_spec(shape, layer):
    nd = len(shape)
    return pl.BlockSpec((None,) + shape, lambda *_: (layer,) + (0,) * nd, pipeline_mode=pl.Buffered(1))


def _mixer_weight_specs(layer):
    return [
        _layer_spec((1, D_MODEL), layer),
        _layer_spec((D_MODEL, IN_WIDTH), layer),
        _layer_spec((len(POOL_WINDOWS), POOL_GROUP, POOL_GROUP), layer),
        _layer_spec((1, POOL_WIDTH), layer),
        _layer_spec((D_MODEL, D_MODEL), layer),
    ]


def _mixer_long(h, rope, kin, vin, uin, sinks, ln, win, pw, ps, wout, *, tile, base_pos, layer, name):
    n_seq, seq_len, _ = h.shape
    tps = seq_len // tile
    n_tiles = n_seq * tps
    shared_init = kin.shape[0] == 1

    def proj_tile(i):
        return jnp.minimum(i, n_tiles - 1)

    def attn_tile(i):
        return jnp.maximum(i - 1, 0)

    def init_idx(i):
        return (0 if shared_init else proj_tile(i) // tps, 0, 0)

    kern = functools.partial(_mixer_long_kernel, tile=tile, base_pos=base_pos, layer=layer,
                             tiles_per_seq=tps, n_tiles=n_tiles)
    buffer_set = [
        pltpu.VMEM((tile // WINDOW, N_HEADS * WINDOW, LANES), BF16),
        pltpu.VMEM((WINDOW + tile, KV_WIDTH), BF16),
        pltpu.VMEM((WINDOW + tile, KV_WIDTH), BF16),
        pltpu.VMEM((tile, POOL_WIDTH), BF16),
    ]
    return pl.pallas_call(
        kern,
        grid=(n_tiles + 1,),
        in_specs=[
            pl.BlockSpec(memory_space=pltpu.SMEM),
            pl.BlockSpec((1, tile, D_MODEL), lambda i: (proj_tile(i) // tps, proj_tile(i) % tps, 0)),
            pl.BlockSpec((1, tile, D_MODEL), lambda i: (attn_tile(i) // tps, attn_tile(i) % tps, 0)),
            pl.BlockSpec((tile, 3 * LANES), lambda i: (proj_tile(i) % tps, 0)),
            pl.BlockSpec((1, WINDOW, KV_WIDTH), init_idx),
            pl.BlockSpec((1, WINDOW, KV_WIDTH), init_idx),
            pl.BlockSpec((1, HIST_ROWS, POOL_WIDTH), init_idx),
        ] + _mixer_weight_specs(layer),
        out_specs=[
            pl.BlockSpec((1, tile, D_MODEL), lambda i: (attn_tile(i) // tps, attn_tile(i) % tps, 0)),
            pl.BlockSpec((1, WINDOW, KV_WIDTH), lambda i: (proj_tile(i) // tps, 0, 0)),
            pl.BlockSpec((1, WINDOW, KV_WIDTH), lambda i: (proj_tile(i) // tps, 0, 0)),
            pl.BlockSpec((1, HIST_ROWS, POOL_WIDTH), lambda i: (proj_tile(i) // tps, 0, 0)),
        ],
        out_shape=[
            jax.ShapeDtypeStruct((n_seq, seq_len, D_MODEL), F32),
            jax.ShapeDtypeStruct((n_seq, WINDOW, KV_WIDTH), F32),
            jax.ShapeDtypeStruct((n_seq, WINDOW, KV_WIDTH), F32),
            jax.ShapeDtypeStruct((n_seq, HIST_ROWS, POOL_WIDTH), F32),
        ],
        scratch_shapes=buffer_set + buffer_set + [pltpu.VMEM((HIST_ROWS + tile, POOL_WIDTH), F32)],
        compiler_params=pltpu.CompilerParams(
            dimension_semantics=("arbitrary",), vmem_limit_bytes=VMEM_LIMIT),
        name=name,
    )(sinks, h, h, rope, kin, vin, uin, ln, win, pw, ps, wout)


def _mixer_short_kernel(sink_ref, h_ref, rope_ref, ck_ref, cv_ref, st_ref, ln_ref, win_ref,
                        pw_ref, ps_ref, wout_ref, ho_ref, kn_ref, vn_ref, un_ref,
                        qbuf, abuf, ubuf, *, seqs, layer):
    rows = seqs * DEC_SEQ
    h = h_ref[...]
    qb, k, v, u = _split_rope(_in_proj(h, ln_ref, win_ref), rope_ref[...])
    for j in range(Q_BLOCKS):
        qbuf[:, j * LANES:(j + 1) * LANES] = qb[j]
    kn_ref[...] = k
    vn_ref[...] = v
    un_ref[...] = u
    ubuf[:, 0:HIST_ROWS, :] = st_ref[...]
    ubuf[:, HIST_ROWS:HIST_ROWS + DEC_SEQ, :] = u.reshape(seqs, DEC_SEQ, POOL_WIDTH)

    stacked = N_HEADS * DEC_SEQ
    keys = WINDOW + 2 * DEC_SEQ
    lo = lax.broadcasted_iota(jnp.int32, (1, LANES), 1) < HEAD_DIM
    srow = lax.broadcasted_iota(jnp.int32, (stacked, keys), 0)
    kc = lax.broadcasted_iota(jnp.int32, (stacked, keys), 1)
    qi = jnp.bitwise_and(srow, DEC_SEQ - 1)
    bias = jnp.where((kc >= qi) & (kc <= qi + WINDOW), 0.0, NEG_INF)
    hrow = jnp.right_shift(lax.broadcasted_iota(jnp.int32, (stacked, 1), 0), DEC_SEQ.bit_length() - 1)
    sink = jnp.zeros((stacked, 1), F32)
    for hh in range(N_HEADS):
        sink = jnp.where(hrow == hh, sink_ref[layer, hh], sink)
    zpad = jnp.zeros((DEC_SEQ, KV_WIDTH), F32)

    def seq_body(i, carry):
        r0 = pl.multiple_of(i * DEC_SEQ, DEC_SEQ)
        qrows = qbuf[pl.ds(r0, DEC_SEQ), :]
        q = _stack_heads([qrows[:, j * LANES:(j + 1) * LANES] for j in range(Q_BLOCKS)], lo)
        kall = jnp.concatenate([ck_ref[i], kn_ref[pl.ds(r0, DEC_SEQ), :], zpad], axis=0).astype(BF16)
        vall = jnp.concatenate([cv_ref[i], vn_ref[pl.ds(r0, DEC_SEQ), :], zpad], axis=0).astype(BF16)
        s = lax.dot_general(q, kall, (((1,), (1,)), ((), ())), preferred_element_type=F32)
        prob, inv = _softmax_sink(s + bias, sink)
        o = jnp.dot(prob, vall, preferred_element_type=F32) * inv
        abuf[pl.ds(r0, DEC_SEQ), :] = jnp.concatenate(
            [jnp.where(lo, o[j * DEC_SEQ:(j + 1) * DEC_SEQ],
                       o[(j + Q_BLOCKS) * DEC_SEQ:(j + Q_BLOCKS + 1) * DEC_SEQ])
             for j in range(Q_BLOCKS)], axis=-1)
        return carry

    lax.fori_loop(0, seqs, seq_body, 0, unroll=SAMPLE_UNROLL)

    d_groups = []
    for g, w in enumerate(POOL_WINDOWS):
        cols = slice(g * POOL_GROUP, (g + 1) * POOL_GROUP)
        wsum = _window_sum(ubuf[:, :, cols], w, 1)[:, HIST_ROWS:, :]
        cnt = float(min(PAST_LEN + 1, w))
        d_groups.append((wsum / cnt).reshape(rows, POOL_GROUP) - u[:, cols])
    pm = _pool_project(d_groups, pw_ref, ps_ref)

    mix = jnp.concatenate([abuf[...], pm], axis=-1).astype(BF16)
    ho_ref[...] = h + jnp.dot(mix, wout_ref[...], preferred_element_type=F32)


def _mixer_short(h, rope, ck, cv, st, sinks, ln, win, pw, ps, wout, *, seqs, layer, name):
    n_rows = h.shape[0]
    rows = seqs * DEC_SEQ
    kern = functools.partial(_mixer_short_kernel, seqs=seqs, layer=layer)
    return pl.pallas_call(
        kern,
        grid=(n_rows // rows,),
        in_specs=[
            pl.BlockSpec(memory_space=pltpu.SMEM),
            pl.BlockSpec((rows, D_MODEL), lambda i: (i, 0)),
            _const_spec((rows, 3 * LANES)),
            pl.BlockSpec((seqs, WINDOW, KV_WIDTH), lambda i: (i, 0, 0)),
            pl.BlockSpec((seqs, WINDOW, KV_WIDTH), lambda i: (i, 0, 0)),
            pl.BlockSpec((seqs, HIST_ROWS, POOL_WIDTH), lambda i: (i, 0, 0)),
        ] + _mixer_weight_specs(layer),
        out_specs=[
            pl.BlockSpec((rows, D_MODEL), lambda i: (i, 0)),
            pl.BlockSpec((rows, KV_WIDTH), lambda i: (i, 0)),
            pl.BlockSpec((rows, KV_WIDTH), lambda i: (i, 0)),
            pl.BlockSpec((rows, POOL_WIDTH), lambda i: (i, 0)),
        ],
        out_shape=[
            jax.ShapeDtypeStruct((n_rows, D_MODEL), F32),
            jax.ShapeDtypeStruct((n_rows, KV_WIDTH), F32),
            jax.ShapeDtypeStruct((n_rows, KV_WIDTH), F32),
            jax.ShapeDtypeStruct((n_rows, POOL_WIDTH), F32),
        ],
        scratch_shapes=[
            pltpu.VMEM((rows, ATTN_WIDTH), F32),
            pltpu.VMEM((rows, ATTN_WIDTH), F32),
            pltpu.VMEM((seqs, HIST_ROWS + DEC_SEQ, POOL_WIDTH), F32),
        ],
        compiler_params=pltpu.CompilerParams(
            dimension_semantics=("arbitrary",), vmem_limit_bytes=VMEM_LIMIT),
        name=name,
    )(sinks, h, rope, ck, cv, st, ln, win, pw, ps, wout)


def _mlp_kernel(h_ref, ln_ref, wup_ref, wdown_ref, lnf_ref, o_ref, *, final_norm):
    h = h_ref[...]
    xn = _rmsnorm(h, ln_ref[...]).astype(BF16)
    a = jnp.maximum(jnp.dot(xn, wup_ref[...], preferred_element_type=F32), 0.0)
    out = h + jnp.dot((a * a).astype(BF16), wdown_ref[...], preferred_element_type=F32)
    if final_norm:
        out = _rmsnorm(out, lnf_ref[...])
    o_ref[...] = out


def _mlp(h, ln, wup, wdown, lnf, *, final_norm, layer, name):
    n_rows = h.shape[0]
    tile = min(MLP_TILE, n_rows)
    kern = functools.partial(_mlp_kernel, final_norm=final_norm)
    return pl.pallas_call(
        kern,
        grid=(n_rows // tile,),
        in_specs=[
            pl.BlockSpec((tile, D_MODEL), lambda i: (i, 0)),
            _layer_spec((1, D_MODEL), layer),
            _layer_spec((D_MODEL, D_FF), layer),
            _layer_spec((D_FF, D_MODEL), layer),
            _const_spec((1, D_MODEL)),
        ],
        out_specs=pl.BlockSpec((tile, D_MODEL), lambda i: (i, 0)),
        out_shape=jax.ShapeDtypeStruct((n_rows, D_MODEL), F32),
        compiler_params=pltpu.CompilerParams(
            dimension_semantics=("arbitrary",), vmem_limit_bytes=VMEM_LIMIT),
        name=name,
    )(h, ln, wup, wdown, lnf)


def _rope_table(pos):
    n = pos.shape[0]
    half = ROT_DIM // 2
    inv_freq = ROPE_THETA ** (-jnp.arange(0, ROT_DIM, 2, dtype=F32) / ROT_DIM)
    ang = pos.astype(F32)[:, None] * inv_freq[None, :]
    cos, sin = jnp.cos(ang), jnp.sin(ang)
    rest = HEAD_DIM - ROT_DIM
    zh = jnp.zeros((n, half), F32)
    c = jnp.concatenate([cos, cos, jnp.ones((n, rest), F32)], axis=-1)
    s1 = jnp.concatenate([-sin, zh, jnp.zeros((n, rest), F32)], axis=-1)
    s2 = jnp.concatenate([zh, sin, jnp.zeros((n, rest), F32)], axis=-1)
    reps = LANES // HEAD_DIM
    return jnp.concatenate([jnp.tile(c, (1, reps)), jnp.tile(s1, (1, reps)), jnp.tile(s2, (1, reps))], axis=-1)


def _pair_heads(w, axis):
    shape = w.shape
    split = shape[:axis] + (N_HEADS // Q_BLOCKS, Q_BLOCKS, HEAD_DIM) + shape[axis + 1:]
    return jnp.swapaxes(w.reshape(split), axis, axis + 1).reshape(shape)


def kernel(x_prompt, x_sample, cache_k, cache_v, state_pool, meta_tokens, ln1, w_in, attn_sinks,
           pool_w, pool_scale, w_out, ln2, w_up, w_down, ln_f):
    batch, seq, _ = x_prompt.shape
    dec_batch, dec_seq, _ = x_sample.shape
    depth = w_in.shape[0]
    assert dec_seq == DEC_SEQ and seq % PROMPT_TILE == 0 and dec_batch % SAMPLE_SEQS == 0

    pad = WINDOW - N_META
    rope_meta = _rope_table(jnp.arange(WINDOW) - pad)
    rope_prompt = _rope_table(N_META + jnp.arange(seq))
    rope_sample = jnp.tile(_rope_table(PAST_LEN + jnp.arange(DEC_SEQ)), (SAMPLE_SEQS, 1))

    win = jnp.concatenate([_pair_heads(w_in[:, :, :ATTN_WIDTH], 2), w_in[:, :, ATTN_WIDTH:]], axis=2).astype(BF16)
    wout = jnp.concatenate([_pair_heads(w_out[:, :ATTN_WIDTH], 1), w_out[:, ATTN_WIDTH:]], axis=1).astype(BF16)
    wup = w_up.astype(BF16)
    wdown = w_down.astype(BF16)
    pw = pool_w.astype(BF16)
    ps = pool_scale.reshape(depth, 1, POOL_WIDTH)
    l1 = ln1.reshape(depth, 1, D_MODEL)
    l2 = ln2.reshape(depth, 1, D_MODEL)
    lnf = ln_f.reshape(1, D_MODEL)
    sinks = attn_sinks.astype(F32)
    mixer_w = (sinks, l1, win, pw, ps, wout)

    hm = jnp.concatenate([jnp.zeros((pad, D_MODEL), F32), meta_tokens.astype(F32)], axis=0)[None]
    hp = x_prompt
    hs = x_sample.reshape(dec_batch * dec_seq, D_MODEL)
    zero_kv = jnp.zeros((1, WINDOW, KV_WIDTH), F32)
    zero_u = jnp.zeros((1, HIST_ROWS, POOL_WIDTH), F32)

    pk, pv, pu, sk, sv, su = [], [], [], [], [], []
    for l in range(depth):
        last = l == depth - 1
        hm_mid, km, vm, um = _mixer_long(hm, rope_meta, zero_kv, zero_kv, zero_u, *mixer_w, tile=WINDOW,
                                         base_pos=-pad, layer=l, name=f"mixer_meta_{l}")
        hp_mid, kt, vt, ut = _mixer_long(hp, rope_prompt, km, vm, um, *mixer_w, tile=PROMPT_TILE,
                                         base_pos=N_META, layer=l, name=f"mixer_prompt_{l}")
        st = jnp.pad(state_pool[l], ((0, 0), (1, 0), (0, 0)))
        ck = cache_k[l].reshape(dec_batch, WINDOW, KV_WIDTH)
        cv = cache_v[l].reshape(dec_batch, WINDOW, KV_WIDTH)
        hs_mid, kn, vn, un = _mixer_short(hs, rope_sample, ck, cv, st, *mixer_w, seqs=SAMPLE_SEQS,
                                          layer=l, name=f"mixer_sample_{l}")

        mlp_w = (l2, wup, wdown, lnf)
        if not last:
            hm = _mlp(hm_mid[0], *mlp_w, final_norm=False, layer=l, name=f"mlp_meta_{l}")[None]
        hp = _mlp(hp_mid.reshape(batch * seq, D_MODEL), *mlp_w, final_norm=last, layer=l,
                  name=f"mlp_prompt_{l}").reshape(batch, seq, D_MODEL)
        hs = _mlp(hs_mid, *mlp_w, final_norm=last, layer=l, name=f"mlp_sample_{l}")

        pk.append(kt.reshape(batch, WINDOW, N_KV_HEADS, HEAD_DIM))
        pv.append(vt.reshape(batch, WINDOW, N_KV_HEADS, HEAD_DIM))
        pu.append(ut[:, 1:, :])
        kn = kn.reshape(dec_batch, dec_seq, KV_WIDTH)
        vn = vn.reshape(dec_batch, dec_seq, KV_WIDTH)
        un = un.reshape(dec_batch, dec_seq, POOL_WIDTH)
        sk.append(jnp.concatenate([ck[:, dec_seq:], kn], axis=1).reshape(dec_batch, WINDOW, N_KV_HEADS, HEAD_DIM))
        sv.append(jnp.concatenate([cv[:, dec_seq:], vn], axis=1).reshape(dec_batch, WINDOW, N_KV_HEADS, HEAD_DIM))
        su.append(jnp.concatenate([state_pool[l][:, dec_seq:], un], axis=1))

    y_sample = hs.reshape(dec_batch, dec_seq, D_MODEL)
    return (hp, y_sample, jnp.stack(pk), jnp.stack(pv), jnp.stack(pu),
            jnp.stack(sk), jnp.stack(sv), jnp.stack(su))
```

```python
import functools

import jax
import jax.numpy as jnp
from jax import lax
from jax.experimental import pallas as pl
from jax.experimental.pallas import tpu as pltpu

D_MODEL = 1024
N_HEADS = 8
N_KV_HEADS = 2
HEAD_DIM = 64
ATTN_WIDTH = N_HEADS * HEAD_DIM
KV_WIDTH = N_KV_HEADS * HEAD_DIM
POOL_WINDOWS = (2, 4, 8, 16)
POOL_WIDTH = D_MODEL - ATTN_WIDTH
POOL_GROUP = POOL_WIDTH // len(POOL_WINDOWS)
POOL_HIST = max(POOL_WINDOWS) - 1
IN_WIDTH = ATTN_WIDTH + 2 * KV_WIDTH + POOL_WIDTH
WINDOW = 128
ROT_DIM = HEAD_DIM // 4
ROPE_THETA = 500000.0
D_FF = 4 * D_MODEL
N_META = 16
RMS_EPS = 1e-5
PAST_LEN = 16384
DEC_SEQ = 8

LANES = 128
HIST_ROWS = POOL_HIST + 1
Q_BLOCKS = ATTN_WIDTH // LANES
SCALE = HEAD_DIM ** -0.5
NEG_INF = float("-inf")

PROMPT_TILE = 256
SAMPLE_SEQS = 32
SAMPLE_UNROLL = 4
MLP_TILE = 512
VMEM_LIMIT = 56 * 1024 * 1024

F32 = jnp.float32
BF16 = jnp.bfloat16


def _rmsnorm(x, g):
    r = lax.rsqrt(jnp.mean(x * x, axis=-1, keepdims=True) + RMS_EPS)
    return x * r * g


def _rope(x, rope):
    c = rope[:, 0:LANES]
    s1 = rope[:, LANES:2 * LANES]
    s2 = rope[:, 2 * LANES:3 * LANES]
    half = ROT_DIM // 2
    return x * c + pltpu.roll(x, LANES - half, 1) * s1 + pltpu.roll(x, half, 1) * s2


def _in_proj(h, ln_ref, win_ref):
    xn = _rmsnorm(h, ln_ref[...]).astype(BF16)
    return jnp.dot(xn, win_ref[...], preferred_element_type=F32)


def _split_rope(p, rope):
    qb = [_rope(p[:, j * LANES:(j + 1) * LANES], rope) * SCALE for j in range(Q_BLOCKS)]
    k = _rope(p[:, ATTN_WIDTH:ATTN_WIDTH + KV_WIDTH], rope)
    v = p[:, ATTN_WIDTH + KV_WIDTH:ATTN_WIDTH + 2 * KV_WIDTH]
    u = p[:, ATTN_WIDTH + 2 * KV_WIDTH:]
    return qb, k, v, u


def _stack_heads(qrows, lo):
    zero = jnp.zeros_like(qrows[0])
    parts = [jnp.where(lo, q, zero) for q in qrows] + [jnp.where(lo, zero, q) for q in qrows]
    return jnp.concatenate(parts, axis=0).astype(BF16)


def _softmax_sink(s, sink):
    m = jnp.maximum(jnp.max(s, axis=-1, keepdims=True), sink)
    e = jnp.exp(s - m)
    l = jnp.sum(e, axis=-1, keepdims=True) + jnp.exp(sink - m)
    return e.astype(BF16), 1.0 / l


def _window_sum(x, w, axis):
    span = 1
    while span < w:
        x = x + pltpu.roll(x, span, axis)
        span *= 2
    return x


def _pool_project(d_groups, pw_ref, ps_ref):
    outs = [jnp.dot(d.astype(BF16), pw_ref[g], preferred_element_type=F32)
            for g, d in enumerate(d_groups)]
    return jnp.concatenate(outs, axis=-1) * ps_ref[...]


def _project_finish(p, rope_ref, kin_ref, vin_ref, uin_ref, pw_ref, ps_ref,
                    kt_ref, vt_ref, ut_ref, ubuf, dst, src, *, tile, first, p0):
    dq, dk, dv, dm, _ = dst
    _, sk, sv, _, _ = src
    qb, k, v, u = _split_rope(p, rope_ref[...])
    lo = lax.broadcasted_iota(jnp.int32, (1, LANES), 1) < HEAD_DIM
    for r in range(tile // WINDOW):
        rows = slice(r * WINDOW, (r + 1) * WINDOW)
        dq[r] = _stack_heads([qb[j][rows] for j in range(Q_BLOCKS)], lo)
    dk[0:WINDOW, :] = jnp.where(first, kin_ref[0].astype(BF16), sk[tile:tile + WINDOW, :])
    dv[0:WINDOW, :] = jnp.where(first, vin_ref[0].astype(BF16), sv[tile:tile + WINDOW, :])
    dk[WINDOW:WINDOW + tile, :] = k.astype(BF16)
    dv[WINDOW:WINDOW + tile, :] = v.astype(BF16)
    kt_ref[0] = k[tile - WINDOW:, :]
    vt_ref[0] = v[tile - WINDOW:, :]
    ut_ref[0] = u[tile - HIST_ROWS:, :]

    ubuf[0:HIST_ROWS, :] = jnp.where(first, uin_ref[0], ubuf[tile:tile + HIST_ROWS, :])
    ubuf[HIST_ROWS:HIST_ROWS + tile, :] = u
    pos = p0 + lax.broadcasted_iota(jnp.int32, (tile, 1), 0)
    d_groups = []
    for g, w in enumerate(POOL_WINDOWS):
        cols = slice(g * POOL_GROUP, (g + 1) * POOL_GROUP)
        wsum = _window_sum(ubuf[:, cols], w, 0)[HIST_ROWS:]
        cnt = jnp.clip(pos + 1, 1, w).astype(F32)
        d_groups.append(wsum / cnt - u[:, cols])
    dm[...] = _pool_project(d_groups, pw_ref, ps_ref).astype(BF16)


def _attend_scores(src, *, tile):
    sq, sk, _, _, _ = src
    return [lax.dot_general(sq[r], sk[r * WINDOW:(r + 2) * WINDOW, :], (((1,), (1,)), ((), ())),
                            preferred_element_type=F32) for r in range(tile // WINDOW)]


def _attend_values(scores, sink_ref, src, *, layer, p0):
    _, _, sv, _, _ = src
    lo = lax.broadcasted_iota(jnp.int32, (1, LANES), 1) < HEAD_DIM
    qi = lax.broadcasted_iota(jnp.int32, (WINDOW, 2 * WINDOW), 0)
    kc = lax.broadcasted_iota(jnp.int32, (WINDOW, 2 * WINDOW), 1)
    band = (kc >= qi) & (kc <= qi + WINDOW)
    attn_rows = []
    for r, s in enumerate(scores):
        vb = sv[r * WINDOW:(r + 2) * WINDOW, :]
        kpos = p0 + (r - 1) * WINDOW + kc
        bias = jnp.where(band & (kpos >= 0), 0.0, NEG_INF)
        probs, inv = zip(*[_softmax_sink(s[hh * WINDOW:(hh + 1) * WINDOW] + bias, sink_ref[layer, hh])
                           for hh in range(N_HEADS)])
        o = jnp.dot(jnp.concatenate(probs, axis=0), vb, preferred_element_type=F32)
        heads = [o[hh * WINDOW:(hh + 1) * WINDOW] * inv[hh] for hh in range(N_HEADS)]
        attn_rows.append(jnp.concatenate(
            [jnp.where(lo, heads[j], heads[j + Q_BLOCKS]) for j in range(Q_BLOCKS)], axis=-1))
    return jnp.concatenate(attn_rows, axis=0).astype(BF16)


def _mixer_long_kernel(sink_ref, hp_ref, hr_ref, rope_ref, kin_ref, vin_ref, uin_ref, ln_ref, win_ref,
                       pw_ref, ps_ref, wout_ref, ho_ref, kt_ref, vt_ref, ut_ref,
                       qa, ka, va, ma, xa, qb, kb, vb, mb, xb, ubuf,
                       *, tile, base_pos, layer, tiles_per_seq, n_tiles):
    i = pl.program_id(0)
    tp = lax.rem(jnp.minimum(i, n_tiles - 1), tiles_per_seq)
    ts = lax.rem(jnp.clip(i - 1, 0, n_tiles - 1), tiles_per_seq)
    set_a, set_b = (qa, ka, va, ma, xa), (qb, kb, vb, mb, xb)

    @pl.when(i == 0)
    def _():
        for ref in set_b + (xa, ubuf):
            ref[...] = jnp.zeros(ref.shape, ref.dtype)

    def step(dst, src):
        scores = _attend_scores(src, tile=tile)
        p = _in_proj(hp_ref[0], ln_ref, win_ref)
        ho_ref[0] = hr_ref[0] + jnp.dot(dst[4][...], wout_ref[...], preferred_element_type=F32)
        attn = _attend_values(scores, sink_ref, src, layer=layer, p0=base_pos + ts * tile)
        src[4][...] = jnp.concatenate([attn, src[3][...]], axis=-1)
        _project_finish(p, rope_ref, kin_ref, vin_ref, uin_ref, pw_ref, ps_ref, kt_ref, vt_ref, ut_ref,
                        ubuf, dst, src, tile=tile, first=tp == 0, p0=base_pos + tp * tile)

    parity = lax.rem(i, 2)

    @pl.when(parity == 0)
    def _():
        step(set_a, set_b)

    @pl.when(parity == 1)
    def _():
        step(set_b, set_a)


def _const_spec(shape):
    nd = len(shape)
    return pl.BlockSpec(shape, lambda *_: (0,) * nd, pipeline_mode=pl.Buffered(1))


def _layer_spec(shape, layer):
    nd = len(shape)
    return pl.BlockSpec((None,) + shape, lambda *_: (layer,) + (0,) * nd, pipeline_mode=pl.Buffered(1))


def _mixer_weight_specs(layer):
    return [
        _layer_spec((1, D_MODEL), layer),
        _layer_spec((D_MODEL, IN_WIDTH), layer),
        _layer_spec((len(POOL_WINDOWS), POOL_GROUP, POOL_GROUP), layer),
        _layer_spec((1, POOL_WIDTH), layer),
        _layer_spec((D_MODEL, D_MODEL), layer),
    ]


def _mixer_long(h, rope, kin, vin, uin, sinks, ln, win, pw, ps, wout, *, tile, base_pos, layer, name):
    n_seq, seq_len, _ = h.shape
    tps = seq_len // tile
    n_tiles = n_seq * tps
    shared_init = kin.shape[0] == 1

    def proj_tile(i):
        return jnp.minimum(i, n_tiles - 1)

    def out_tile(i):
        return jnp.maximum(i - 2, 0)

    def init_idx(i):
        return (0 if shared_init else proj_tile(i) // tps, 0, 0)

    kern = functools.partial(_mixer_long_kernel, tile=tile, base_pos=base_pos, layer=layer,
                             tiles_per_seq=tps, n_tiles=n_tiles)
    buffer_set = [
        pltpu.VMEM((tile // WINDOW, N_HEADS * WINDOW, LANES), BF16),
        pltpu.VMEM((WINDOW + tile, KV_WIDTH), BF16),
        pltpu.VMEM((WINDOW + tile, KV_WIDTH), BF16),
        pltpu.VMEM((tile, POOL_WIDTH), BF16),
        pltpu.VMEM((tile, D_MODEL), BF16),
    ]
    return pl.pallas_call(
        kern,
        grid=(n_tiles + 2,),
        in_specs=[
            pl.BlockSpec(memory_space=pltpu.SMEM),
            pl.BlockSpec((1, tile, D_MODEL), lambda i: (proj_tile(i) // tps, proj_tile(i) % tps, 0)),
            pl.BlockSpec((1, tile, D_MODEL), lambda i: (out_tile(i) // tps, out_tile(i) % tps, 0)),
            pl.BlockSpec((tile, 3 * LANES), lambda i: (proj_tile(i) % tps, 0)),
            pl.BlockSpec((1, WINDOW, KV_WIDTH), init_idx),
            pl.BlockSpec((1, WINDOW, KV_WIDTH), init_idx),
            pl.BlockSpec((1, HIST_ROWS, POOL_WIDTH), init_idx),
        ] + _mixer_weight_specs(layer),
        out_specs=[
            pl.BlockSpec((1, tile, D_MODEL), lambda i: (out_tile(i) // tps, out_tile(i) % tps, 0)),
            pl.BlockSpec((1, WINDOW, KV_WIDTH), lambda i: (proj_tile(i) // tps, 0, 0)),
            pl.BlockSpec((1, WINDOW, KV_WIDTH), lambda i: (proj_tile(i) // tps, 0, 0)),
            pl.BlockSpec((1, HIST_ROWS, POOL_WIDTH), lambda i: (proj_tile(i) // tps, 0, 0)),
        ],
        out_shape=[
            jax.ShapeDtypeStruct((n_seq, seq_len, D_MODEL), F32),
            jax.ShapeDtypeStruct((n_seq, WINDOW, KV_WIDTH), F32),
            jax.ShapeDtypeStruct((n_seq, WINDOW, KV_WIDTH), F32),
            jax.ShapeDtypeStruct((n_seq, HIST_ROWS, POOL_WIDTH), F32),
        ],
        scratch_shapes=buffer_set + buffer_set + [pltpu.VMEM((HIST_ROWS + tile, POOL_WIDTH), F32)],
        compiler_params=pltpu.CompilerParams(
            dimension_semantics=("arbitrary",), vmem_limit_bytes=VMEM_LIMIT),
        name=name,
    )(sinks, h, h, rope, kin, vin, uin, ln, win, pw, ps, wout)


def _mixer_short_kernel(sink_ref, h_ref, rope_ref, ck_ref, cv_ref, st_ref, ln_ref, win_ref,
                        pw_ref, ps_ref, wout_ref, ho_ref, kc_ref, vc_ref, un_ref,
                        qbuf, abuf, ubuf, *, seqs, layer):
    rows = seqs * DEC_SEQ
    h = h_ref[...]
    qb, k, v, u = _split_rope(_in_proj(h, ln_ref, win_ref), rope_ref[...])
    for j in range(Q_BLOCKS):
        qbuf[:, j * LANES:(j + 1) * LANES] = qb[j]
    kept = WINDOW - DEC_SEQ
    kc_ref[:, 0:kept, :] = ck_ref[:, DEC_SEQ:, :]
    vc_ref[:, 0:kept, :] = cv_ref[:, DEC_SEQ:, :]
    kc_ref[:, kept:, :] = k.reshape(seqs, DEC_SEQ, KV_WIDTH)
    vc_ref[:, kept:, :] = v.reshape(seqs, DEC_SEQ, KV_WIDTH)
    un_ref[...] = u
    ubuf[:, 0:HIST_ROWS, :] = st_ref[...]
    ubuf[:, HIST_ROWS:HIST_ROWS + DEC_SEQ, :] = u.reshape(seqs, DEC_SEQ, POOL_WIDTH)

    stacked = N_HEADS * DEC_SEQ
    keys = WINDOW + 2 * DEC_SEQ
    lo = lax.broadcasted_iota(jnp.int32, (1, LANES), 1) < HEAD_DIM
    srow = lax.broadcasted_iota(jnp.int32, (stacked, keys), 0)
    kc = lax.broadcasted_iota(jnp.int32, (stacked, keys), 1)
    qi = jnp.bitwise_and(srow, DEC_SEQ - 1)
    bias = jnp.where((kc >= qi) & (kc <= qi + WINDOW), 0.0, NEG_INF)
    hrow = jnp.right_shift(lax.broadcasted_iota(jnp.int32, (stacked, 1), 0), DEC_SEQ.bit_length() - 1)
    sink = jnp.zeros((stacked, 1), F32)
    for hh in range(N_HEADS):
        sink = jnp.where(hrow == hh, sink_ref[layer, hh], sink)
    zpad = jnp.zeros((DEC_SEQ, KV_WIDTH), F32)

    def seq_body(i, carry):
        r0 = pl.multiple_of(i * DEC_SEQ, DEC_SEQ)
        qrows = qbuf[pl.ds(r0, DEC_SEQ), :]
        q = _stack_heads([qrows[:, j * LANES:(j + 1) * LANES] for j in range(Q_BLOCKS)], lo)
        kall = jnp.concatenate([ck_ref[i], kc_ref[i, kept:, :], zpad], axis=0).astype(BF16)
        vall = jnp.concatenate([cv_ref[i], vc_ref[i, kept:, :], zpad], axis=0).astype(BF16)
        s = lax.dot_general(q, kall, (((1,), (1,)), ((), ())), preferred_element_type=F32)
        prob, inv = _softmax_sink(s + bias, sink)
        o = jnp.dot(prob, vall, preferred_element_type=F32) * inv
        abuf[pl.ds(r0, DEC_SEQ), :] = jnp.concatenate(
            [jnp.where(lo, o[j * DEC_SEQ:(j + 1) * DEC_SEQ],
                       o[(j + Q_BLOCKS) * DEC_SEQ:(j + Q_BLOCKS + 1) * DEC_SEQ])
             for j in range(Q_BLOCKS)], axis=-1)
        return carry

    lax.fori_loop(0, seqs, seq_body, 0, unroll=SAMPLE_UNROLL)

    d_groups = []
    for g, w in enumerate(POOL_WINDOWS):
        cols = slice(g * POOL_GROUP, (g + 1) * POOL_GROUP)
        wsum = _window_sum(ubuf[:, :, cols], w, 1)[:, HIST_ROWS:, :]
        cnt = float(min(PAST_LEN + 1, w))
        d_groups.append((wsum / cnt).reshape(rows, POOL_GROUP) - u[:, cols])
    pm = _pool_project(d_groups, pw_ref, ps_ref)

    mix = jnp.concatenate([abuf[...], pm], axis=-1).astype(BF16)
    ho_ref[...] = h + jnp.dot(mix, wout_ref[...], preferred_element_type=F32)


def _mixer_short(h, rope, ck, cv, st, sinks, ln, win, pw, ps, wout, *, seqs, layer, name):
    n_rows = h.shape[0]
    rows = seqs * DEC_SEQ
    kern = functools.partial(_mixer_short_kernel, seqs=seqs, layer=layer)
    return pl.pallas_call(
        kern,
        grid=(n_rows // rows,),
        in_specs=[
            pl.BlockSpec(memory_space=pltpu.SMEM),
            pl.BlockSpec((rows, D_MODEL), lambda i: (i, 0)),
            _const_spec((rows, 3 * LANES)),
            pl.BlockSpec((None, seqs, WINDOW, KV_WIDTH), lambda i: (layer, i, 0, 0)),
            pl.BlockSpec((None, seqs, WINDOW, KV_WIDTH), lambda i: (layer, i, 0, 0)),
            pl.BlockSpec((seqs, HIST_ROWS, POOL_WIDTH), lambda i: (i, 0, 0)),
        ] + _mixer_weight_specs(layer),
        out_specs=[
            pl.BlockSpec((rows, D_MODEL), lambda i: (i, 0)),
            pl.BlockSpec((seqs, WINDOW, KV_WIDTH), lambda i: (i, 0, 0)),
            pl.BlockSpec((seqs, WINDOW, KV_WIDTH), lambda i: (i, 0, 0)),
            pl.BlockSpec((rows, POOL_WIDTH), lambda i: (i, 0)),
        ],
        out_shape=[
            jax.ShapeDtypeStruct((n_rows, D_MODEL), F32),
            jax.ShapeDtypeStruct((n_rows // DEC_SEQ, WINDOW, KV_WIDTH), F32),
            jax.ShapeDtypeStruct((n_rows // DEC_SEQ, WINDOW, KV_WIDTH), F32),
            jax.ShapeDtypeStruct((n_rows, POOL_WIDTH), F32),
        ],
        scratch_shapes=[
            pltpu.VMEM((rows, ATTN_WIDTH), F32),
            pltpu.VMEM((rows, ATTN_WIDTH), F32),
            pltpu.VMEM((seqs, HIST_ROWS + DEC_SEQ, POOL_WIDTH), F32),
        ],
        compiler_params=pltpu.CompilerParams(
            dimension_semantics=("arbitrary",), vmem_limit_bytes=VMEM_LIMIT),
        name=name,
    )(sinks, h, rope, ck, cv, st, ln, win, pw, ps, wout)


def _mlp_kernel(h_ref, ln_ref, wup_ref, wdown_ref, lnf_ref, o_ref, *, final_norm):
    h = h_ref[...]
    xn = _rmsnorm(h, ln_ref[...]).astype(BF16)
    a = jnp.maximum(jnp.dot(xn, wup_ref[...], preferred_element_type=F32), 0.0)
    out = h + jnp.dot((a * a).astype(BF16), wdown_ref[...], preferred_element_type=F32)
    if final_norm:
        out = _rmsnorm(out, lnf_ref[...])
    o_ref[...] = out


def _mlp(h, ln, wup, wdown, lnf, *, final_norm, layer, name):
    n_rows = h.shape[0]
    tile = min(MLP_TILE, n_rows)
    kern = functools.partial(_mlp_kernel, final_norm=final_norm)
    return pl.pallas_call(
        kern,
        grid=(n_rows // tile,),
        in_specs=[
            pl.BlockSpec((tile, D_MODEL), lambda i: (i, 0)),
            _layer_spec((1, D_MODEL), layer),
            _layer_spec((D_MODEL, D_FF), layer),
            _layer_spec((D_FF, D_MODEL), layer),
            _const_spec((1, D_MODEL)),
        ],
        out_specs=pl.BlockSpec((tile, D_MODEL), lambda i: (i, 0)),
        out_shape=jax.ShapeDtypeStruct((n_rows, D_MODEL), F32),
        compiler_params=pltpu.CompilerParams(
            dimension_semantics=("arbitrary",), vmem_limit_bytes=VMEM_LIMIT),
        name=name,
    )(h, ln, wup, wdown, lnf)


def _rope_table(pos):
    n = pos.shape[0]
    half = ROT_DIM // 2
    inv_freq = ROPE_THETA ** (-jnp.arange(0, ROT_DIM, 2, dtype=F32) / ROT_DIM)
    ang = pos.astype(F32)[:, None] * inv_freq[None, :]
    cos, sin = jnp.cos(ang), jnp.sin(ang)
    rest = HEAD_DIM - ROT_DIM
    zh = jnp.zeros((n, half), F32)
    c = jnp.concatenate([cos, cos, jnp.ones((n, rest), F32)], axis=-1)
    s1 = jnp.concatenate([-sin, zh, jnp.zeros((n, rest), F32)], axis=-1)
    s2 = jnp.concatenate([zh, sin, jnp.zeros((n, rest), F32)], axis=-1)
    reps = LANES // HEAD_DIM
    return jnp.concatenate([jnp.tile(c, (1, reps)), jnp.tile(s1, (1, reps)), jnp.tile(s2, (1, reps))], axis=-1)


def _pair_heads(w, axis):
    shape = w.shape
    split = shape[:axis] + (N_HEADS // Q_BLOCKS, Q_BLOCKS, HEAD_DIM) + shape[axis + 1:]
    return jnp.swapaxes(w.reshape(split), axis, axis + 1).reshape(shape)


def kernel(x_prompt, x_sample, cache_k, cache_v, state_pool, meta_tokens, ln1, w_in, attn_sinks,
           pool_w, pool_scale, w_out, ln2, w_up, w_down, ln_f):
    batch, seq, _ = x_prompt.shape
    dec_batch, dec_seq, _ = x_sample.shape
    depth = w_in.shape[0]
    assert dec_seq == DEC_SEQ and seq % PROMPT_TILE == 0 and dec_batch % SAMPLE_SEQS == 0

    pad = WINDOW - N_META
    rope_meta = _rope_table(jnp.arange(WINDOW) - pad)
    rope_prompt = _rope_table(N_META + jnp.arange(seq))
    rope_sample = jnp.tile(_rope_table(PAST_LEN + jnp.arange(DEC_SEQ)), (SAMPLE_SEQS, 1))

    win = jnp.concatenate([_pair_heads(w_in[:, :, :ATTN_WIDTH], 2), w_in[:, :, ATTN_WIDTH:]], axis=2).astype(BF16)
    wout = jnp.concatenate([_pair_heads(w_out[:, :ATTN_WIDTH], 1), w_out[:, ATTN_WIDTH:]], axis=1).astype(BF16)
    wup = w_up.astype(BF16)
    wdown = w_down.astype(BF16)
    pw = pool_w.astype(BF16)
    ps = pool_scale.reshape(depth, 1, POOL_WIDTH)
    l1 = ln1.reshape(depth, 1, D_MODEL)
    l2 = ln2.reshape(depth, 1, D_MODEL)
    lnf = ln_f.reshape(1, D_MODEL)
    sinks = attn_sinks.astype(F32)
    mixer_w = (sinks, l1, win, pw, ps, wout)

    hm = jnp.concatenate([jnp.zeros((pad, D_MODEL), F32), meta_tokens.astype(F32)], axis=0)[None]
    hp = x_prompt
    hs = x_sample.reshape(dec_batch * dec_seq, D_MODEL)
    zero_kv = jnp.zeros((1, WINDOW, KV_WIDTH), F32)
    zero_u = jnp.zeros((1, HIST_ROWS, POOL_WIDTH), F32)
    ck = cache_k.reshape(depth, dec_batch, WINDOW, KV_WIDTH)
    cv = cache_v.reshape(depth, dec_batch, WINDOW, KV_WIDTH)

    pk, pv, pu, sk, sv, su = [], [], [], [], [], []
    for l in range(depth):
        last = l == depth - 1
        hm_mid, km, vm, um = _mixer_long(hm, rope_meta, zero_kv, zero_kv, zero_u, *mixer_w, tile=WINDOW,
                                         base_pos=-pad, layer=l, name=f"mixer_meta_{l}")
        hp_mid, kt, vt, ut = _mixer_long(hp, rope_prompt, km, vm, um, *mixer_w, tile=PROMPT_TILE,
                                         base_pos=N_META, layer=l, name=f"mixer_prompt_{l}")
        st = jnp.pad(state_pool[l], ((0, 0), (1, 0), (0, 0)))
        hs_mid, kc, vc, un = _mixer_short(hs, rope_sample, ck, cv, st, *mixer_w, seqs=SAMPLE_SEQS,
                                          layer=l, name=f"mixer_sample_{l}")

        mlp_w = (l2, wup, wdown, lnf)
        if not last:
            hm = _mlp(hm_mid[0], *mlp_w, final_norm=False, layer=l, name=f"mlp_meta_{l}")[None]
        hp = _mlp(hp_mid.reshape(batch * seq, D_MODEL), *mlp_w, final_norm=last, layer=l,
                  name=f"mlp_prompt_{l}").reshape(batch, seq, D_MODEL)
        hs = _mlp(hs_mid, *mlp_w, final_norm=last, layer=l, name=f"mlp_sample_{l}")

        pk.append(kt.reshape(batch, WINDOW, N_KV_HEADS, HEAD_DIM))
        pv.append(vt.reshape(batch, WINDOW, N_KV_HEADS, HEAD_DIM))
        pu.append(ut[:, 1:, :])
        un = un.reshape(dec_batch, dec_seq, POOL_WIDTH)
        sk.append(kc.reshape(dec_batch, WINDOW, N_KV_HEADS, HEAD_DIM))
        sv.append(vc.reshape(dec_batch, WINDOW, N_KV_HEADS, HEAD_DIM))
        su.append(jnp.concatenate([state_pool[l][:, dec_seq:], un], axis=1))

    y_sample = hs.reshape(dec_batch, dec_seq, D_MODEL)
    return (hp, y_sample, jnp.stack(pk), jnp.stack(pv), jnp.stack(pu),
            jnp.stack(sk), jnp.stack(sv), jnp.stack(su))
```

```python
import functools

import jax
import jax.numpy as jnp
from jax import lax
from jax.experimental import pallas as pl
from jax.experimental.pallas import tpu as pltpu

D_MODEL = 1024
N_HEADS = 8
N_KV_HEADS = 2
HEAD_DIM = 64
ATTN_WIDTH = N_HEADS * HEAD_DIM
KV_WIDTH = N_KV_HEADS * HEAD_DIM
POOL_WINDOWS = (2, 4, 8, 16)
POOL_WIDTH = D_MODEL - ATTN_WIDTH
POOL_GROUP = POOL_WIDTH // len(POOL_WINDOWS)
POOL_HIST = max(POOL_WINDOWS) - 1
IN_WIDTH = ATTN_WIDTH + 2 * KV_WIDTH + POOL_WIDTH
WINDOW = 128
ROT_DIM = HEAD_DIM // 4
ROPE_THETA = 500000.0
D_FF = 4 * D_MODEL
N_META = 16
RMS_EPS = 1e-5
PAST_LEN = 16384
DEC_SEQ = 8

LANES = 128
HIST_ROWS = POOL_HIST + 1
Q_BLOCKS = ATTN_WIDTH // LANES
SCALE = HEAD_DIM ** -0.5
NEG_INF = float("-inf")

PROMPT_TILE = 512
SAMPLE_SEQS = 32
SAMPLE_UNROLL = 4
MLP_TILE = 512
VMEM_LIMIT = 56 * 1024 * 1024

F32 = jnp.float32
BF16 = jnp.bfloat16


def _rmsnorm(x, g):
    r = lax.rsqrt(jnp.mean(x * x, axis=-1, keepdims=True) + RMS_EPS)
    return x * r * g


def _rope(x, rope):
    c = rope[:, 0:LANES]
    s1 = rope[:, LANES:2 * LANES]
    s2 = rope[:, 2 * LANES:3 * LANES]
    half = ROT_DIM // 2
    return x * c + pltpu.roll(x, LANES - half, 1) * s1 + pltpu.roll(x, half, 1) * s2


def _in_proj(h, ln_ref, win_ref):
    xn = _rmsnorm(h, ln_ref[...]).astype(BF16)
    return jnp.dot(xn, win_ref[...], preferred_element_type=F32)


def _split_rope(p, rope):
    qb = [_rope(p[:, j * LANES:(j + 1) * LANES], rope) * SCALE for j in range(Q_BLOCKS)]
    k = _rope(p[:, ATTN_WIDTH:ATTN_WIDTH + KV_WIDTH], rope)
    v = p[:, ATTN_WIDTH + KV_WIDTH:ATTN_WIDTH + 2 * KV_WIDTH]
    u = p[:, ATTN_WIDTH + 2 * KV_WIDTH:]
    return qb, k, v, u


def _stack_heads(qrows, lo):
    zero = jnp.zeros_like(qrows[0])
    parts = [jnp.where(lo, q, zero) for q in qrows] + [jnp.where(lo, zero, q) for q in qrows]
    return jnp.concatenate(parts, axis=0).astype(BF16)


def _softmax_sink(s, sink):
    m = jnp.maximum(jnp.max(s, axis=-1, keepdims=True), sink)
    e = jnp.exp(s - m)
    l = jnp.sum(e, axis=-1, keepdims=True) + jnp.exp(sink - m)
    return e.astype(BF16), 1.0 / l


def _window_sum(x, w, axis):
    span = 1
    while span < w:
        x = x + pltpu.roll(x, span, axis)
        span *= 2
    return x


def _pool_project(d_groups, pw_ref, ps_ref):
    outs = [jnp.dot(d.astype(BF16), pw_ref[g], preferred_element_type=F32)
            for g, d in enumerate(d_groups)]
    return jnp.concatenate(outs, axis=-1) * ps_ref[...]


def _project_finish(p, rope_ref, kin_ref, vin_ref, uin_ref, pw_ref, ps_ref,
                    kt_ref, vt_ref, ut_ref, ubuf, dst, src, *, tile, first, p0):
    dq, dk, dv, dm, _ = dst
    _, sk, sv, _, _ = src
    qb, k, v, u = _split_rope(p, rope_ref[...])
    lo = lax.broadcasted_iota(jnp.int32, (1, LANES), 1) < HEAD_DIM
    for r in range(tile // WINDOW):
        rows = slice(r * WINDOW, (r + 1) * WINDOW)
        dq[r] = _stack_heads([qb[j][rows] for j in range(Q_BLOCKS)], lo)
    dk[0:WINDOW, :] = jnp.where(first, kin_ref[0].astype(BF16), sk[tile:tile + WINDOW, :])
    dv[0:WINDOW, :] = jnp.where(first, vin_ref[0].astype(BF16), sv[tile:tile + WINDOW, :])
    dk[WINDOW:WINDOW + tile, :] = k.astype(BF16)
    dv[WINDOW:WINDOW + tile, :] = v.astype(BF16)
    kt_ref[0] = k[tile - WINDOW:, :]
    vt_ref[0] = v[tile - WINDOW:, :]
    ut_ref[0] = u[tile - HIST_ROWS:, :]

    ubuf[0:HIST_ROWS, :] = jnp.where(first, uin_ref[0], ubuf[tile:tile + HIST_ROWS, :])
    ubuf[HIST_ROWS:HIST_ROWS + tile, :] = u
    pos = p0 + lax.broadcasted_iota(jnp.int32, (tile, 1), 0)
    d_groups = []
    for g, w in enumerate(POOL_WINDOWS):
        cols = slice(g * POOL_GROUP, (g + 1) * POOL_GROUP)
        wsum = _window_sum(ubuf[:, cols], w, 0)[HIST_ROWS:]
        cnt = jnp.clip(pos + 1, 1, w).astype(F32)
        d_groups.append(wsum / cnt - u[:, cols])
    dm[...] = _pool_project(d_groups, pw_ref, ps_ref).astype(BF16)


def _attend_scores(src, *, tile):
    sq, sk, _, _, _ = src
    return [lax.dot_general(sq[r], sk[r * WINDOW:(r + 2) * WINDOW, :], (((1,), (1,)), ((), ())),
                            preferred_element_type=F32) for r in range(tile // WINDOW)]


def _attend_values(scores, sink_ref, src, *, layer, p0):
    _, _, sv, _, _ = src
    lo = lax.broadcasted_iota(jnp.int32, (1, LANES), 1) < HEAD_DIM
    qi = lax.broadcasted_iota(jnp.int32, (WINDOW, 2 * WINDOW), 0)
    kc = lax.broadcasted_iota(jnp.int32, (WINDOW, 2 * WINDOW), 1)
    band = (kc >= qi) & (kc <= qi + WINDOW)
    attn_rows = []
    for r, s in enumerate(scores):
        vb = sv[r * WINDOW:(r + 2) * WINDOW, :]
        kpos = p0 + (r - 1) * WINDOW + kc
        bias = jnp.where(band & (kpos >= 0), 0.0, NEG_INF)
        probs, inv = zip(*[_softmax_sink(s[hh * WINDOW:(hh + 1) * WINDOW] + bias, sink_ref[layer, hh])
                           for hh in range(N_HEADS)])
        o = jnp.dot(jnp.concatenate(probs, axis=0), vb, preferred_element_type=F32)
        heads = [o[hh * WINDOW:(hh + 1) * WINDOW] * inv[hh] for hh in range(N_HEADS)]
        attn_rows.append(jnp.concatenate(
            [jnp.where(lo, heads[j], heads[j + Q_BLOCKS]) for j in range(Q_BLOCKS)], axis=-1))
    return jnp.concatenate(attn_rows, axis=0).astype(BF16)


def _mixer_long_kernel(sink_ref, hp_ref, hr_ref, rope_ref, kin_ref, vin_ref, uin_ref, ln_ref, win_ref,
                       pw_ref, ps_ref, wout_ref, ho_ref, kt_ref, vt_ref, ut_ref,
                       qa, ka, va, ma, xa, qb, kb, vb, mb, xb, ubuf,
                       *, tile, base_pos, layer, tiles_per_seq, n_tiles):
    i = pl.program_id(0)
    tp = lax.rem(jnp.minimum(i, n_tiles - 1), tiles_per_seq)
    ts = lax.rem(jnp.clip(i - 1, 0, n_tiles - 1), tiles_per_seq)
    set_a, set_b = (qa, ka, va, ma, xa), (qb, kb, vb, mb, xb)

    @pl.when(i == 0)
    def _():
        for ref in set_b + (xa, ubuf):
            ref[...] = jnp.zeros(ref.shape, ref.dtype)

    def step(dst, src):
        scores = _attend_scores(src, tile=tile)
        p = _in_proj(hp_ref[0], ln_ref, win_ref)
        ho_ref[0] = hr_ref[0] + jnp.dot(dst[4][...], wout_ref[...], preferred_element_type=F32)
        attn = _attend_values(scores, sink_ref, src, layer=layer, p0=base_pos + ts * tile)
        src[4][...] = jnp.concatenate([attn, src[3][...]], axis=-1)
        _project_finish(p, rope_ref, kin_ref, vin_ref, uin_ref, pw_ref, ps_ref, kt_ref, vt_ref, ut_ref,
                        ubuf, dst, src, tile=tile, first=tp == 0, p0=base_pos + tp * tile)

    parity = lax.rem(i, 2)

    @pl.when(parity == 0)
    def _():
        step(set_a, set_b)

    @pl.when(parity == 1)
    def _():
        step(set_b, set_a)


def _const_spec(shape):
    nd = len(shape)
    return pl.BlockSpec(shape, lambda *_: (0,) * nd, pipeline_mode=pl.Buffered(1))


def _layer_spec(shape, layer):
    nd = len(shape)
    return pl.BlockSpec((None,) + shape, lambda *_: (layer,) + (0,) * nd, pipeline_mode=pl.Buffered(1))


def _mixer_weight_specs(layer):
    return [
        _layer_spec((1, D_MODEL), layer),
        _layer_spec((D_MODEL, IN_WIDTH), layer),
        _layer_spec((len(POOL_WINDOWS), POOL_GROUP, POOL_GROUP), layer),
        _layer_spec((1, POOL_WIDTH), layer),
        _layer_spec((D_MODEL, D_MODEL), layer),
    ]


def _mixer_long(h, rope, kin, vin, uin, sinks, ln, win, pw, ps, wout, *, tile, base_pos, layer, name):
    n_seq, seq_len, _ = h.shape
    tps = seq_len // tile
    n_tiles = n_seq * tps
    shared_init = kin.shape[0] == 1

    def proj_tile(i):
        return jnp.minimum(i, n_tiles - 1)

    def out_tile(i):
        return jnp.maximum(i - 2, 0)

    def init_idx(i):
        return (0 if shared_init else proj_tile(i) // tps, 0, 0)

    kern = functools.partial(_mixer_long_kernel, tile=tile, base_pos=base_pos, layer=layer,
                             tiles_per_seq=tps, n_tiles=n_tiles)
    buffer_set = [
        pltpu.VMEM((tile // WINDOW, N_HEADS * WINDOW, LANES), BF16),
        pltpu.VMEM((WINDOW + tile, KV_WIDTH), BF16),
        pltpu.VMEM((WINDOW + tile, KV_WIDTH), BF16),
        pltpu.VMEM((tile, POOL_WIDTH), BF16),
        pltpu.VMEM((tile, D_MODEL), BF16),
    ]
    return pl.pallas_call(
        kern,
        grid=(n_tiles + 2,),
        in_specs=[
            pl.BlockSpec(memory_space=pltpu.SMEM),
            pl.BlockSpec((1, tile, D_MODEL), lambda i: (proj_tile(i) // tps, proj_tile(i) % tps, 0)),
            pl.BlockSpec((1, tile, D_MODEL), lambda i: (out_tile(i) // tps, out_tile(i) % tps, 0)),
            pl.BlockSpec((tile, 3 * LANES), lambda i: (proj_tile(i) % tps, 0)),
            pl.BlockSpec((1, WINDOW, KV_WIDTH), init_idx),
            pl.BlockSpec((1, WINDOW, KV_WIDTH), init_idx),
            pl.BlockSpec((1, HIST_ROWS, POOL_WIDTH), init_idx),
        ] + _mixer_weight_specs(layer),
        out_specs=[
            pl.BlockSpec((1, tile, D_MODEL), lambda i: (out_tile(i) // tps, out_tile(i) % tps, 0)),
            pl.BlockSpec((1, WINDOW, KV_WIDTH), lambda i: (proj_tile(i) // tps, 0, 0)),
            pl.BlockSpec((1, WINDOW, KV_WIDTH), lambda i: (proj_tile(i) // tps, 0, 0)),
            pl.BlockSpec((1, HIST_ROWS, POOL_WIDTH), lambda i: (proj_tile(i) // tps, 0, 0)),
        ],
        out_shape=[
            jax.ShapeDtypeStruct((n_seq, seq_len, D_MODEL), F32),
            jax.ShapeDtypeStruct((n_seq, WINDOW, KV_WIDTH), F32),
            jax.ShapeDtypeStruct((n_seq, WINDOW, KV_WIDTH), F32),
            jax.ShapeDtypeStruct((n_seq, HIST_ROWS, POOL_WIDTH), F32),
        ],
        scratch_shapes=buffer_set + buffer_set + [pltpu.VMEM((HIST_ROWS + tile, POOL_WIDTH), F32)],
        compiler_params=pltpu.CompilerParams(
            dimension_semantics=("arbitrary",), vmem_limit_bytes=VMEM_LIMIT),
        name=name,
    )(sinks, h, h, rope, kin, vin, uin, ln, win, pw, ps, wout)


def _mixer_short_kernel(sink_ref, h_ref, rope_ref, ck_ref, cv_ref, st_ref, ln_ref, win_ref,
                        pw_ref, ps_ref, wout_ref, *rest, seqs, layer):
    if layer:
        kprev_ref, vprev_ref = rest[:2]
        rest = rest[2:]
    ho_ref, kc_ref, vc_ref, un_ref, qbuf, abuf, ubuf = rest
    if layer:
        kc_ref[0:layer] = kprev_ref[...]
        vc_ref[0:layer] = vprev_ref[...]
    rows = seqs * DEC_SEQ
    h = h_ref[...]
    qb, k, v, u = _split_rope(_in_proj(h, ln_ref, win_ref), rope_ref[...])
    for j in range(Q_BLOCKS):
        qbuf[:, j * LANES:(j + 1) * LANES] = qb[j]
    kept = WINDOW - DEC_SEQ
    kc_ref[layer, :, 0:kept, :] = ck_ref[:, DEC_SEQ:, :]
    vc_ref[layer, :, 0:kept, :] = cv_ref[:, DEC_SEQ:, :]
    kc_ref[layer, :, kept:, :] = k.reshape(seqs, DEC_SEQ, KV_WIDTH)
    vc_ref[layer, :, kept:, :] = v.reshape(seqs, DEC_SEQ, KV_WIDTH)
    un_ref[...] = u
    ubuf[:, 0:HIST_ROWS, :] = st_ref[...]
    ubuf[:, HIST_ROWS:HIST_ROWS + DEC_SEQ, :] = u.reshape(seqs, DEC_SEQ, POOL_WIDTH)

    stacked = N_HEADS * DEC_SEQ
    keys = WINDOW + 2 * DEC_SEQ
    lo = lax.broadcasted_iota(jnp.int32, (1, LANES), 1) < HEAD_DIM
    srow = lax.broadcasted_iota(jnp.int32, (stacked, keys), 0)
    kc = lax.broadcasted_iota(jnp.int32, (stacked, keys), 1)
    qi = jnp.bitwise_and(srow, DEC_SEQ - 1)
    bias = jnp.where((kc >= qi) & (kc <= qi + WINDOW), 0.0, NEG_INF)
    hrow = jnp.right_shift(lax.broadcasted_iota(jnp.int32, (stacked, 1), 0), DEC_SEQ.bit_length() - 1)
    sink = jnp.zeros((stacked, 1), F32)
    for hh in range(N_HEADS):
        sink = jnp.where(hrow == hh, sink_ref[layer, hh], sink)
    zpad = jnp.zeros((DEC_SEQ, KV_WIDTH), F32)

    def seq_body(i, carry):
        r0 = pl.multiple_of(i * DEC_SEQ, DEC_SEQ)
        qrows = qbuf[pl.ds(r0, DEC_SEQ), :]
        q = _stack_heads([qrows[:, j * LANES:(j + 1) * LANES] for j in range(Q_BLOCKS)], lo)
        kall = jnp.concatenate([ck_ref[i], kc_ref[layer, i, kept:, :], zpad], axis=0).astype(BF16)
        vall = jnp.concatenate([cv_ref[i], vc_ref[layer, i, kept:, :], zpad], axis=0).astype(BF16)
        s = lax.dot_general(q, kall, (((1,), (1,)), ((), ())), preferred_element_type=F32)
        prob, inv = _softmax_sink(s + bias, sink)
        o = jnp.dot(prob, vall, preferred_element_type=F32) * inv
        abuf[pl.ds(r0, DEC_SEQ), :] = jnp.concatenate(
            [jnp.where(lo, o[j * DEC_SEQ:(j + 1) * DEC_SEQ],
                       o[(j + Q_BLOCKS) * DEC_SEQ:(j + Q_BLOCKS + 1) * DEC_SEQ])
             for j in range(Q_BLOCKS)], axis=-1)
        return carry

    lax.fori_loop(0, seqs, seq_body, 0, unroll=SAMPLE_UNROLL)

    d_groups = []
    for g, w in enumerate(POOL_WINDOWS):
        cols = slice(g * POOL_GROUP, (g + 1) * POOL_GROUP)
        wsum = _window_sum(ubuf[:, :, cols], w, 1)[:, HIST_ROWS:, :]
        cnt = float(min(PAST_LEN + 1, w))
        d_groups.append((wsum / cnt).reshape(rows, POOL_GROUP) - u[:, cols])
    pm = _pool_project(d_groups, pw_ref, ps_ref)

    mix = jnp.concatenate([abuf[...], pm], axis=-1).astype(BF16)
    ho_ref[...] = h + jnp.dot(mix, wout_ref[...], preferred_element_type=F32)


def _mixer_short(h, rope, ck, cv, st, sinks, ln, win, pw, ps, wout, kc_prev, vc_prev, *, seqs, layer, name):
    n_rows = h.shape[0]
    rows = seqs * DEC_SEQ
    n_seq = n_rows // DEC_SEQ
    stacked_spec = pl.BlockSpec((layer + 1, seqs, WINDOW, KV_WIDTH), lambda i: (0, i, 0, 0))
    prev_specs = [pl.BlockSpec((layer, seqs, WINDOW, KV_WIDTH), lambda i: (0, i, 0, 0))] * 2 if layer else []
    prev_args = (kc_prev, vc_prev) if layer else ()
    kern = functools.partial(_mixer_short_kernel, seqs=seqs, layer=layer)
    return pl.pallas_call(
        kern,
        grid=(n_rows // rows,),
        in_specs=[
            pl.BlockSpec(memory_space=pltpu.SMEM),
            pl.BlockSpec((rows, D_MODEL), lambda i: (i, 0)),
            _const_spec((rows, 3 * LANES)),
            pl.BlockSpec((None, seqs, WINDOW, KV_WIDTH), lambda i: (layer, i, 0, 0)),
            pl.BlockSpec((None, seqs, WINDOW, KV_WIDTH), lambda i: (layer, i, 0, 0)),
            pl.BlockSpec((seqs, HIST_ROWS, POOL_WIDTH), lambda i: (i, 0, 0)),
        ] + _mixer_weight_specs(layer) + prev_specs,
        out_specs=[
            pl.BlockSpec((rows, D_MODEL), lambda i: (i, 0)),
            stacked_spec,
            stacked_spec,
            pl.BlockSpec((rows, POOL_WIDTH), lambda i: (i, 0)),
        ],
        out_shape=[
            jax.ShapeDtypeStruct((n_rows, D_MODEL), F32),
            jax.ShapeDtypeStruct((layer + 1, n_seq, WINDOW, KV_WIDTH), F32),
            jax.ShapeDtypeStruct((layer + 1, n_seq, WINDOW, KV_WIDTH), F32),
            jax.ShapeDtypeStruct((n_rows, POOL_WIDTH), F32),
        ],
        scratch_shapes=[
            pltpu.VMEM((rows, ATTN_WIDTH), F32),
            pltpu.VMEM((rows, ATTN_WIDTH), F32),
            pltpu.VMEM((seqs, HIST_ROWS + DEC_SEQ, POOL_WIDTH), F32),
        ],
        compiler_params=pltpu.CompilerParams(
            dimension_semantics=("arbitrary",), vmem_limit_bytes=VMEM_LIMIT),
        name=name,
    )(sinks, h, rope, ck, cv, st, ln, win, pw, ps, wout, *prev_args)


def _mlp_kernel(h_ref, ln_ref, wup_ref, wdown_ref, lnf_ref, o_ref, *, final_norm):
    h = h_ref[...]
    xn = _rmsnorm(h, ln_ref[...]).astype(BF16)
    a = jnp.maximum(jnp.dot(xn, wup_ref[...], preferred_element_type=F32), 0.0)
    out = h + jnp.dot((a * a).astype(BF16), wdown_ref[...], preferred_element_type=F32)
    if final_norm:
        out = _rmsnorm(out, lnf_ref[...])
    o_ref[...] = out


def _mlp(h, ln, wup, wdown, lnf, *, final_norm, layer, name):
    n_rows = h.shape[0]
    tile = min(MLP_TILE, n_rows)
    kern = functools.partial(_mlp_kernel, final_norm=final_norm)
    return pl.pallas_call(
        kern,
        grid=(n_rows // tile,),
        in_specs=[
            pl.BlockSpec((tile, D_MODEL), lambda i: (i, 0)),
            _layer_spec((1, D_MODEL), layer),
            _layer_spec((D_MODEL, D_FF), layer),
            _layer_spec((D_FF, D_MODEL), layer),
            _const_spec((1, D_MODEL)),
        ],
        out_specs=pl.BlockSpec((tile, D_MODEL), lambda i: (i, 0)),
        out_shape=jax.ShapeDtypeStruct((n_rows, D_MODEL), F32),
        compiler_params=pltpu.CompilerParams(
            dimension_semantics=("arbitrary",), vmem_limit_bytes=VMEM_LIMIT),
        name=name,
    )(h, ln, wup, wdown, lnf)


def _rope_table(pos):
    n = pos.shape[0]
    half = ROT_DIM // 2
    inv_freq = ROPE_THETA ** (-jnp.arange(0, ROT_DIM, 2, dtype=F32) / ROT_DIM)
    ang = pos.astype(F32)[:, None] * inv_freq[None, :]
    cos, sin = jnp.cos(ang), jnp.sin(ang)
    rest = HEAD_DIM - ROT_DIM
    zh = jnp.zeros((n, half), F32)
    c = jnp.concatenate([cos, cos, jnp.ones((n, rest), F32)], axis=-1)
    s1 = jnp.concatenate([-sin, zh, jnp.zeros((n, rest), F32)], axis=-1)
    s2 = jnp.concatenate([zh, sin, jnp.zeros((n, rest), F32)], axis=-1)
    reps = LANES // HEAD_DIM
    return jnp.concatenate([jnp.tile(c, (1, reps)), jnp.tile(s1, (1, reps)), jnp.tile(s2, (1, reps))], axis=-1)


def _pair_heads(w, axis):
    shape = w.shape
    split = shape[:axis] + (N_HEADS // Q_BLOCKS, Q_BLOCKS, HEAD_DIM) + shape[axis + 1:]
    return jnp.swapaxes(w.reshape(split), axis, axis + 1).reshape(shape)


def kernel(x_prompt, x_sample, cache_k, cache_v, state_pool, meta_tokens, ln1, w_in, attn_sinks,
           pool_w, pool_scale, w_out, ln2, w_up, w_down, ln_f):
    batch, seq, _ = x_prompt.shape
    dec_batch, dec_seq, _ = x_sample.shape
    depth = w_in.shape[0]
    assert dec_seq == DEC_SEQ and seq % PROMPT_TILE == 0 and dec_batch % SAMPLE_SEQS == 0

    pad = WINDOW - N_META
    rope_meta = _rope_table(jnp.arange(WINDOW) - pad)
    rope_prompt = _rope_table(N_META + jnp.arange(seq))
    rope_sample = jnp.tile(_rope_table(PAST_LEN + jnp.arange(DEC_SEQ)), (SAMPLE_SEQS, 1))

    win = jnp.concatenate([_pair_heads(w_in[:, :, :ATTN_WIDTH], 2), w_in[:, :, ATTN_WIDTH:]], axis=2).astype(BF16)
    wout = jnp.concatenate([_pair_heads(w_out[:, :ATTN_WIDTH], 1), w_out[:, ATTN_WIDTH:]], axis=1).astype(BF16)
    wup = w_up.astype(BF16)
    wdown = w_down.astype(BF16)
    pw = pool_w.astype(BF16)
    ps = pool_scale.reshape(depth, 1, POOL_WIDTH)
    l1 = ln1.reshape(depth, 1, D_MODEL)
    l2 = ln2.reshape(depth, 1, D_MODEL)
    lnf = ln_f.reshape(1, D_MODEL)
    sinks = attn_sinks.astype(F32)
    mixer_w = (sinks, l1, win, pw, ps, wout)

    hm = jnp.concatenate([jnp.zeros((pad, D_MODEL), F32), meta_tokens.astype(F32)], axis=0)[None]
    hp = x_prompt
    hs = x_sample.reshape(dec_batch * dec_seq, D_MODEL)
    zero_kv = jnp.zeros((1, WINDOW, KV_WIDTH), F32)
    zero_u = jnp.zeros((1, HIST_ROWS, POOL_WIDTH), F32)
    ck = cache_k.reshape(depth, dec_batch, WINDOW, KV_WIDTH)
    cv = cache_v.reshape(depth, dec_batch, WINDOW, KV_WIDTH)

    pk, pv, pu, su = [], [], [], []
    kc = vc = None
    for l in range(depth):
        last = l == depth - 1
        hm_mid, km, vm, um = _mixer_long(hm, rope_meta, zero_kv, zero_kv, zero_u, *mixer_w, tile=WINDOW,
                                         base_pos=-pad, layer=l, name=f"mixer_meta_{l}")
        hp_mid, kt, vt, ut = _mixer_long(hp, rope_prompt, km, vm, um, *mixer_w, tile=PROMPT_TILE,
                                         base_pos=N_META, layer=l, name=f"mixer_prompt_{l}")
        st = jnp.pad(state_pool[l], ((0, 0), (1, 0), (0, 0)))
        hs_mid, kc, vc, un = _mixer_short(hs, rope_sample, ck, cv, st, *mixer_w, kc, vc, seqs=SAMPLE_SEQS,
                                          layer=l, name=f"mixer_sample_{l}")

        mlp_w = (l2, wup, wdown, lnf)
        if not last:
            hm = _mlp(hm_mid[0], *mlp_w, final_norm=False, layer=l, name=f"mlp_meta_{l}")[None]
        hp = _mlp(hp_mid.reshape(batch * seq, D_MODEL), *mlp_w, final_norm=last, layer=l,
                  name=f"mlp_prompt_{l}").reshape(batch, seq, D_MODEL)
        hs = _mlp(hs_mid, *mlp_w, final_norm=last, layer=l, name=f"mlp_sample_{l}")

        pk.append(kt.reshape(batch, WINDOW, N_KV_HEADS, HEAD_DIM))
        pv.append(vt.reshape(batch, WINDOW, N_KV_HEADS, HEAD_DIM))
        pu.append(ut[:, 1:, :])
        un = un.reshape(dec_batch, dec_seq, POOL_WIDTH)
        su.append(jnp.concatenate([state_pool[l][:, dec_seq:], un], axis=1))

    y_sample = hs.reshape(dec_batch, dec_seq, D_MODEL)
    cache_shape = (depth, dec_batch, WINDOW, N_KV_HEADS, HEAD_DIM)
    return (hp, y_sample, jnp.stack(pk), jnp.stack(pv), jnp.stack(pu),
            kc.reshape(cache_shape), vc.reshape(cache_shape), jnp.stack(su))
```

```python
import functools

import jax
import jax.numpy as jnp
from jax import lax
from jax.experimental import pallas as pl
from jax.experimental.pallas import tpu as pltpu

D_MODEL = 1024
N_HEADS = 8
N_KV_HEADS = 2
HEAD_DIM = 64
ATTN_WIDTH = N_HEADS * HEAD_DIM
KV_WIDTH = N_KV_HEADS * HEAD_DIM
POOL_WINDOWS = (2, 4, 8, 16)
POOL_WIDTH = D_MODEL - ATTN_WIDTH
POOL_GROUP = POOL_WIDTH // len(POOL_WINDOWS)
POOL_HIST = max(POOL_WINDOWS) - 1
IN_WIDTH = ATTN_WIDTH + 2 * KV_WIDTH + POOL_WIDTH
WINDOW = 128
ROT_DIM = HEAD_DIM // 4
ROPE_THETA = 500000.0
D_FF = 4 * D_MODEL
N_META = 16
RMS_EPS = 1e-5
PAST_LEN = 16384
DEC_SEQ = 8

LANES = 128
HIST_ROWS = POOL_HIST + 1
Q_BLOCKS = ATTN_WIDTH // LANES
SCALE = HEAD_DIM ** -0.5
NEG_INF = float("-inf")

PROMPT_TILE = 512
SAMPLE_SEQS = 32
SAMPLE_UNROLL = 16
MLP_TILE = 512
VMEM_LIMIT = 56 * 1024 * 1024

F32 = jnp.float32
BF16 = jnp.bfloat16


def _rmsnorm(x, g):
    r = lax.rsqrt(jnp.mean(x * x, axis=-1, keepdims=True) + RMS_EPS)
    return x * r * g


def _rope(x, rope):
    c = rope[:, 0:LANES]
    s1 = rope[:, LANES:2 * LANES]
    s2 = rope[:, 2 * LANES:3 * LANES]
    half = ROT_DIM // 2
    return x * c + pltpu.roll(x, LANES - half, 1) * s1 + pltpu.roll(x, half, 1) * s2


def _in_proj(h, ln_ref, win_ref):
    xn = _rmsnorm(h, ln_ref[...]).astype(BF16)
    return jnp.dot(xn, win_ref[...], preferred_element_type=F32)


def _split_rope(p, rope):
    qb = [_rope(p[:, j * LANES:(j + 1) * LANES], rope) * SCALE for j in range(Q_BLOCKS)]
    k = _rope(p[:, ATTN_WIDTH:ATTN_WIDTH + KV_WIDTH], rope)
    v = p[:, ATTN_WIDTH + KV_WIDTH:ATTN_WIDTH + 2 * KV_WIDTH]
    u = p[:, ATTN_WIDTH + 2 * KV_WIDTH:]
    return qb, k, v, u


def _stack_heads(qrows, lo):
    zero = jnp.zeros_like(qrows[0])
    parts = [jnp.where(lo, q, zero) for q in qrows] + [jnp.where(lo, zero, q) for q in qrows]
    return jnp.concatenate(parts, axis=0).astype(BF16)


def _softmax_sink(s, sink):
    m = jnp.maximum(jnp.max(s, axis=-1, keepdims=True), sink)
    e = jnp.exp(s - m)
    l = jnp.sum(e, axis=-1, keepdims=True) + jnp.exp(sink - m)
    return e.astype(BF16), 1.0 / l


def _window_sum(x, w, axis):
    span = 1
    while span < w:
        x = x + pltpu.roll(x, span, axis)
        span *= 2
    return x


def _pool_project(d_groups, pw_ref, ps_ref):
    outs = [jnp.dot(d.astype(BF16), pw_ref[g], preferred_element_type=F32)
            for g, d in enumerate(d_groups)]
    return jnp.concatenate(outs, axis=-1) * ps_ref[...]


def _project_finish(p, rope_ref, kin_ref, vin_ref, uin_ref, pw_ref, ps_ref,
                    kt_ref, vt_ref, ut_ref, ubuf, dst, src, *, tile, first, p0):
    dq, dk, dv, dm, _ = dst
    _, sk, sv, _, _ = src
    qb, k, v, u = _split_rope(p, rope_ref[...])
    lo = lax.broadcasted_iota(jnp.int32, (1, LANES), 1) < HEAD_DIM
    for r in range(tile // WINDOW):
        rows = slice(r * WINDOW, (r + 1) * WINDOW)
        dq[r] = _stack_heads([qb[j][rows] for j in range(Q_BLOCKS)], lo)
    dk[0:WINDOW, :] = jnp.where(first, kin_ref[0].astype(BF16), sk[tile:tile + WINDOW, :])
    dv[0:WINDOW, :] = jnp.where(first, vin_ref[0].astype(BF16), sv[tile:tile + WINDOW, :])
    dk[WINDOW:WINDOW + tile, :] = k.astype(BF16)
    dv[WINDOW:WINDOW + tile, :] = v.astype(BF16)
    kt_ref[0] = k[tile - WINDOW:, :]
    vt_ref[0] = v[tile - WINDOW:, :]
    ut_ref[0] = u[tile - HIST_ROWS:, :]

    ubuf[0:HIST_ROWS, :] = jnp.where(first, uin_ref[0], ubuf[tile:tile + HIST_ROWS, :])
    ubuf[HIST_ROWS:HIST_ROWS + tile, :] = u
    pos = p0 + lax.broadcasted_iota(jnp.int32, (tile, 1), 0)
    d_groups = []
    for g, w in enumerate(POOL_WINDOWS):
        cols = slice(g * POOL_GROUP, (g + 1) * POOL_GROUP)
        wsum = _window_sum(ubuf[:, cols], w, 0)[HIST_ROWS:]
        cnt = jnp.clip(pos + 1, 1, w).astype(F32)
        d_groups.append(wsum / cnt - u[:, cols])
    dm[...] = _pool_project(d_groups, pw_ref, ps_ref).astype(BF16)


def _attend_scores(src, *, tile):
    sq, sk, _, _, _ = src
    return [lax.dot_general(sq[r], sk[r * WINDOW:(r + 2) * WINDOW, :], (((1,), (1,)), ((), ())),
                            preferred_element_type=F32) for r in range(tile // WINDOW)]


def _attend_values(scores, sink_ref, src, *, layer, p0):
    _, _, sv, _, _ = src
    lo = lax.broadcasted_iota(jnp.int32, (1, LANES), 1) < HEAD_DIM
    qi = lax.broadcasted_iota(jnp.int32, (WINDOW, 2 * WINDOW), 0)
    kc = lax.broadcasted_iota(jnp.int32, (WINDOW, 2 * WINDOW), 1)
    band = (kc >= qi) & (kc <= qi + WINDOW)
    attn_rows = []
    for r, s in enumerate(scores):
        vb = sv[r * WINDOW:(r + 2) * WINDOW, :]
        kpos = p0 + (r - 1) * WINDOW + kc
        bias = jnp.where(band & (kpos >= 0), 0.0, NEG_INF)
        probs, inv = zip(*[_softmax_sink(s[hh * WINDOW:(hh + 1) * WINDOW] + bias, sink_ref[layer, hh])
                           for hh in range(N_HEADS)])
        o = jnp.dot(jnp.concatenate(probs, axis=0), vb, preferred_element_type=F32)
        heads = [o[hh * WINDOW:(hh + 1) * WINDOW] * inv[hh] for hh in range(N_HEADS)]
        attn_rows.append(jnp.concatenate(
            [jnp.where(lo, heads[j], heads[j + Q_BLOCKS]) for j in range(Q_BLOCKS)], axis=-1))
    return jnp.concatenate(attn_rows, axis=0).astype(BF16)


def _mixer_long_kernel(sink_ref, hp_ref, hr_ref, rope_ref, kin_ref, vin_ref, uin_ref, ln_ref, win_ref,
                       pw_ref, ps_ref, wout_ref, ho_ref, kt_ref, vt_ref, ut_ref,
                       qa, ka, va, ma, xa, qb, kb, vb, mb, xb, ubuf,
                       *, tile, base_pos, layer, tiles_per_seq, n_tiles):
    i = pl.program_id(0)
    tp = lax.rem(jnp.minimum(i, n_tiles - 1), tiles_per_seq)
    ts = lax.rem(jnp.clip(i - 1, 0, n_tiles - 1), tiles_per_seq)
    set_a, set_b = (qa, ka, va, ma, xa), (qb, kb, vb, mb, xb)

    @pl.when(i == 0)
    def _():
        for ref in set_b + (xa, ubuf):
            ref[...] = jnp.zeros(ref.shape, ref.dtype)

    def step(dst, src):
        scores = _attend_scores(src, tile=tile)
        p = _in_proj(hp_ref[0], ln_ref, win_ref)
        ho_ref[0] = hr_ref[0] + jnp.dot(dst[4][...], wout_ref[...], preferred_element_type=F32)
        attn = _attend_values(scores, sink_ref, src, layer=layer, p0=base_pos + ts * tile)
        src[4][...] = jnp.concatenate([attn, src[3][...]], axis=-1)
        _project_finish(p, rope_ref, kin_ref, vin_ref, uin_ref, pw_ref, ps_ref, kt_ref, vt_ref, ut_ref,
                        ubuf, dst, src, tile=tile, first=tp == 0, p0=base_pos + tp * tile)

    parity = lax.rem(i, 2)

    @pl.when(parity == 0)
    def _():
        step(set_a, set_b)

    @pl.when(parity == 1)
    def _():
        step(set_b, set_a)


def _const_spec(shape):
    nd = len(shape)
    return pl.BlockSpec(shape, lambda *_: (0,) * nd, pipeline_mode=pl.Buffered(1))


def _layer_spec(shape, layer):
    nd = len(shape)
    return pl.BlockSpec((None,) + shape, lambda *_: (layer,) + (0,) * nd, pipeline_mode=pl.Buffered(1))


def _mixer_weight_specs(layer):
    return [
        _layer_spec((1, D_MODEL), layer),
        _layer_spec((D_MODEL, IN_WIDTH), layer),
        _layer_spec((len(POOL_WINDOWS), POOL_GROUP, POOL_GROUP), layer),
        _layer_spec((1, POOL_WIDTH), layer),
        _layer_spec((D_MODEL, D_MODEL), layer),
    ]


def _mixer_long(h, rope, kin, vin, uin, sinks, ln, win, pw, ps, wout, *, tile, base_pos, layer, name):
    n_seq, seq_len, _ = h.shape
    tps = seq_len // tile
    n_tiles = n_seq * tps
    shared_init = kin.shape[0] == 1

    def proj_tile(i):
        return jnp.minimum(i, n_tiles - 1)

    def out_tile(i):
        return jnp.maximum(i - 2, 0)

    def init_idx(i):
        return (0 if shared_init else proj_tile(i) // tps, 0, 0)

    kern = functools.partial(_mixer_long_kernel, tile=tile, base_pos=base_pos, layer=layer,
                             tiles_per_seq=tps, n_tiles=n_tiles)
    buffer_set = [
        pltpu.VMEM((tile // WINDOW, N_HEADS * WINDOW, LANES), BF16),
        pltpu.VMEM((WINDOW + tile, KV_WIDTH), BF16),
        pltpu.VMEM((WINDOW + tile, KV_WIDTH), BF16),
        pltpu.VMEM((tile, POOL_WIDTH), BF16),
        pltpu.VMEM((tile, D_MODEL), BF16),
    ]
    return pl.pallas_call(
        kern,
        grid=(n_tiles + 2,),
        in_specs=[
            pl.BlockSpec(memory_space=pltpu.SMEM),
            pl.BlockSpec((1, tile, D_MODEL), lambda i: (proj_tile(i) // tps, proj_tile(i) % tps, 0)),
            pl.BlockSpec((1, tile, D_MODEL), lambda i: (out_tile(i) // tps, out_tile(i) % tps, 0)),
            pl.BlockSpec((tile, 3 * LANES), lambda i: (proj_tile(i) % tps, 0)),
            pl.BlockSpec((1, WINDOW, KV_WIDTH), init_idx),
            pl.BlockSpec((1, WINDOW, KV_WIDTH), init_idx),
            pl.BlockSpec((1, HIST_ROWS, POOL_WIDTH), init_idx),
        ] + _mixer_weight_specs(layer),
        out_specs=[
            pl.BlockSpec((1, tile, D_MODEL), lambda i: (out_tile(i) // tps, out_tile(i) % tps, 0)),
            pl.BlockSpec((1, WINDOW, KV_WIDTH), lambda i: (proj_tile(i) // tps, 0, 0)),
            pl.BlockSpec((1, WINDOW, KV_WIDTH), lambda i: (proj_tile(i) // tps, 0, 0)),
            pl.BlockSpec((1, HIST_ROWS, POOL_WIDTH), lambda i: (proj_tile(i) // tps, 0, 0)),
        ],
        out_shape=[
            jax.ShapeDtypeStruct((n_seq, seq_len, D_MODEL), F32),
            jax.ShapeDtypeStruct((n_seq, WINDOW, KV_WIDTH), F32),
            jax.ShapeDtypeStruct((n_seq, WINDOW, KV_WIDTH), F32),
            jax.ShapeDtypeStruct((n_seq, HIST_ROWS, POOL_WIDTH), F32),
        ],
        scratch_shapes=buffer_set + buffer_set + [pltpu.VMEM((HIST_ROWS + tile, POOL_WIDTH), F32)],
        compiler_params=pltpu.CompilerParams(
            dimension_semantics=("arbitrary",), vmem_limit_bytes=VMEM_LIMIT),
        name=name,
    )(sinks, h, h, rope, kin, vin, uin, ln, win, pw, ps, wout)


def _mixer_short_kernel(sink_ref, h_ref, rope_ref, ck_ref, cv_ref, st_ref, ln_ref, win_ref,
                        pw_ref, ps_ref, wout_ref, *rest, seqs, layer):
    if layer:
        kprev_ref, vprev_ref = rest[:2]
        rest = rest[2:]
    ho_ref, kc_ref, vc_ref, un_ref, qbuf, abuf, ubuf = rest
    if layer:
        kc_ref[0:layer] = kprev_ref[...]
        vc_ref[0:layer] = vprev_ref[...]
    rows = seqs * DEC_SEQ
    h = h_ref[...]
    qb, k, v, u = _split_rope(_in_proj(h, ln_ref, win_ref), rope_ref[...])
    for j in range(Q_BLOCKS):
        qbuf[:, j * LANES:(j + 1) * LANES] = qb[j]
    kept = WINDOW - DEC_SEQ
    kc_ref[layer, :, 0:kept, :] = ck_ref[:, DEC_SEQ:, :]
    vc_ref[layer, :, 0:kept, :] = cv_ref[:, DEC_SEQ:, :]
    kc_ref[layer, :, kept:, :] = k.reshape(seqs, DEC_SEQ, KV_WIDTH)
    vc_ref[layer, :, kept:, :] = v.reshape(seqs, DEC_SEQ, KV_WIDTH)
    un_ref[...] = u
    ubuf[:, 0:HIST_ROWS, :] = st_ref[...]
    ubuf[:, HIST_ROWS:HIST_ROWS + DEC_SEQ, :] = u.reshape(seqs, DEC_SEQ, POOL_WIDTH)

    stacked = N_HEADS * DEC_SEQ
    keys = WINDOW + 2 * DEC_SEQ
    lo = lax.broadcasted_iota(jnp.int32, (1, LANES), 1) < HEAD_DIM
    srow = lax.broadcasted_iota(jnp.int32, (stacked, keys), 0)
    kc = lax.broadcasted_iota(jnp.int32, (stacked, keys), 1)
    qi = jnp.bitwise_and(srow, DEC_SEQ - 1)
    bias = jnp.where((kc >= qi) & (kc <= qi + WINDOW), 0.0, NEG_INF)
    hrow = jnp.right_shift(lax.broadcasted_iota(jnp.int32, (stacked, 1), 0), DEC_SEQ.bit_length() - 1)
    sink = jnp.zeros((stacked, 1), F32)
    for hh in range(N_HEADS):
        sink = jnp.where(hrow == hh, sink_ref[layer, hh], sink)
    zpad = jnp.zeros((DEC_SEQ, KV_WIDTH), F32)

    def group_body(gi, carry):
        ids = [gi * SAMPLE_UNROLL + n for n in range(SAMPLE_UNROLL)]
        starts = [pl.multiple_of(i * DEC_SEQ, DEC_SEQ) for i in ids]
        scores = []
        for i, r0 in zip(ids, starts):
            qrows = qbuf[pl.ds(r0, DEC_SEQ), :]
            q = _stack_heads([qrows[:, j * LANES:(j + 1) * LANES] for j in range(Q_BLOCKS)], lo)
            kall = jnp.concatenate([ck_ref[i], kc_ref[layer, i, kept:, :], zpad], axis=0).astype(BF16)
            scores.append(lax.dot_general(q, kall, (((1,), (1,)), ((), ())), preferred_element_type=F32))
        probs = [_softmax_sink(s + bias, sink) for s in scores]
        for i, r0, (prob, inv) in zip(ids, starts, probs):
            vall = jnp.concatenate([cv_ref[i], vc_ref[layer, i, kept:, :], zpad], axis=0).astype(BF16)
            o = jnp.dot(prob, vall, preferred_element_type=F32) * inv
            abuf[pl.ds(r0, DEC_SEQ), :] = jnp.concatenate(
                [jnp.where(lo, o[j * DEC_SEQ:(j + 1) * DEC_SEQ],
                           o[(j + Q_BLOCKS) * DEC_SEQ:(j + Q_BLOCKS + 1) * DEC_SEQ])
                 for j in range(Q_BLOCKS)], axis=-1)
        return carry

    lax.fori_loop(0, seqs // SAMPLE_UNROLL, group_body, 0)

    d_groups = []
    for g, w in enumerate(POOL_WINDOWS):
        cols = slice(g * POOL_GROUP, (g + 1) * POOL_GROUP)
        wsum = _window_sum(ubuf[:, :, cols], w, 1)[:, HIST_ROWS:, :]
        cnt = float(min(PAST_LEN + 1, w))
        d_groups.append((wsum / cnt).reshape(rows, POOL_GROUP) - u[:, cols])
    pm = _pool_project(d_groups, pw_ref, ps_ref)

    mix = jnp.concatenate([abuf[...], pm], axis=-1).astype(BF16)
    ho_ref[...] = h + jnp.dot(mix, wout_ref[...], preferred_element_type=F32)


def _mixer_short(h, rope, ck, cv, st, sinks, ln, win, pw, ps, wout, kc_prev, vc_prev, *, seqs, layer, name):
    n_rows = h.shape[0]
    rows = seqs * DEC_SEQ
    n_seq = n_rows // DEC_SEQ
    stacked_spec = pl.BlockSpec((layer + 1, seqs, WINDOW, KV_WIDTH), lambda i: (0, i, 0, 0))
    prev_specs = [pl.BlockSpec((layer, seqs, WINDOW, KV_WIDTH), lambda i: (0, i, 0, 0))] * 2 if layer else []
    prev_args = (kc_prev, vc_prev) if layer else ()
    kern = functools.partial(_mixer_short_kernel, seqs=seqs, layer=layer)
    return pl.pallas_call(
        kern,
        grid=(n_rows // rows,),
        in_specs=[
            pl.BlockSpec(memory_space=pltpu.SMEM),
            pl.BlockSpec((rows, D_MODEL), lambda i: (i, 0)),
            _const_spec((rows, 3 * LANES)),
            pl.BlockSpec((None, seqs, WINDOW, KV_WIDTH), lambda i: (layer, i, 0, 0)),
            pl.BlockSpec((None, seqs, WINDOW, KV_WIDTH), lambda i: (layer, i, 0, 0)),
            pl.BlockSpec((seqs, HIST_ROWS, POOL_WIDTH), lambda i: (i, 0, 0)),
        ] + _mixer_weight_specs(layer) + prev_specs,
        out_specs=[
            pl.BlockSpec((rows, D_MODEL), lambda i: (i, 0)),
            stacked_spec,
            stacked_spec,
            pl.BlockSpec((rows, POOL_WIDTH), lambda i: (i, 0)),
        ],
        out_shape=[
            jax.ShapeDtypeStruct((n_rows, D_MODEL), F32),
            jax.ShapeDtypeStruct((layer + 1, n_seq, WINDOW, KV_WIDTH), F32),
            jax.ShapeDtypeStruct((layer + 1, n_seq, WINDOW, KV_WIDTH), F32),
            jax.ShapeDtypeStruct((n_rows, POOL_WIDTH), F32),
        ],
        scratch_shapes=[
            pltpu.VMEM((rows, ATTN_WIDTH), F32),
            pltpu.VMEM((rows, ATTN_WIDTH), F32),
            pltpu.VMEM((seqs, HIST_ROWS + DEC_SEQ, POOL_WIDTH), F32),
        ],
        compiler_params=pltpu.CompilerParams(
            dimension_semantics=("arbitrary",), vmem_limit_bytes=VMEM_LIMIT),
        name=name,
    )(sinks, h, rope, ck, cv, st, ln, win, pw, ps, wout, *prev_args)


def _mlp_kernel(h_ref, ln_ref, wup_ref, wdown_ref, lnf_ref, o_ref, *, final_norm):
    h = h_ref[...]
    xn = _rmsnorm(h, ln_ref[...]).astype(BF16)
    a = jnp.maximum(jnp.dot(xn, wup_ref[...], preferred_element_type=F32), 0.0)
    out = h + jnp.dot((a * a).astype(BF16), wdown_ref[...], preferred_element_type=F32)
    if final_norm:
        out = _rmsnorm(out, lnf_ref[...])
    o_ref[...] = out


def _mlp(h, ln, wup, wdown, lnf, *, final_norm, layer, name):
    n_rows = h.shape[0]
    tile = min(MLP_TILE, n_rows)
    kern = functools.partial(_mlp_kernel, final_norm=final_norm)
    return pl.pallas_call(
        kern,
        grid=(n_rows // tile,),
        in_specs=[
            pl.BlockSpec((tile, D_MODEL), lambda i: (i, 0)),
            _layer_spec((1, D_MODEL), layer),
            _layer_spec((D_MODEL, D_FF), layer),
            _layer_spec((D_FF, D_MODEL), layer),
            _const_spec((1, D_MODEL)),
        ],
        out_specs=pl.BlockSpec((tile, D_MODEL), lambda i: (i, 0)),
        out_shape=jax.ShapeDtypeStruct((n_rows, D_MODEL), F32),
        compiler_params=pltpu.CompilerParams(
            dimension_semantics=("arbitrary",), vmem_limit_bytes=VMEM_LIMIT),
        name=name,
    )(h, ln, wup, wdown, lnf)


def _rope_table(pos):
    n = pos.shape[0]
    half = ROT_DIM // 2
    inv_freq = ROPE_THETA ** (-jnp.arange(0, ROT_DIM, 2, dtype=F32) / ROT_DIM)
    ang = pos.astype(F32)[:, None] * inv_freq[None, :]
    cos, sin = jnp.cos(ang), jnp.sin(ang)
    rest = HEAD_DIM - ROT_DIM
    zh = jnp.zeros((n, half), F32)
    c = jnp.concatenate([cos, cos, jnp.ones((n, rest), F32)], axis=-1)
    s1 = jnp.concatenate([-sin, zh, jnp.zeros((n, rest), F32)], axis=-1)
    s2 = jnp.concatenate([zh, sin, jnp.zeros((n, rest), F32)], axis=-1)
    reps = LANES // HEAD_DIM
    return jnp.concatenate([jnp.tile(c, (1, reps)), jnp.tile(s1, (1, reps)), jnp.tile(s2, (1, reps))], axis=-1)


def _pair_heads(w, axis):
    shape = w.shape
    split = shape[:axis] + (N_HEADS // Q_BLOCKS, Q_BLOCKS, HEAD_DIM) + shape[axis + 1:]
    return jnp.swapaxes(w.reshape(split), axis, axis + 1).reshape(shape)


def kernel(x_prompt, x_sample, cache_k, cache_v, state_pool, meta_tokens, ln1, w_in, attn_sinks,
           pool_w, pool_scale, w_out, ln2, w_up, w_down, ln_f):
    batch, seq, _ = x_prompt.shape
    dec_batch, dec_seq, _ = x_sample.shape
    depth = w_in.shape[0]
    assert dec_seq == DEC_SEQ and seq % PROMPT_TILE == 0 and dec_batch % SAMPLE_SEQS == 0

    pad = WINDOW - N_META
    rope_meta = _rope_table(jnp.arange(WINDOW) - pad)
    rope_prompt = _rope_table(N_META + jnp.arange(seq))
    rope_sample = jnp.tile(_rope_table(PAST_LEN + jnp.arange(DEC_SEQ)), (SAMPLE_SEQS, 1))

    win = jnp.concatenate([_pair_heads(w_in[:, :, :ATTN_WIDTH], 2), w_in[:, :, ATTN_WIDTH:]], axis=2).astype(BF16)
    wout = jnp.concatenate([_pair_heads(w_out[:, :ATTN_WIDTH], 1), w_out[:, ATTN_WIDTH:]], axis=1).astype(BF16)
    wup = w_up.astype(BF16)
    wdown = w_down.astype(BF16)
    pw = pool_w.astype(BF16)
    ps = pool_scale.reshape(depth, 1, POOL_WIDTH)
    l1 = ln1.reshape(depth, 1, D_MODEL)
    l2 = ln2.reshape(depth, 1, D_MODEL)
    lnf = ln_f.reshape(1, D_MODEL)
    sinks = attn_sinks.astype(F32)
    mixer_w = (sinks, l1, win, pw, ps, wout)

    hm = jnp.concatenate([jnp.zeros((pad, D_MODEL), F32), meta_tokens.astype(F32)], axis=0)[None]
    hp = x_prompt
    hs = x_sample.reshape(dec_batch * dec_seq, D_MODEL)
    zero_kv = jnp.zeros((1, WINDOW, KV_WIDTH), F32)
    zero_u = jnp.zeros((1, HIST_ROWS, POOL_WIDTH), F32)
    ck = cache_k.reshape(depth, dec_batch, WINDOW, KV_WIDTH)
    cv = cache_v.reshape(depth, dec_batch, WINDOW, KV_WIDTH)

    pk, pv, pu, su = [], [], [], []
    kc = vc = None
    for l in range(depth):
        last = l == depth - 1
        hm_mid, km, vm, um = _mixer_long(hm, rope_meta, zero_kv, zero_kv, zero_u, *mixer_w, tile=WINDOW,
                                         base_pos=-pad, layer=l, name=f"mixer_meta_{l}")
        hp_mid, kt, vt, ut = _mixer_long(hp, rope_prompt, km, vm, um, *mixer_w, tile=PROMPT_TILE,
                                         base_pos=N_META, layer=l, name=f"mixer_prompt_{l}")
        st = jnp.pad(state_pool[l], ((0, 0), (1, 0), (0, 0)))
        hs_mid, kc, vc, un = _mixer_short(hs, rope_sample, ck, cv, st, *mixer_w, kc, vc, seqs=SAMPLE_SEQS,
                                          layer=l, name=f"mixer_sample_{l}")

        mlp_w = (l2, wup, wdown, lnf)
        if not last:
            hm = _mlp(hm_mid[0], *mlp_w, final_norm=False, layer=l, name=f"mlp_meta_{l}")[None]
        hp = _mlp(hp_mid.reshape(batch * seq, D_MODEL), *mlp_w, final_norm=last, layer=l,
                  name=f"mlp_prompt_{l}").reshape(batch, seq, D_MODEL)
        hs = _mlp(hs_mid, *mlp_w, final_norm=last, layer=l, name=f"mlp_sample_{l}")

        pk.append(kt.reshape(batch, WINDOW, N_KV_HEADS, HEAD_DIM))
        pv.append(vt.reshape(batch, WINDOW, N_KV_HEADS, HEAD_DIM))
        pu.append(ut[:, 1:, :])
        un = un.reshape(dec_batch, dec_seq, POOL_WIDTH)
        su.append(jnp.concatenate([state_pool[l][:, dec_seq:], un], axis=1))

    y_sample = hs.reshape(dec_batch, dec_seq, D_MODEL)
    cache_shape = (depth, dec_batch, WINDOW, N_KV_HEADS, HEAD_DIM)
    return (hp, y_sample, jnp.stack(pk), jnp.stack(pv), jnp.stack(pu),
            kc.reshape(cache_shape), vc.reshape(cache_shape), jnp.stack(su))
```

```python
import functools

import jax
import jax.numpy as jnp
from jax import lax
from jax.experimental import pallas as pl
from jax.experimental.pallas import tpu as pltpu

D_MODEL = 1024
N_HEADS = 8
N_KV_HEADS = 2
HEAD_DIM = 64
ATTN_WIDTH = N_HEADS * HEAD_DIM
KV_WIDTH = N_KV_HEADS * HEAD_DIM
POOL_WINDOWS = (2, 4, 8, 16)
POOL_WIDTH = D_MODEL - ATTN_WIDTH
POOL_GROUP = POOL_WIDTH // len(POOL_WINDOWS)
POOL_HIST = max(POOL_WINDOWS) - 1
IN_WIDTH = ATTN_WIDTH + 2 * KV_WIDTH + POOL_WIDTH
WINDOW = 128
ROT_DIM = HEAD_DIM // 4
ROPE_THETA = 500000.0
D_FF = 4 * D_MODEL
N_META = 16
RMS_EPS = 1e-5
PAST_LEN = 16384
DEC_SEQ = 8

LANES = 128
HIST_ROWS = POOL_HIST + 1
Q_BLOCKS = ATTN_WIDTH // LANES
LOG2E = 1.4426950408889634
Q_SCALE = HEAD_DIM ** -0.5 * LOG2E
NEG_INF = float("-inf")

PROMPT_TILE = 512
SAMPLE_SEQS = 32
SAMPLE_UNROLL = 16
MLP_TILE = 512
VMEM_LIMIT = 56 * 1024 * 1024

F32 = jnp.float32
BF16 = jnp.bfloat16


def _rmsnorm(x, g):
    r = lax.rsqrt(jnp.mean(x * x, axis=-1, keepdims=True) + RMS_EPS)
    return x * r * g


def _rope(x, rope):
    c = rope[:, 0:LANES]
    s = rope[:, LANES:2 * LANES]
    half = ROT_DIM // 2
    lane = lax.broadcasted_iota(jnp.int32, (1, LANES), 1)
    on_x1 = jnp.bitwise_and(lane, HEAD_DIM - 1) < half
    partner = jnp.where(on_x1, pltpu.roll(x, LANES - half, 1), pltpu.roll(x, half, 1))
    return x * c + partner * s


def _in_proj(h, ln_ref, win_ref):
    xn = _rmsnorm(h, ln_ref[...]).astype(BF16)
    return jnp.dot(xn, win_ref[...], preferred_element_type=F32)


def _split_rope(p, rope):
    qb = [_rope(p[:, j * LANES:(j + 1) * LANES], rope) * Q_SCALE for j in range(Q_BLOCKS)]
    k = _rope(p[:, ATTN_WIDTH:ATTN_WIDTH + KV_WIDTH], rope)
    v = p[:, ATTN_WIDTH + KV_WIDTH:ATTN_WIDTH + 2 * KV_WIDTH]
    u = p[:, ATTN_WIDTH + 2 * KV_WIDTH:]
    return qb, k, v, u


def _stack_heads(qrows, lo):
    zero = jnp.zeros_like(qrows[0])
    parts = [jnp.where(lo, q, zero) for q in qrows] + [jnp.where(lo, zero, q) for q in qrows]
    return jnp.concatenate(parts, axis=0).astype(BF16)


def _softmax_sink(s, sink):
    m = jnp.maximum(jnp.max(s, axis=-1, keepdims=True), sink)
    e = jnp.exp2(s - m)
    l = jnp.sum(e, axis=-1, keepdims=True) + jnp.exp2(sink - m)
    return e.astype(BF16), 1.0 / l


def _window_sum(x, w, axis):
    span = 1
    while span < w:
        x = x + pltpu.roll(x, span, axis)
        span *= 2
    return x


def _pool_project(d_groups, pw_ref, ps_ref):
    outs = [jnp.dot(d.astype(BF16), pw_ref[g], preferred_element_type=F32)
            for g, d in enumerate(d_groups)]
    return jnp.concatenate(outs, axis=-1) * ps_ref[...]


def _project_finish(p, rope_ref, kin_ref, vin_ref, uin_ref, pw_ref, ps_ref,
                    kt_ref, vt_ref, ut_ref, ubuf, dst, src, *, tile, first, p0):
    dq, dk, dv, dm, _ = dst
    _, sk, sv, _, _ = src
    qb, k, v, u = _split_rope(p, rope_ref[...])
    lo = lax.broadcasted_iota(jnp.int32, (1, LANES), 1) < HEAD_DIM
    for r in range(tile // WINDOW):
        rows = slice(r * WINDOW, (r + 1) * WINDOW)
        dq[r] = _stack_heads([qb[j][rows] for j in range(Q_BLOCKS)], lo)
    dk[0:WINDOW, :] = jnp.where(first, kin_ref[0].astype(BF16), sk[tile:tile + WINDOW, :])
    dv[0:WINDOW, :] = jnp.where(first, vin_ref[0].astype(BF16), sv[tile:tile + WINDOW, :])
    dk[WINDOW:WINDOW + tile, :] = k.astype(BF16)
    dv[WINDOW:WINDOW + tile, :] = v.astype(BF16)
    kt_ref[0] = k[tile - WINDOW:, :]
    vt_ref[0] = v[tile - WINDOW:, :]
    ut_ref[0] = u[tile - HIST_ROWS:, :]

    ubuf[0:HIST_ROWS, :] = jnp.where(first, uin_ref[0], ubuf[tile:tile + HIST_ROWS, :])
    ubuf[HIST_ROWS:HIST_ROWS + tile, :] = u
    pos = p0 + lax.broadcasted_iota(jnp.int32, (tile, 1), 0)
    d_groups = []
    for g, w in enumerate(POOL_WINDOWS):
        cols = slice(g * POOL_GROUP, (g + 1) * POOL_GROUP)
        wsum = _window_sum(ubuf[:, cols], w, 0)[HIST_ROWS:]
        cnt = jnp.clip(pos + 1, 1, w).astype(F32)
        d_groups.append(wsum / cnt - u[:, cols])
    dm[...] = _pool_project(d_groups, pw_ref, ps_ref).astype(BF16)


def _attend_scores(src, *, tile):
    sq, sk, _, _, _ = src
    return [lax.dot_general(sq[r], sk[r * WINDOW:(r + 2) * WINDOW, :], (((1,), (1,)), ((), ())),
                            preferred_element_type=F32) for r in range(tile // WINDOW)]


def _attend_values(scores, sink_ref, src, *, layer, p0):
    _, _, sv, _, _ = src
    lo = lax.broadcasted_iota(jnp.int32, (1, LANES), 1) < HEAD_DIM
    qi = lax.broadcasted_iota(jnp.int32, (WINDOW, 2 * WINDOW), 0)
    kc = lax.broadcasted_iota(jnp.int32, (WINDOW, 2 * WINDOW), 1)
    band = (kc >= qi) & (kc <= qi + WINDOW)
    attn_rows = []
    for r, s in enumerate(scores):
        vb = sv[r * WINDOW:(r + 2) * WINDOW, :]
        kpos = p0 + (r - 1) * WINDOW + kc
        bias = jnp.where(band & (kpos >= 0), 0.0, NEG_INF)
        probs, inv = zip(*[_softmax_sink(s[hh * WINDOW:(hh + 1) * WINDOW] + bias, sink_ref[layer, hh] * LOG2E)
                           for hh in range(N_HEADS)])
        o = jnp.dot(jnp.concatenate(probs, axis=0), vb, preferred_element_type=F32)
        heads = [o[hh * WINDOW:(hh + 1) * WINDOW] * inv[hh] for hh in range(N_HEADS)]
        attn_rows.append(jnp.concatenate(
            [jnp.where(lo, heads[j], heads[j + Q_BLOCKS]) for j in range(Q_BLOCKS)], axis=-1))
    return jnp.concatenate(attn_rows, axis=0).astype(BF16)


def _mixer_long_kernel(sink_ref, hp_ref, hr_ref, rope_ref, kin_ref, vin_ref, uin_ref, ln_ref, win_ref,
                       pw_ref, ps_ref, wout_ref, ho_ref, kt_ref, vt_ref, ut_ref,
                       qa, ka, va, ma, xa, qb, kb, vb, mb, xb, ubuf,
                       *, tile, base_pos, layer, tiles_per_seq, n_tiles):
    i = pl.program_id(0)
    tp = lax.rem(jnp.minimum(i, n_tiles - 1), tiles_per_seq)
    ts = lax.rem(jnp.clip(i - 1, 0, n_tiles - 1), tiles_per_seq)
    set_a, set_b = (qa, ka, va, ma, xa), (qb, kb, vb, mb, xb)

    @pl.when(i == 0)
    def _():
        for ref in (kb, vb, ubuf):
            ref[...] = jnp.zeros(ref.shape, ref.dtype)

    def step(dst, src, project, attend, output):
        if attend:
            scores = _attend_scores(src, tile=tile)
        if project:
            p = _in_proj(hp_ref[0], ln_ref, win_ref)
        if output:
            ho_ref[0] = hr_ref[0] + jnp.dot(dst[4][...], wout_ref[...], preferred_element_type=F32)
        if attend:
            attn = _attend_values(scores, sink_ref, src, layer=layer, p0=base_pos + ts * tile)
            src[4][...] = jnp.concatenate([attn, src[3][...]], axis=-1)
        if project:
            _project_finish(p, rope_ref, kin_ref, vin_ref, uin_ref, pw_ref, ps_ref, kt_ref, vt_ref, ut_ref,
                            ubuf, dst, src, tile=tile, first=tp == 0, p0=base_pos + tp * tile)

    variants = {}
    for idx in range(n_tiles + 2):
        key = (idx < n_tiles, 1 <= idx <= n_tiles, idx >= 2, idx % 2)
        variants.setdefault(key, []).append(idx)
    parity = lax.rem(i, 2)
    for (project, attend, output, par), steps in variants.items():
        dst, src = (set_a, set_b) if par == 0 else (set_b, set_a)
        cond = (i >= steps[0]) & (i <= steps[-1]) & (parity == par)
        pl.when(cond)(functools.partial(step, dst, src, project, attend, output))


def _const_spec(shape):
    nd = len(shape)
    return pl.BlockSpec(shape, lambda *_: (0,) * nd, pipeline_mode=pl.Buffered(1))


def _layer_spec(shape, layer):
    nd = len(shape)
    return pl.BlockSpec((None,) + shape, lambda *_: (layer,) + (0,) * nd, pipeline_mode=pl.Buffered(1))


def _mixer_weight_specs(layer):
    return [
        _layer_spec((1, D_MODEL), layer),
        _layer_spec((D_MODEL, IN_WIDTH), layer),
        _layer_spec((len(POOL_WINDOWS), POOL_GROUP, POOL_GROUP), layer),
        _layer_spec((1, POOL_WIDTH), layer),
        _layer_spec((D_MODEL, D_MODEL), layer),
    ]


def _mixer_long(h, rope, kin, vin, uin, sinks, ln, win, pw, ps, wout, *, tile, base_pos, layer, name):
    n_seq, seq_len, _ = h.shape
    tps = seq_len // tile
    n_tiles = n_seq * tps
    shared_init = kin.shape[0] == 1

    def proj_tile(i):
        return jnp.minimum(i, n_tiles - 1)

    def out_tile(i):
        return jnp.maximum(i - 2, 0)

    def init_idx(i):
        return (0 if shared_init else proj_tile(i) // tps, 0, 0)

    kern = functools.partial(_mixer_long_kernel, tile=tile, base_pos=base_pos, layer=layer,
                             tiles_per_seq=tps, n_tiles=n_tiles)
    buffer_set = [
        pltpu.VMEM((tile // WINDOW, N_HEADS * WINDOW, LANES), BF16),
        pltpu.VMEM((WINDOW + tile, KV_WIDTH), BF16),
        pltpu.VMEM((WINDOW + tile, KV_WIDTH), BF16),
        pltpu.VMEM((tile, POOL_WIDTH), BF16),
        pltpu.VMEM((tile, D_MODEL), BF16),
    ]
    return pl.pallas_call(
        kern,
        grid=(n_tiles + 2,),
        in_specs=[
            pl.BlockSpec(memory_space=pltpu.SMEM),
            pl.BlockSpec((1, tile, D_MODEL), lambda i: (proj_tile(i) // tps, proj_tile(i) % tps, 0)),
            pl.BlockSpec((1, tile, D_MODEL), lambda i: (out_tile(i) // tps, out_tile(i) % tps, 0)),
            pl.BlockSpec((tile, 2 * LANES), lambda i: (proj_tile(i) % tps, 0)),
            pl.BlockSpec((1, WINDOW, KV_WIDTH), init_idx),
            pl.BlockSpec((1, WINDOW, KV_WIDTH), init_idx),
            pl.BlockSpec((1, HIST_ROWS, POOL_WIDTH), init_idx),
        ] + _mixer_weight_specs(layer),
        out_specs=[
            pl.BlockSpec((1, tile, D_MODEL), lambda i: (out_tile(i) // tps, out_tile(i) % tps, 0)),
            pl.BlockSpec((1, WINDOW, KV_WIDTH), lambda i: (proj_tile(i) // tps, 0, 0)),
            pl.BlockSpec((1, WINDOW, KV_WIDTH), lambda i: (proj_tile(i) // tps, 0, 0)),
            pl.BlockSpec((1, HIST_ROWS, POOL_WIDTH), lambda i: (proj_tile(i) // tps, 0, 0)),
        ],
        out_shape=[
            jax.ShapeDtypeStruct((n_seq, seq_len, D_MODEL), F32),
            jax.ShapeDtypeStruct((n_seq, WINDOW, KV_WIDTH), F32),
            jax.ShapeDtypeStruct((n_seq, WINDOW, KV_WIDTH), F32),
            jax.ShapeDtypeStruct((n_seq, HIST_ROWS, POOL_WIDTH), F32),
        ],
        scratch_shapes=buffer_set + buffer_set + [pltpu.VMEM((HIST_ROWS + tile, POOL_WIDTH), F32)],
        compiler_params=pltpu.CompilerParams(
            dimension_semantics=("arbitrary",), vmem_limit_bytes=VMEM_LIMIT),
        name=name,
    )(sinks, h, h, rope, kin, vin, uin, ln, win, pw, ps, wout)


def _mixer_short_kernel(sink_ref, h_ref, rope_ref, ck_ref, cv_ref, st_ref, ln_ref, win_ref,
                        pw_ref, ps_ref, wout_ref, *rest, seqs, layer):
    if layer:
        kprev_ref, vprev_ref = rest[:2]
        rest = rest[2:]
    ho_ref, kc_ref, vc_ref, un_ref, qbuf, abuf, ubuf = rest
    if layer:
        kc_ref[0:layer] = kprev_ref[...]
        vc_ref[0:layer] = vprev_ref[...]
    rows = seqs * DEC_SEQ
    h = h_ref[...]
    qb, k, v, u = _split_rope(_in_proj(h, ln_ref, win_ref), rope_ref[...])
    for j in range(Q_BLOCKS):
        qbuf[:, j * LANES:(j + 1) * LANES] = qb[j]
    kept = WINDOW - DEC_SEQ
    kc_ref[layer, :, 0:kept, :] = ck_ref[:, DEC_SEQ:, :]
    vc_ref[layer, :, 0:kept, :] = cv_ref[:, DEC_SEQ:, :]
    kc_ref[layer, :, kept:, :] = k.reshape(seqs, DEC_SEQ, KV_WIDTH)
    vc_ref[layer, :, kept:, :] = v.reshape(seqs, DEC_SEQ, KV_WIDTH)
    un_ref[...] = u
    ubuf[:, 0:HIST_ROWS, :] = st_ref[...]
    ubuf[:, HIST_ROWS:HIST_ROWS + DEC_SEQ, :] = u.reshape(seqs, DEC_SEQ, POOL_WIDTH)

    stacked = N_HEADS * DEC_SEQ
    keys = WINDOW + 2 * DEC_SEQ
    lo = lax.broadcasted_iota(jnp.int32, (1, LANES), 1) < HEAD_DIM
    srow = lax.broadcasted_iota(jnp.int32, (stacked, keys), 0)
    kc = lax.broadcasted_iota(jnp.int32, (stacked, keys), 1)
    qi = jnp.bitwise_and(srow, DEC_SEQ - 1)
    bias = jnp.where((kc >= qi) & (kc <= qi + WINDOW), 0.0, NEG_INF)
    hrow = jnp.right_shift(lax.broadcasted_iota(jnp.int32, (stacked, 1), 0), DEC_SEQ.bit_length() - 1)
    sink = jnp.zeros((stacked, 1), F32)
    for hh in range(N_HEADS):
        sink = jnp.where(hrow == hh, sink_ref[layer, hh] * LOG2E, sink)
    zpad = jnp.zeros((DEC_SEQ, KV_WIDTH), F32)

    def group_body(gi, carry):
        ids = [gi * SAMPLE_UNROLL + n for n in range(SAMPLE_UNROLL)]
        starts = [pl.multiple_of(i * DEC_SEQ, DEC_SEQ) for i in ids]
        scores = []
        for i, r0 in zip(ids, starts):
            qrows = qbuf[pl.ds(r0, DEC_SEQ), :]
            q = _stack_heads([qrows[:, j * LANES:(j + 1) * LANES] for j in range(Q_BLOCKS)], lo)
            kall = jnp.concatenate([ck_ref[i], kc_ref[layer, i, kept:, :], zpad], axis=0).astype(BF16)
            scores.append(lax.dot_general(q, kall, (((1,), (1,)), ((), ())), preferred_element_type=F32))
        probs = [_softmax_sink(s + bias, sink) for s in scores]
        for i, r0, (prob, inv) in zip(ids, starts, probs):
            vall = jnp.concatenate([cv_ref[i], vc_ref[layer, i, kept:, :], zpad], axis=0).astype(BF16)
            o = jnp.dot(prob, vall, preferred_element_type=F32) * inv
            abuf[pl.ds(r0, DEC_SEQ), :] = jnp.concatenate(
                [jnp.where(lo, o[j * DEC_SEQ:(j + 1) * DEC_SEQ],
                           o[(j + Q_BLOCKS) * DEC_SEQ:(j + Q_BLOCKS + 1) * DEC_SEQ])
                 for j in range(Q_BLOCKS)], axis=-1)
        return carry

    lax.fori_loop(0, seqs // SAMPLE_UNROLL, group_body, 0)

    d_groups = []
    for g, w in enumerate(POOL_WINDOWS):
        cols = slice(g * POOL_GROUP, (g + 1) * POOL_GROUP)
        wsum = _window_sum(ubuf[:, :, cols], w, 1)[:, HIST_ROWS:, :]
        cnt = float(min(PAST_LEN + 1, w))
        d_groups.append((wsum / cnt).reshape(rows, POOL_GROUP) - u[:, cols])
    pm = _pool_project(d_groups, pw_ref, ps_ref)

    mix = jnp.concatenate([abuf[...], pm], axis=-1).astype(BF16)
    ho_ref[...] = h + jnp.dot(mix, wout_ref[...], preferred_element_type=F32)


def _mixer_short(h, rope, ck, cv, st, sinks, ln, win, pw, ps, wout, kc_prev, vc_prev, *, seqs, layer, name):
    n_rows = h.shape[0]
    rows = seqs * DEC_SEQ
    n_seq = n_rows // DEC_SEQ
    stacked_spec = pl.BlockSpec((layer + 1, seqs, WINDOW, KV_WIDTH), lambda i: (0, i, 0, 0))
    prev_specs = [pl.BlockSpec((layer, seqs, WINDOW, KV_WIDTH), lambda i: (0, i, 0, 0))] * 2 if layer else []
    prev_args = (kc_prev, vc_prev) if layer else ()
    kern = functools.partial(_mixer_short_kernel, seqs=seqs, layer=layer)
    return pl.pallas_call(
        kern,
        grid=(n_rows // rows,),
        in_specs=[
            pl.BlockSpec(memory_space=pltpu.SMEM),
            pl.BlockSpec((rows, D_MODEL), lambda i: (i, 0)),
            _const_spec((rows, 2 * LANES)),
            pl.BlockSpec((None, seqs, WINDOW, KV_WIDTH), lambda i: (layer, i, 0, 0)),
            pl.BlockSpec((None, seqs, WINDOW, KV_WIDTH), lambda i: (layer, i, 0, 0)),
            pl.BlockSpec((seqs, HIST_ROWS, POOL_WIDTH), lambda i: (i, 0, 0)),
        ] + _mixer_weight_specs(layer) + prev_specs,
        out_specs=[
            pl.BlockSpec((rows, D_MODEL), lambda i: (i, 0)),
            stacked_spec,
            stacked_spec,
            pl.BlockSpec((rows, POOL_WIDTH), lambda i: (i, 0)),
        ],
        out_shape=[
            jax.ShapeDtypeStruct((n_rows, D_MODEL), F32),
            jax.ShapeDtypeStruct((layer + 1, n_seq, WINDOW, KV_WIDTH), F32),
            jax.ShapeDtypeStruct((layer + 1, n_seq, WINDOW, KV_WIDTH), F32),
            jax.ShapeDtypeStruct((n_rows, POOL_WIDTH), F32),
        ],
        scratch_shapes=[
            pltpu.VMEM((rows, ATTN_WIDTH), F32),
            pltpu.VMEM((rows, ATTN_WIDTH), F32),
            pltpu.VMEM((seqs, HIST_ROWS + DEC_SEQ, POOL_WIDTH), F32),
        ],
        compiler_params=pltpu.CompilerParams(
            dimension_semantics=("arbitrary",), vmem_limit_bytes=VMEM_LIMIT),
        name=name,
    )(sinks, h, rope, ck, cv, st, ln, win, pw, ps, wout, *prev_args)


def _mlp_kernel(h_ref, ln_ref, wup_ref, wdown_ref, lnf_ref, o_ref, *, final_norm):
    h = h_ref[...]
    xn = _rmsnorm(h, ln_ref[...]).astype(BF16)
    a = jnp.maximum(jnp.dot(xn, wup_ref[...], preferred_element_type=F32), 0.0)
    out = h + jnp.dot((a * a).astype(BF16), wdown_ref[...], preferred_element_type=F32)
    if final_norm:
        out = _rmsnorm(out, lnf_ref[...])
    o_ref[...] = out


def _mlp(h, ln, wup, wdown, lnf, *, final_norm, layer, name):
    n_rows = h.shape[0]
    tile = min(MLP_TILE, n_rows)
    kern = functools.partial(_mlp_kernel, final_norm=final_norm)
    return pl.pallas_call(
        kern,
        grid=(n_rows // tile,),
        in_specs=[
            pl.BlockSpec((tile, D_MODEL), lambda i: (i, 0)),
            _layer_spec((1, D_MODEL), layer),
            _layer_spec((D_MODEL, D_FF), layer),
            _layer_spec((D_FF, D_MODEL), layer),
            _const_spec((1, D_MODEL)),
        ],
        out_specs=pl.BlockSpec((tile, D_MODEL), lambda i: (i, 0)),
        out_shape=jax.ShapeDtypeStruct((n_rows, D_MODEL), F32),
        compiler_params=pltpu.CompilerParams(
            dimension_semantics=("arbitrary",), vmem_limit_bytes=VMEM_LIMIT),
        name=name,
    )(h, ln, wup, wdown, lnf)


def _rope_table(pos):
    n = pos.shape[0]
    inv_freq = ROPE_THETA ** (-jnp.arange(0, ROT_DIM, 2, dtype=F32) / ROT_DIM)
    ang = pos.astype(F32)[:, None] * inv_freq[None, :]
    cos, sin = jnp.cos(ang), jnp.sin(ang)
    rest = HEAD_DIM - ROT_DIM
    c = jnp.concatenate([cos, cos, jnp.ones((n, rest), F32)], axis=-1)
    s = jnp.concatenate([-sin, sin, jnp.zeros((n, rest), F32)], axis=-1)
    reps = LANES // HEAD_DIM
    return jnp.concatenate([jnp.tile(c, (1, reps)), jnp.tile(s, (1, reps))], axis=-1)


def _pair_heads(w, axis):
    shape = w.shape
    split = shape[:axis] + (N_HEADS // Q_BLOCKS, Q_BLOCKS, HEAD_DIM) + shape[axis + 1:]
    return jnp.swapaxes(w.reshape(split), axis, axis + 1).reshape(shape)


def kernel(x_prompt, x_sample, cache_k, cache_v, state_pool, meta_tokens, ln1, w_in, attn_sinks,
           pool_w, pool_scale, w_out, ln2, w_up, w_down, ln_f):
    batch, seq, _ = x_prompt.shape
    dec_batch, dec_seq, _ = x_sample.shape
    depth = w_in.shape[0]
    assert dec_seq == DEC_SEQ and seq % PROMPT_TILE == 0 and dec_batch % SAMPLE_SEQS == 0

    pad = WINDOW - N_META
    rope_meta = _rope_table(jnp.arange(WINDOW) - pad)
    rope_prompt = _rope_table(N_META + jnp.arange(seq))
    rope_sample = jnp.tile(_rope_table(PAST_LEN + jnp.arange(DEC_SEQ)), (SAMPLE_SEQS, 1))

    win = jnp.concatenate([_pair_heads(w_in[:, :, :ATTN_WIDTH], 2), w_in[:, :, ATTN_WIDTH:]], axis=2).astype(BF16)
    wout = jnp.concatenate([_pair_heads(w_out[:, :ATTN_WIDTH], 1), w_out[:, ATTN_WIDTH:]], axis=1).astype(BF16)
    wup = w_up.astype(BF16)
    wdown = w_down.astype(BF16)
    pw = pool_w.astype(BF16)
    ps = pool_scale.reshape(depth, 1, POOL_WIDTH)
    l1 = ln1.reshape(depth, 1, D_MODEL)
    l2 = ln2.reshape(depth, 1, D_MODEL)
    lnf = ln_f.reshape(1, D_MODEL)
    sinks = attn_sinks.astype(F32)
    mixer_w = (sinks, l1, win, pw, ps, wout)

    hm = jnp.concatenate([jnp.zeros((pad, D_MODEL), F32), meta_tokens.astype(F32)], axis=0)[None]
    hp = x_prompt
    hs = x_sample.reshape(dec_batch * dec_seq, D_MODEL)
    zero_kv = jnp.zeros((1, WINDOW, KV_WIDTH), F32)
    zero_u = jnp.zeros((1, HIST_ROWS, POOL_WIDTH), F32)
    ck = cache_k.reshape(depth, dec_batch, WINDOW, KV_WIDTH)
    cv = cache_v.reshape(depth, dec_batch, WINDOW, KV_WIDTH)

    pk, pv, pu, su = [], [], [], []
    kc = vc = None
    for l in range(depth):
        last = l == depth - 1
        hm_mid, km, vm, um = _mixer_long(hm, rope_meta, zero_kv, zero_kv, zero_u, *mixer_w, tile=WINDOW,
                                         base_pos=-pad, layer=l, name=f"mixer_meta_{l}")
        hp_mid, kt, vt, ut = _mixer_long(hp, rope_prompt, km, vm, um, *mixer_w, tile=PROMPT_TILE,
                                         base_pos=N_META, layer=l, name=f"mixer_prompt_{l}")
        st = jnp.pad(state_pool[l], ((0, 0), (1, 0), (0, 0)))
        hs_mid, kc, vc, un = _mixer_short(hs, rope_sample, ck, cv, st, *mixer_w, kc, vc, seqs=SAMPLE_SEQS,
                                          layer=l, name=f"mixer_sample_{l}")

        mlp_w = (l2, wup, wdown, lnf)
        if not last:
            hm = _mlp(hm_mid[0], *mlp_w, final_norm=False, layer=l, name=f"mlp_meta_{l}")[None]
        hp = _mlp(hp_mid.reshape(batch * seq, D_MODEL), *mlp_w, final_norm=last, layer=l,
                  name=f"mlp_prompt_{l}").reshape(batch, seq, D_MODEL)
        hs = _mlp(hs_mid, *mlp_w, final_norm=last, layer=l, name=f"mlp_sample_{l}")

        pk.append(kt.reshape(batch, WINDOW, N_KV_HEADS, HEAD_DIM))
        pv.append(vt.reshape(batch, WINDOW, N_KV_HEADS, HEAD_DIM))
        pu.append(ut[:, 1:, :])
        un = un.reshape(dec_batch, dec_seq, POOL_WIDTH)
        su.append(jnp.concatenate([state_pool[l][:, dec_seq:], un], axis=1))

    y_sample = hs.reshape(dec_batch, dec_seq, D_MODEL)
    cache_shape = (depth, dec_batch, WINDOW, N_KV_HEADS, HEAD_DIM)
    return (hp, y_sample, jnp.stack(pk), jnp.stack(pv), jnp.stack(pu),
            kc.reshape(cache_shape), vc.reshape(cache_shape), jnp.stack(su))
```

```python
import functools

import jax
import jax.numpy as jnp
from jax import lax
from jax.experimental import pallas as pl
from jax.experimental.pallas import tpu as pltpu

D_MODEL = 1024
N_HEADS = 8
N_KV_HEADS = 2
HEAD_DIM = 64
ATTN_WIDTH = N_HEADS * HEAD_DIM
KV_WIDTH = N_KV_HEADS * HEAD_DIM
POOL_WINDOWS = (2, 4, 8, 16)
POOL_WIDTH = D_MODEL - ATTN_WIDTH
POOL_GROUP = POOL_WIDTH // len(POOL_WINDOWS)
POOL_HIST = max(POOL_WINDOWS) - 1
IN_WIDTH = ATTN_WIDTH + 2 * KV_WIDTH + POOL_WIDTH
WINDOW = 128
ROT_DIM = HEAD_DIM // 4
ROPE_THETA = 500000.0
D_FF = 4 * D_MODEL
N_META = 16
RMS_EPS = 1e-5
PAST_LEN = 16384
DEC_SEQ = 8

LANES = 128
HIST_ROWS = POOL_HIST + 1
Q_BLOCKS = ATTN_WIDTH // LANES
LOG2E = 1.4426950408889634
Q_SCALE = HEAD_DIM ** -0.5 * LOG2E
NEG_INF = float("-inf")

PROMPT_TILE = 512
SAMPLE_SEQS = 32
SAMPLE_UNROLL = 16
MLP_TILE = 1024
MLP_CHUNK = 1024
VMEM_LIMIT = 56 * 1024 * 1024

F32 = jnp.float32
BF16 = jnp.bfloat16


def _rmsnorm(x, g):
    r = lax.rsqrt(jnp.mean(x * x, axis=-1, keepdims=True) + RMS_EPS)
    return x * r * g


def _rope(x, rope):
    c = rope[:, 0:LANES]
    s = rope[:, LANES:2 * LANES]
    half = ROT_DIM // 2
    lane = lax.broadcasted_iota(jnp.int32, (1, LANES), 1)
    on_x1 = jnp.bitwise_and(lane, HEAD_DIM - 1) < half
    partner = jnp.where(on_x1, pltpu.roll(x, LANES - half, 1), pltpu.roll(x, half, 1))
    return x * c + partner * s


def _in_proj(h, ln_ref, win_ref):
    xn = _rmsnorm(h, ln_ref[...]).astype(BF16)
    return jnp.dot(xn, win_ref[...], preferred_element_type=F32)


def _split_rope(p, rope):
    qb = [_rope(p[:, j * LANES:(j + 1) * LANES], rope) * Q_SCALE for j in range(Q_BLOCKS)]
    k = _rope(p[:, ATTN_WIDTH:ATTN_WIDTH + KV_WIDTH], rope)
    v = p[:, ATTN_WIDTH + KV_WIDTH:ATTN_WIDTH + 2 * KV_WIDTH]
    u = p[:, ATTN_WIDTH + 2 * KV_WIDTH:]
    return qb, k, v, u


def _stack_heads(qrows, lo):
    zero = jnp.zeros_like(qrows[0])
    parts = [jnp.where(lo, q, zero) for q in qrows] + [jnp.where(lo, zero, q) for q in qrows]
    return jnp.concatenate(parts, axis=0).astype(BF16)


def _softmax_sink(s, sink):
    m = jnp.maximum(jnp.max(s, axis=-1, keepdims=True), sink)
    e = jnp.exp2(s - m)
    l = jnp.sum(e, axis=-1, keepdims=True) + jnp.exp2(sink - m)
    return e.astype(BF16), 1.0 / l


def _window_sum(x, w, axis):
    span = 1
    while span < w:
        x = x + pltpu.roll(x, span, axis)
        span *= 2
    return x


def _pool_project(d_groups, pw_ref, ps_ref):
    outs = [jnp.dot(d.astype(BF16), pw_ref[g], preferred_element_type=F32)
            for g, d in enumerate(d_groups)]
    return jnp.concatenate(outs, axis=-1) * ps_ref[...]


def _project_finish(p, rope_ref, kin_ref, vin_ref, uin_ref, pw_ref, ps_ref,
                    kt_ref, vt_ref, ut_ref, ubuf, dst, src, *, tile, first, p0):
    dq, dk, dv, dm, _ = dst
    _, sk, sv, _, _ = src
    qb, k, v, u = _split_rope(p, rope_ref[...])
    lo = lax.broadcasted_iota(jnp.int32, (1, LANES), 1) < HEAD_DIM
    for r in range(tile // WINDOW):
        rows = slice(r * WINDOW, (r + 1) * WINDOW)
        dq[r] = _stack_heads([qb[j][rows] for j in range(Q_BLOCKS)], lo)
    dk[0:WINDOW, :] = jnp.where(first, kin_ref[0].astype(BF16), sk[tile:tile + WINDOW, :])
    dv[0:WINDOW, :] = jnp.where(first, vin_ref[0].astype(BF16), sv[tile:tile + WINDOW, :])
    dk[WINDOW:WINDOW + tile, :] = k.astype(BF16)
    dv[WINDOW:WINDOW + tile, :] = v.astype(BF16)
    kt_ref[0] = k[tile - WINDOW:, :]
    vt_ref[0] = v[tile - WINDOW:, :]
    ut_ref[0] = u[tile - HIST_ROWS:, :]

    ubuf[0:HIST_ROWS, :] = jnp.where(first, uin_ref[0], ubuf[tile:tile + HIST_ROWS, :])
    ubuf[HIST_ROWS:HIST_ROWS + tile, :] = u
    pos = p0 + lax.broadcasted_iota(jnp.int32, (tile, 1), 0)
    d_groups = []
    for g, w in enumerate(POOL_WINDOWS):
        cols = slice(g * POOL_GROUP, (g + 1) * POOL_GROUP)
        wsum = _window_sum(ubuf[:, cols], w, 0)[HIST_ROWS:]
        cnt = jnp.clip(pos + 1, 1, w).astype(F32)
        d_groups.append(wsum / cnt - u[:, cols])
    dm[...] = _pool_project(d_groups, pw_ref, ps_ref).astype(BF16)


def _attend_scores(src, *, tile):
    sq, sk, _, _, _ = src
    return [lax.dot_general(sq[r], sk[r * WINDOW:(r + 2) * WINDOW, :], (((1,), (1,)), ((), ())),
                            preferred_element_type=F32) for r in range(tile // WINDOW)]


def _attend_values(scores, sink_ref, src, *, layer, p0):
    _, _, sv, _, _ = src
    lo = lax.broadcasted_iota(jnp.int32, (1, LANES), 1) < HEAD_DIM
    qi = lax.broadcasted_iota(jnp.int32, (WINDOW, 2 * WINDOW), 0)
    kc = lax.broadcasted_iota(jnp.int32, (WINDOW, 2 * WINDOW), 1)
    band = (kc >= qi) & (kc <= qi + WINDOW)
    attn_rows = []
    for r, s in enumerate(scores):
        vb = sv[r * WINDOW:(r + 2) * WINDOW, :]
        kpos = p0 + (r - 1) * WINDOW + kc
        bias = jnp.where(band & (kpos >= 0), 0.0, NEG_INF)
        probs, inv = zip(*[_softmax_sink(s[hh * WINDOW:(hh + 1) * WINDOW] + bias, sink_ref[layer, hh] * LOG2E)
                           for hh in range(N_HEADS)])
        o = jnp.dot(jnp.concatenate(probs, axis=0), vb, preferred_element_type=F32)
        heads = [o[hh * WINDOW:(hh + 1) * WINDOW] * inv[hh] for hh in range(N_HEADS)]
        attn_rows.append(jnp.concatenate(
            [jnp.where(lo, heads[j], heads[j + Q_BLOCKS]) for j in range(Q_BLOCKS)], axis=-1))
    return jnp.concatenate(attn_rows, axis=0).astype(BF16)


def _mixer_long_kernel(sink_ref, hp_ref, hr_ref, rope_ref, kin_ref, vin_ref, uin_ref, ln_ref, win_ref,
                       pw_ref, ps_ref, wout_ref, ho_ref, kt_ref, vt_ref, ut_ref,
                       qa, ka, va, ma, xa, qb, kb, vb, mb, xb, ubuf,
                       *, tile, base_pos, layer, tiles_per_seq, n_tiles):
    i = pl.program_id(0)
    tp = lax.rem(jnp.minimum(i, n_tiles - 1), tiles_per_seq)
    ts = lax.rem(jnp.clip(i - 1, 0, n_tiles - 1), tiles_per_seq)
    set_a, set_b = (qa, ka, va, ma, xa), (qb, kb, vb, mb, xb)

    @pl.when(i == 0)
    def _():
        for ref in (kb, vb, ubuf):
            ref[...] = jnp.zeros(ref.shape, ref.dtype)

    def step(dst, src, project, attend, output):
        if attend:
            scores = _attend_scores(src, tile=tile)
        if project:
            p = _in_proj(hp_ref[0], ln_ref, win_ref)
        if output:
            ho_ref[0] = hr_ref[0] + jnp.dot(dst[4][...], wout_ref[...], preferred_element_type=F32)
        if attend:
            attn = _attend_values(scores, sink_ref, src, layer=layer, p0=base_pos + ts * tile)
            src[4][...] = jnp.concatenate([attn, src[3][...]], axis=-1)
        if project:
            _project_finish(p, rope_ref, kin_ref, vin_ref, uin_ref, pw_ref, ps_ref, kt_ref, vt_ref, ut_ref,
                            ubuf, dst, src, tile=tile, first=tp == 0, p0=base_pos + tp * tile)

    variants = {}
    for idx in range(n_tiles + 2):
        key = (idx < n_tiles, 1 <= idx <= n_tiles, idx >= 2, idx % 2)
        variants.setdefault(key, []).append(idx)
    parity = lax.rem(i, 2)
    for (project, attend, output, par), steps in variants.items():
        dst, src = (set_a, set_b) if par == 0 else (set_b, set_a)
        cond = (i >= steps[0]) & (i <= steps[-1]) & (parity == par)
        pl.when(cond)(functools.partial(step, dst, src, project, attend, output))


def _const_spec(shape):
    nd = len(shape)
    return pl.BlockSpec(shape, lambda *_: (0,) * nd, pipeline_mode=pl.Buffered(1))


def _layer_spec(shape, layer):
    nd = len(shape)
    return pl.BlockSpec((None,) + shape, lambda *_: (layer,) + (0,) * nd, pipeline_mode=pl.Buffered(1))


def _mixer_weight_specs(layer):
    return [
        _layer_spec((1, D_MODEL), layer),
        _layer_spec((D_MODEL, IN_WIDTH), layer),
        _layer_spec((len(POOL_WINDOWS), POOL_GROUP, POOL_GROUP), layer),
        _layer_spec((1, POOL_WIDTH), layer),
        _layer_spec((D_MODEL, D_MODEL), layer),
    ]


def _mixer_long(h, rope, kin, vin, uin, sinks, ln, win, pw, ps, wout, *, tile, base_pos, layer, name):
    n_seq, seq_len, _ = h.shape
    tps = seq_len // tile
    n_tiles = n_seq * tps
    shared_init = kin.shape[0] == 1

    def proj_tile(i):
        return jnp.minimum(i, n_tiles - 1)

    def out_tile(i):
        return jnp.maximum(i - 2, 0)

    def init_idx(i):
        return (0 if shared_init else proj_tile(i) // tps, 0, 0)

    kern = functools.partial(_mixer_long_kernel, tile=tile, base_pos=base_pos, layer=layer,
                             tiles_per_seq=tps, n_tiles=n_tiles)
    buffer_set = [
        pltpu.VMEM((tile // WINDOW, N_HEADS * WINDOW, LANES), BF16),
        pltpu.VMEM((WINDOW + tile, KV_WIDTH), BF16),
        pltpu.VMEM((WINDOW + tile, KV_WIDTH), BF16),
        pltpu.VMEM((tile, POOL_WIDTH), BF16),
        pltpu.VMEM((tile, D_MODEL), BF16),
    ]
    return pl.pallas_call(
        kern,
        grid=(n_tiles + 2,),
        in_specs=[
            pl.BlockSpec(memory_space=pltpu.SMEM),
            pl.BlockSpec((1, tile, D_MODEL), lambda i: (proj_tile(i) // tps, proj_tile(i) % tps, 0)),
            pl.BlockSpec((1, tile, D_MODEL), lambda i: (out_tile(i) // tps, out_tile(i) % tps, 0)),
            pl.BlockSpec((tile, 2 * LANES), lambda i: (proj_tile(i) % tps, 0)),
            pl.BlockSpec((1, WINDOW, KV_WIDTH), init_idx),
            pl.BlockSpec((1, WINDOW, KV_WIDTH), init_idx),
            pl.BlockSpec((1, HIST_ROWS, POOL_WIDTH), init_idx),
        ] + _mixer_weight_specs(layer),
        out_specs=[
            pl.BlockSpec((1, tile, D_MODEL), lambda i: (out_tile(i) // tps, out_tile(i) % tps, 0)),
            pl.BlockSpec((1, WINDOW, KV_WIDTH), lambda i: (proj_tile(i) // tps, 0, 0)),
            pl.BlockSpec((1, WINDOW, KV_WIDTH), lambda i: (proj_tile(i) // tps, 0, 0)),
            pl.BlockSpec((1, HIST_ROWS, POOL_WIDTH), lambda i: (proj_tile(i) // tps, 0, 0)),
        ],
        out_shape=[
            jax.ShapeDtypeStruct((n_seq, seq_len, D_MODEL), F32),
            jax.ShapeDtypeStruct((n_seq, WINDOW, KV_WIDTH), F32),
            jax.ShapeDtypeStruct((n_seq, WINDOW, KV_WIDTH), F32),
            jax.ShapeDtypeStruct((n_seq, HIST_ROWS, POOL_WIDTH), F32),
        ],
        scratch_shapes=buffer_set + buffer_set + [pltpu.VMEM((HIST_ROWS + tile, POOL_WIDTH), F32)],
        compiler_params=pltpu.CompilerParams(
            dimension_semantics=("arbitrary",), vmem_limit_bytes=VMEM_LIMIT),
        name=name,
    )(sinks, h, h, rope, kin, vin, uin, ln, win, pw, ps, wout)


def _mixer_short_kernel(sink_ref, h_ref, rope_ref, ck_ref, cv_ref, st_ref, ln_ref, win_ref,
                        pw_ref, ps_ref, wout_ref, *rest, seqs, layer):
    if layer:
        kprev_ref, vprev_ref = rest[:2]
        rest = rest[2:]
    ho_ref, kc_ref, vc_ref, un_ref, qbuf, abuf, ubuf = rest
    if layer:
        kc_ref[0:layer] = kprev_ref[...]
        vc_ref[0:layer] = vprev_ref[...]
    rows = seqs * DEC_SEQ
    h = h_ref[...]
    qb, k, v, u = _split_rope(_in_proj(h, ln_ref, win_ref), rope_ref[...])
    for j in range(Q_BLOCKS):
        qbuf[:, j * LANES:(j + 1) * LANES] = qb[j]
    kept = WINDOW - DEC_SEQ
    kc_ref[layer, :, 0:kept, :] = ck_ref[:, DEC_SEQ:, :]
    vc_ref[layer, :, 0:kept, :] = cv_ref[:, DEC_SEQ:, :]
    kc_ref[layer, :, kept:, :] = k.reshape(seqs, DEC_SEQ, KV_WIDTH)
    vc_ref[layer, :, kept:, :] = v.reshape(seqs, DEC_SEQ, KV_WIDTH)
    un_ref[...] = u
    ubuf[:, 0:HIST_ROWS, :] = st_ref[...]
    ubuf[:, HIST_ROWS:HIST_ROWS + DEC_SEQ, :] = u.reshape(seqs, DEC_SEQ, POOL_WIDTH)

    stacked = N_HEADS * DEC_SEQ
    keys = WINDOW + 2 * DEC_SEQ
    lo = lax.broadcasted_iota(jnp.int32, (1, LANES), 1) < HEAD_DIM
    srow = lax.broadcasted_iota(jnp.int32, (stacked, keys), 0)
    kc = lax.broadcasted_iota(jnp.int32, (stacked, keys), 1)
    qi = jnp.bitwise_and(srow, DEC_SEQ - 1)
    bias = jnp.where((kc >= qi) & (kc <= qi + WINDOW), 0.0, NEG_INF)
    hrow = jnp.right_shift(lax.broadcasted_iota(jnp.int32, (stacked, 1), 0), DEC_SEQ.bit_length() - 1)
    sink = jnp.zeros((stacked, 1), F32)
    for hh in range(N_HEADS):
        sink = jnp.where(hrow == hh, sink_ref[layer, hh] * LOG2E, sink)
    zpad = jnp.zeros((DEC_SEQ, KV_WIDTH), F32)

    def group_body(gi, carry):
        ids = [gi * SAMPLE_UNROLL + n for n in range(SAMPLE_UNROLL)]
        starts = [pl.multiple_of(i * DEC_SEQ, DEC_SEQ) for i in ids]
        scores = []
        for i, r0 in zip(ids, starts):
            qrows = qbuf[pl.ds(r0, DEC_SEQ), :]
            q = _stack_heads([qrows[:, j * LANES:(j + 1) * LANES] for j in range(Q_BLOCKS)], lo)
            kall = jnp.concatenate([ck_ref[i], kc_ref[layer, i, kept:, :], zpad], axis=0).astype(BF16)
            scores.append(lax.dot_general(q, kall, (((1,), (1,)), ((), ())), preferred_element_type=F32))
        probs = [_softmax_sink(s + bias, sink) for s in scores]
        for i, r0, (prob, inv) in zip(ids, starts, probs):
            vall = jnp.concatenate([cv_ref[i], vc_ref[layer, i, kept:, :], zpad], axis=0).astype(BF16)
            o = jnp.dot(prob, vall, preferred_element_type=F32) * inv
            abuf[pl.ds(r0, DEC_SEQ), :] = jnp.concatenate(
                [jnp.where(lo, o[j * DEC_SEQ:(j + 1) * DEC_SEQ],
                           o[(j + Q_BLOCKS) * DEC_SEQ:(j + Q_BLOCKS + 1) * DEC_SEQ])
                 for j in range(Q_BLOCKS)], axis=-1)
        return carry

    lax.fori_loop(0, seqs // SAMPLE_UNROLL, group_body, 0)

    d_groups = []
    for g, w in enumerate(POOL_WINDOWS):
        cols = slice(g * POOL_GROUP, (g + 1) * POOL_GROUP)
        wsum = _window_sum(ubuf[:, :, cols], w, 1)[:, HIST_ROWS:, :]
        cnt = float(min(PAST_LEN + 1, w))
        d_groups.append((wsum / cnt).reshape(rows, POOL_GROUP) - u[:, cols])
    pm = _pool_project(d_groups, pw_ref, ps_ref)

    mix = jnp.concatenate([abuf[...], pm], axis=-1).astype(BF16)
    ho_ref[...] = h + jnp.dot(mix, wout_ref[...], preferred_element_type=F32)


def _mixer_short(h, rope, ck, cv, st, sinks, ln, win, pw, ps, wout, kc_prev, vc_prev, *, seqs, layer, name):
    n_rows = h.shape[0]
    rows = seqs * DEC_SEQ
    n_seq = n_rows // DEC_SEQ
    stacked_spec = pl.BlockSpec((layer + 1, seqs, WINDOW, KV_WIDTH), lambda i: (0, i, 0, 0))
    prev_specs = [pl.BlockSpec((layer, seqs, WINDOW, KV_WIDTH), lambda i: (0, i, 0, 0))] * 2 if layer else []
    prev_args = (kc_prev, vc_prev) if layer else ()
    kern = functools.partial(_mixer_short_kernel, seqs=seqs, layer=layer)
    return pl.pallas_call(
        kern,
        grid=(n_rows // rows,),
        in_specs=[
            pl.BlockSpec(memory_space=pltpu.SMEM),
            pl.BlockSpec((rows, D_MODEL), lambda i: (i, 0)),
            _const_spec((rows, 2 * LANES)),
            pl.BlockSpec((None, seqs, WINDOW, KV_WIDTH), lambda i: (layer, i, 0, 0)),
            pl.BlockSpec((None, seqs, WINDOW, KV_WIDTH), lambda i: (layer, i, 0, 0)),
            pl.BlockSpec((seqs, HIST_ROWS, POOL_WIDTH), lambda i: (i, 0, 0)),
        ] + _mixer_weight_specs(layer) + prev_specs,
        out_specs=[
            pl.BlockSpec((rows, D_MODEL), lambda i: (i, 0)),
            stacked_spec,
            stacked_spec,
            pl.BlockSpec((rows, POOL_WIDTH), lambda i: (i, 0)),
        ],
        out_shape=[
            jax.ShapeDtypeStruct((n_rows, D_MODEL), F32),
            jax.ShapeDtypeStruct((layer + 1, n_seq, WINDOW, KV_WIDTH), F32),
            jax.ShapeDtypeStruct((layer + 1, n_seq, WINDOW, KV_WIDTH), F32),
            jax.ShapeDtypeStruct((n_rows, POOL_WIDTH), F32),
        ],
        scratch_shapes=[
            pltpu.VMEM((rows, ATTN_WIDTH), F32),
            pltpu.VMEM((rows, ATTN_WIDTH), F32),
            pltpu.VMEM((seqs, HIST_ROWS + DEC_SEQ, POOL_WIDTH), F32),
        ],
        compiler_params=pltpu.CompilerParams(
            dimension_semantics=("arbitrary",), vmem_limit_bytes=VMEM_LIMIT),
        name=name,
    )(sinks, h, rope, ck, cv, st, ln, win, pw, ps, wout, *prev_args)


def _mlp_kernel(h_ref, ln_ref, wup_ref, wdown_ref, lnf_ref, o_ref, *, final_norm):
    h = h_ref[...]
    xn = _rmsnorm(h, ln_ref[...]).astype(BF16)

    def hidden(c):
        a = jnp.maximum(jnp.dot(xn, wup_ref[:, c * MLP_CHUNK:(c + 1) * MLP_CHUNK],
                                preferred_element_type=F32), 0.0)
        return (a * a).astype(BF16)

    def down(a, c):
        return jnp.dot(a, wdown_ref[c * MLP_CHUNK:(c + 1) * MLP_CHUNK, :], preferred_element_type=F32)

    out = h
    a_prev = hidden(0)
    for c in range(1, D_FF // MLP_CHUNK):
        a_next = hidden(c)
        out = out + down(a_prev, c - 1)
        a_prev = a_next
    out = out + down(a_prev, D_FF // MLP_CHUNK - 1)
    if final_norm:
        out = _rmsnorm(out, lnf_ref[...])
    o_ref[...] = out


def _mlp(h, ln, wup, wdown, lnf, *, final_norm, layer, name):
    n_rows = h.shape[0]
    tile = min(MLP_TILE, n_rows)
    kern = functools.partial(_mlp_kernel, final_norm=final_norm)
    return pl.pallas_call(
        kern,
        grid=(n_rows // tile,),
        in_specs=[
            pl.BlockSpec((tile, D_MODEL), lambda i: (i, 0)),
            _layer_spec((1, D_MODEL), layer),
            _layer_spec((D_MODEL, D_FF), layer),
            _layer_spec((D_FF, D_MODEL), layer),
            _const_spec((1, D_MODEL)),
        ],
        out_specs=pl.BlockSpec((tile, D_MODEL), lambda i: (i, 0)),
        out_shape=jax.ShapeDtypeStruct((n_rows, D_MODEL), F32),
        compiler_params=pltpu.CompilerParams(
            dimension_semantics=("arbitrary",), vmem_limit_bytes=VMEM_LIMIT),
        name=name,
    )(h, ln, wup, wdown, lnf)


def _rope_table(pos):
    n = pos.shape[0]
    inv_freq = ROPE_THETA ** (-jnp.arange(0, ROT_DIM, 2, dtype=F32) / ROT_DIM)
    ang = pos.astype(F32)[:, None] * inv_freq[None, :]
    cos, sin = jnp.cos(ang), jnp.sin(ang)
    rest = HEAD_DIM - ROT_DIM
    c = jnp.concatenate([cos, cos, jnp.ones((n, rest), F32)], axis=-1)
    s = jnp.concatenate([-sin, sin, jnp.zeros((n, rest), F32)], axis=-1)
    reps = LANES // HEAD_DIM
    return jnp.concatenate([jnp.tile(c, (1, reps)), jnp.tile(s, (1, reps))], axis=-1)


def _pair_heads(w, axis):
    shape = w.shape
    split = shape[:axis] + (N_HEADS // Q_BLOCKS, Q_BLOCKS, HEAD_DIM) + shape[axis + 1:]
    return jnp.swapaxes(w.reshape(split), axis, axis + 1).reshape(shape)


def kernel(x_prompt, x_sample, cache_k, cache_v, state_pool, meta_tokens, ln1, w_in, attn_sinks,
           pool_w, pool_scale, w_out, ln2, w_up, w_down, ln_f):
    batch, seq, _ = x_prompt.shape
    dec_batch, dec_seq, _ = x_sample.shape
    depth = w_in.shape[0]
    assert dec_seq == DEC_SEQ and seq % PROMPT_TILE == 0 and dec_batch % SAMPLE_SEQS == 0

    pad = WINDOW - N_META
    rope_meta = _rope_table(jnp.arange(WINDOW) - pad)
    rope_prompt = _rope_table(N_META + jnp.arange(seq))
    rope_sample = jnp.tile(_rope_table(PAST_LEN + jnp.arange(DEC_SEQ)), (SAMPLE_SEQS, 1))

    win = jnp.concatenate([_pair_heads(w_in[:, :, :ATTN_WIDTH], 2), w_in[:, :, ATTN_WIDTH:]], axis=2).astype(BF16)
    wout = jnp.concatenate([_pair_heads(w_out[:, :ATTN_WIDTH], 1), w_out[:, ATTN_WIDTH:]], axis=1).astype(BF16)
    wup = w_up.astype(BF16)
    wdown = w_down.astype(BF16)
    pw = pool_w.astype(BF16)
    ps = pool_scale.reshape(depth, 1, POOL_WIDTH)
    l1 = ln1.reshape(depth, 1, D_MODEL)
    l2 = ln2.reshape(depth, 1, D_MODEL)
    lnf = ln_f.reshape(1, D_MODEL)
    sinks = attn_sinks.astype(F32)
    mixer_w = (sinks, l1, win, pw, ps, wout)

    hm = jnp.concatenate([jnp.zeros((pad, D_MODEL), F32), meta_tokens.astype(F32)], axis=0)[None]
    hp = x_prompt
    hs = x_sample.reshape(dec_batch * dec_seq, D_MODEL)
    zero_kv = jnp.zeros((1, WINDOW, KV_WIDTH), F32)
    zero_u = jnp.zeros((1, HIST_ROWS, POOL_WIDTH), F32)
    ck = cache_k.reshape(depth, dec_batch, WINDOW, KV_WIDTH)
    cv = cache_v.reshape(depth, dec_batch, WINDOW, KV_WIDTH)

    pk, pv, pu, su = [], [], [], []
    kc = vc = None
    for l in range(depth):
        last = l == depth - 1
        hm_mid, km, vm, um = _mixer_long(hm, rope_meta, zero_kv, zero_kv, zero_u, *mixer_w, tile=WINDOW,
                                         base_pos=-pad, layer=l, name=f"mixer_meta_{l}")
        hp_mid, kt, vt, ut = _mixer_long(hp, rope_prompt, km, vm, um, *mixer_w, tile=PROMPT_TILE,
                                         base_pos=N_META, layer=l, name=f"mixer_prompt_{l}")
        st = jnp.pad(state_pool[l], ((0, 0), (1, 0), (0, 0)))
        hs_mid, kc, vc, un = _mixer_short(hs, rope_sample, ck, cv, st, *mixer_w, kc, vc, seqs=SAMPLE_SEQS,
                                          layer=l, name=f"mixer_sample_{l}")

        mlp_w = (l2, wup, wdown, lnf)
        if not last:
            hm = _mlp(hm_mid[0], *mlp_w, final_norm=False, layer=l, name=f"mlp_meta_{l}")[None]
        hp = _mlp(hp_mid.reshape(batch * seq, D_MODEL), *mlp_w, final_norm=last, layer=l,
                  name=f"mlp_prompt_{l}").reshape(batch, seq, D_MODEL)
        hs = _mlp(hs_mid, *mlp_w, final_norm=last, layer=l, name=f"mlp_sample_{l}")

        pk.append(kt.reshape(batch, WINDOW, N_KV_HEADS, HEAD_DIM))
        pv.append(vt.reshape(batch, WINDOW, N_KV_HEADS, HEAD_DIM))
        pu.append(ut[:, 1:, :])
        un = un.reshape(dec_batch, dec_seq, POOL_WIDTH)
        su.append(jnp.concatenate([state_pool[l][:, dec_seq:], un], axis=1))

    y_sample = hs.reshape(dec_batch, dec_seq, D_MODEL)
    cache_shape = (depth, dec_batch, WINDOW, N_KV_HEADS, HEAD_DIM)
    return (hp, y_sample, jnp.stack(pk), jnp.stack(pv), jnp.stack(pu),
            kc.reshape(cache_shape), vc.reshape(cache_shape), jnp.stack(su))
```

```python
import functools

import jax
import jax.numpy as jnp
from jax import lax
from jax.experimental import pallas as pl
from jax.experimental.pallas import tpu as pltpu

D_MODEL = 1024
N_HEADS = 8
N_KV_HEADS = 2
HEAD_DIM = 64
ATTN_WIDTH = N_HEADS * HEAD_DIM
KV_WIDTH = N_KV_HEADS * HEAD_DIM
POOL_WINDOWS = (2, 4, 8, 16)
POOL_WIDTH = D_MODEL - ATTN_WIDTH
POOL_GROUP = POOL_WIDTH // len(POOL_WINDOWS)
POOL_HIST = max(POOL_WINDOWS) - 1
IN_WIDTH = ATTN_WIDTH + 2 * KV_WIDTH + POOL_WIDTH
WINDOW = 128
ROT_DIM = HEAD_DIM // 4
ROPE_THETA = 500000.0
D_FF = 4 * D_MODEL
N_META = 16
RMS_EPS = 1e-5
PAST_LEN = 16384
DEC_SEQ = 8

LANES = 128
HIST_ROWS = POOL_HIST + 1
Q_BLOCKS = ATTN_WIDTH // LANES
LOG2E = 1.4426950408889634
Q_SCALE = HEAD_DIM ** -0.5 * LOG2E
NEG_INF = float("-inf")

PROMPT_TILE = 512
SAMPLE_SEQS = 32
SAMPLE_UNROLL = 16
MLP_TILE = 1024
MLP_CHUNK = 1024
VMEM_LIMIT = 56 * 1024 * 1024

F32 = jnp.float32
BF16 = jnp.bfloat16


def _rmsnorm(x, g):
    r = lax.rsqrt(jnp.mean(x * x, axis=-1, keepdims=True) + RMS_EPS)
    return x * r * g


def _rope(x, rope):
    c = rope[:, 0:LANES]
    s = rope[:, LANES:2 * LANES]
    half = ROT_DIM // 2
    lane = lax.broadcasted_iota(jnp.int32, (1, LANES), 1)
    on_x1 = jnp.bitwise_and(lane, HEAD_DIM - 1) < half
    partner = jnp.where(on_x1, pltpu.roll(x, LANES - half, 1), pltpu.roll(x, half, 1))
    return x * c + partner * s


def _in_proj(h, ln_ref, win_ref):
    xn = _rmsnorm(h, ln_ref[...]).astype(BF16)
    return jnp.dot(xn, win_ref[...], preferred_element_type=F32)


def _split_rope(p, rope):
    qb = [_rope(p[:, j * LANES:(j + 1) * LANES], rope) * Q_SCALE for j in range(Q_BLOCKS)]
    k = _rope(p[:, ATTN_WIDTH:ATTN_WIDTH + KV_WIDTH], rope)
    v = p[:, ATTN_WIDTH + KV_WIDTH:ATTN_WIDTH + 2 * KV_WIDTH]
    u = p[:, ATTN_WIDTH + 2 * KV_WIDTH:]
    return qb, k, v, u


def _stack_heads(qrows, lo):
    zero = jnp.zeros_like(qrows[0])
    parts = [jnp.where(lo, q, zero) for q in qrows] + [jnp.where(lo, zero, q) for q in qrows]
    return jnp.concatenate(parts, axis=0).astype(BF16)


def _softmax_sink(s, sink):
    m = jnp.maximum(jnp.max(s, axis=-1, keepdims=True), sink)
    e = jnp.exp2(s - m)
    l = jnp.sum(e, axis=-1, keepdims=True) + jnp.exp2(sink - m)
    return e.astype(BF16), 1.0 / l


def _window_sum(x, w, axis):
    span = 1
    while span < w:
        x = x + pltpu.roll(x, span, axis)
        span *= 2
    return x


def _pool_project(d_groups, pw_ref, ps_ref):
    outs = [jnp.dot(d.astype(BF16), pw_ref[g], preferred_element_type=F32)
            for g, d in enumerate(d_groups)]
    return jnp.concatenate(outs, axis=-1) * ps_ref[...]


def _project_finish(p, rope_ref, kin_ref, vin_ref, uin_ref, pw_ref, ps_ref,
                    kt_ref, vt_ref, ut_ref, ubuf, dst, src, *, tile, first, p0):
    dq, dk, dv, dm, _ = dst
    _, sk, sv, _, _ = src
    qb, k, v, u = _split_rope(p, rope_ref[...])
    lo = lax.broadcasted_iota(jnp.int32, (1, LANES), 1) < HEAD_DIM
    for r in range(tile // WINDOW):
        rows = slice(r * WINDOW, (r + 1) * WINDOW)
        dq[r] = _stack_heads([qb[j][rows] for j in range(Q_BLOCKS)], lo)
    dk[0:WINDOW, :] = jnp.where(first, kin_ref[0].astype(BF16), sk[tile:tile + WINDOW, :])
    dv[0:WINDOW, :] = jnp.where(first, vin_ref[0].astype(BF16), sv[tile:tile + WINDOW, :])
    dk[WINDOW:WINDOW + tile, :] = k.astype(BF16)
    dv[WINDOW:WINDOW + tile, :] = v.astype(BF16)
    kt_ref[0] = k[tile - WINDOW:, :]
    vt_ref[0] = v[tile - WINDOW:, :]
    ut_ref[0] = u[tile - HIST_ROWS:, :]

    ubuf[0:HIST_ROWS, :] = jnp.where(first, uin_ref[0], ubuf[tile:tile + HIST_ROWS, :])
    ubuf[HIST_ROWS:HIST_ROWS + tile, :] = u
    pos = p0 + lax.broadcasted_iota(jnp.int32, (tile, 1), 0)
    d_groups = []
    for g, w in enumerate(POOL_WINDOWS):
        cols = slice(g * POOL_GROUP, (g + 1) * POOL_GROUP)
        wsum = _window_sum(ubuf[:, cols], w, 0)[HIST_ROWS:]
        cnt = jnp.clip(pos + 1, 1, w).astype(F32)
        d_groups.append(wsum / cnt - u[:, cols])
    dm[...] = _pool_project(d_groups, pw_ref, ps_ref).astype(BF16)


def _attend_scores(src, *, tile):
    sq, sk, _, _, _ = src
    return [lax.dot_general(sq[r], sk[r * WINDOW:(r + 2) * WINDOW, :], (((1,), (1,)), ((), ())),
                            preferred_element_type=F32) for r in range(tile // WINDOW)]


def _attend_values(scores, sink_ref, src, *, layer, p0):
    _, _, sv, _, _ = src
    lo = lax.broadcasted_iota(jnp.int32, (1, LANES), 1) < HEAD_DIM
    qi = lax.broadcasted_iota(jnp.int32, (WINDOW, 2 * WINDOW), 0)
    kc = lax.broadcasted_iota(jnp.int32, (WINDOW, 2 * WINDOW), 1)
    band = (kc >= qi) & (kc <= qi + WINDOW)
    attn_rows = []
    for r, s in enumerate(scores):
        vb = sv[r * WINDOW:(r + 2) * WINDOW, :]
        kpos = p0 + (r - 1) * WINDOW + kc
        bias = jnp.where(band & (kpos >= 0), 0.0, NEG_INF)
        probs, inv = zip(*[_softmax_sink(s[hh * WINDOW:(hh + 1) * WINDOW] + bias, sink_ref[layer, hh] * LOG2E)
                           for hh in range(N_HEADS)])
        o = jnp.dot(jnp.concatenate(probs, axis=0), vb, preferred_element_type=F32)
        heads = [o[hh * WINDOW:(hh + 1) * WINDOW] * inv[hh] for hh in range(N_HEADS)]
        attn_rows.append(jnp.concatenate(
            [jnp.where(lo, heads[j], heads[j + Q_BLOCKS]) for j in range(Q_BLOCKS)], axis=-1))
    return jnp.concatenate(attn_rows, axis=0).astype(BF16)


def _mixer_long_kernel(sink_ref, hp_ref, hr_ref, rope_ref, kin_ref, vin_ref, uin_ref, ln_ref, win_ref,
                       pw_ref, ps_ref, wout_ref, ho_ref, kt_ref, vt_ref, ut_ref,
                       qa, ka, va, ma, xa, qb, kb, vb, mb, xb, ubuf,
                       *, tile, base_pos, layer, tiles_per_seq, n_tiles):
    i = pl.program_id(0)
    tp = lax.rem(jnp.minimum(i, n_tiles - 1), tiles_per_seq)
    ts = lax.rem(jnp.clip(i - 1, 0, n_tiles - 1), tiles_per_seq)
    set_a, set_b = (qa, ka, va, ma, xa), (qb, kb, vb, mb, xb)

    @pl.when(i == 0)
    def _():
        for ref in (kb, vb, ubuf):
            ref[...] = jnp.zeros(ref.shape, ref.dtype)

    def step(dst, src, project, attend, output):
        if attend:
            scores = _attend_scores(src, tile=tile)
        if project:
            p = _in_proj(hp_ref[0], ln_ref, win_ref)
        if output:
            ho_ref[0] = hr_ref[0] + jnp.dot(dst[4][...], wout_ref[...], preferred_element_type=F32)
        if attend:
            attn = _attend_values(scores, sink_ref, src, layer=layer, p0=base_pos + ts * tile)
            src[4][...] = jnp.concatenate([attn, src[3][...]], axis=-1)
        if project:
            _project_finish(p, rope_ref, kin_ref, vin_ref, uin_ref, pw_ref, ps_ref, kt_ref, vt_ref, ut_ref,
                            ubuf, dst, src, tile=tile, first=tp == 0, p0=base_pos + tp * tile)

    variants = {}
    for idx in range(n_tiles + 2):
        key = (idx < n_tiles, 1 <= idx <= n_tiles, idx >= 2, idx % 2)
        variants.setdefault(key, []).append(idx)
    parity = lax.rem(i, 2)
    for (project, attend, output, par), steps in variants.items():
        dst, src = (set_a, set_b) if par == 0 else (set_b, set_a)
        cond = (i >= steps[0]) & (i <= steps[-1]) & (parity == par)
        pl.when(cond)(functools.partial(step, dst, src, project, attend, output))


def _const_spec(shape):
    nd = len(shape)
    return pl.BlockSpec(shape, lambda *_: (0,) * nd, pipeline_mode=pl.Buffered(1))


def _layer_spec(shape, layer):
    nd = len(shape)
    return pl.BlockSpec((None,) + shape, lambda *_: (layer,) + (0,) * nd, pipeline_mode=pl.Buffered(1))


def _mixer_weight_specs(layer):
    return [
        _layer_spec((1, D_MODEL), layer),
        _layer_spec((D_MODEL, IN_WIDTH), layer),
        _layer_spec((len(POOL_WINDOWS), POOL_GROUP, POOL_GROUP), layer),
        _layer_spec((1, POOL_WIDTH), layer),
        _layer_spec((D_MODEL, D_MODEL), layer),
    ]


def _mixer_long(h, rope, kin, vin, uin, sinks, ln, win, pw, ps, wout, *, tile, base_pos, layer, name):
    n_seq, seq_len, _ = h.shape
    tps = seq_len // tile
    n_tiles = n_seq * tps
    shared_init = kin.shape[0] == 1

    def proj_tile(i):
        return jnp.minimum(i, n_tiles - 1)

    def out_tile(i):
        return jnp.maximum(i - 2, 0)

    def init_idx(i):
        return (0 if shared_init else proj_tile(i) // tps, 0, 0)

    kern = functools.partial(_mixer_long_kernel, tile=tile, base_pos=base_pos, layer=layer,
                             tiles_per_seq=tps, n_tiles=n_tiles)
    buffer_set = [
        pltpu.VMEM((tile // WINDOW, N_HEADS * WINDOW, LANES), BF16),
        pltpu.VMEM((WINDOW + tile, KV_WIDTH), BF16),
        pltpu.VMEM((WINDOW + tile, KV_WIDTH), BF16),
        pltpu.VMEM((tile, POOL_WIDTH), BF16),
        pltpu.VMEM((tile, D_MODEL), BF16),
    ]
    return pl.pallas_call(
        kern,
        grid=(n_tiles + 2,),
        in_specs=[
            pl.BlockSpec(memory_space=pltpu.SMEM),
            pl.BlockSpec((1, tile, D_MODEL), lambda i: (proj_tile(i) // tps, proj_tile(i) % tps, 0)),
            pl.BlockSpec((1, tile, D_MODEL), lambda i: (out_tile(i) // tps, out_tile(i) % tps, 0)),
            pl.BlockSpec((tile, 2 * LANES), lambda i: (proj_tile(i) % tps, 0)),
            pl.BlockSpec((1, WINDOW, KV_WIDTH), init_idx),
            pl.BlockSpec((1, WINDOW, KV_WIDTH), init_idx),
            pl.BlockSpec((1, HIST_ROWS, POOL_WIDTH), init_idx),
        ] + _mixer_weight_specs(layer),
        out_specs=[
            pl.BlockSpec((1, tile, D_MODEL), lambda i: (out_tile(i) // tps, out_tile(i) % tps, 0)),
            pl.BlockSpec((1, WINDOW, KV_WIDTH), lambda i: (proj_tile(i) // tps, 0, 0)),
            pl.BlockSpec((1, WINDOW, KV_WIDTH), lambda i: (proj_tile(i) // tps, 0, 0)),
            pl.BlockSpec((1, HIST_ROWS, POOL_WIDTH), lambda i: (proj_tile(i) // tps, 0, 0)),
        ],
        out_shape=[
            jax.ShapeDtypeStruct((n_seq, seq_len, D_MODEL), F32),
            jax.ShapeDtypeStruct((n_seq, WINDOW, KV_WIDTH), F32),
            jax.ShapeDtypeStruct((n_seq, WINDOW, KV_WIDTH), F32),
            jax.ShapeDtypeStruct((n_seq, HIST_ROWS, POOL_WIDTH), F32),
        ],
        scratch_shapes=buffer_set + buffer_set + [pltpu.VMEM((HIST_ROWS + tile, POOL_WIDTH), F32)],
        compiler_params=pltpu.CompilerParams(
            dimension_semantics=("arbitrary",), vmem_limit_bytes=VMEM_LIMIT),
        name=name,
    )(sinks, h, h, rope, kin, vin, uin, ln, win, pw, ps, wout)


def _mixer_short_kernel(sink_ref, h_ref, rope_ref, ck_ref, cv_ref, st_ref, ln_ref, win_ref,
                        pw_ref, ps_ref, wout_ref, *rest, seqs, layer):
    if layer:
        kprev_ref, vprev_ref, sprev_ref = rest[:3]
        rest = rest[3:]
    ho_ref, kc_ref, vc_ref, sn_ref, qbuf, abuf, ubuf, dbuf = rest
    if layer:
        kc_ref[0:layer] = kprev_ref[...]
        vc_ref[0:layer] = vprev_ref[...]
        sn_ref[0:layer] = sprev_ref[...]
    rows = seqs * DEC_SEQ
    h = h_ref[...]
    qb, k, v, u = _split_rope(_in_proj(h, ln_ref, win_ref), rope_ref[...])
    for j in range(Q_BLOCKS):
        qbuf[:, j * LANES:(j + 1) * LANES] = qb[j]
    kept = WINDOW - DEC_SEQ
    kc_ref[layer, :, 0:kept, :] = ck_ref[:, DEC_SEQ:, :]
    vc_ref[layer, :, 0:kept, :] = cv_ref[:, DEC_SEQ:, :]
    kc_ref[layer, :, kept:, :] = k.reshape(seqs, DEC_SEQ, KV_WIDTH)
    vc_ref[layer, :, kept:, :] = v.reshape(seqs, DEC_SEQ, KV_WIDTH)

    groups = len(POOL_WINDOWS)
    for g in range(groups):
        ubuf[g] = u[:, g * POOL_GROUP:(g + 1) * POOL_GROUP]
    new_slabs = [[ubuf[g, pl.ds(t, seqs, stride=DEC_SEQ), :] for g in range(groups)] for t in range(DEC_SEQ)]
    for r in range(POOL_HIST):
        src_row = r + DEC_SEQ
        sn_ref[layer, r] = (st_ref[src_row] if src_row < POOL_HIST
                            else jnp.concatenate(new_slabs[src_row - POOL_HIST], axis=-1))
    for g, w in enumerate(POOL_WINDOWS):
        cols = slice(g * POOL_GROUP, (g + 1) * POOL_GROUP)
        slabs = [st_ref[r, :, cols] for r in range(POOL_HIST)] + [new_slabs[t][g] for t in range(DEC_SEQ)]
        cnt = float(min(PAST_LEN + 1, w))
        for t in range(DEC_SEQ):
            wsum = slabs[POOL_HIST + t]
            for back in range(1, w):
                wsum = wsum + slabs[POOL_HIST + t - back]
            dbuf[g, pl.ds(t, seqs, stride=DEC_SEQ), :] = wsum / cnt - slabs[POOL_HIST + t]

    stacked = N_HEADS * DEC_SEQ
    keys = WINDOW + 2 * DEC_SEQ
    lo = lax.broadcasted_iota(jnp.int32, (1, LANES), 1) < HEAD_DIM
    srow = lax.broadcasted_iota(jnp.int32, (stacked, keys), 0)
    kc = lax.broadcasted_iota(jnp.int32, (stacked, keys), 1)
    qi = jnp.bitwise_and(srow, DEC_SEQ - 1)
    bias = jnp.where((kc >= qi) & (kc <= qi + WINDOW), 0.0, NEG_INF)
    hrow = jnp.right_shift(lax.broadcasted_iota(jnp.int32, (stacked, 1), 0), DEC_SEQ.bit_length() - 1)
    sink = jnp.zeros((stacked, 1), F32)
    for hh in range(N_HEADS):
        sink = jnp.where(hrow == hh, sink_ref[layer, hh] * LOG2E, sink)
    zpad = jnp.zeros((DEC_SEQ, KV_WIDTH), F32)

    def group_body(gi, carry):
        ids = [gi * SAMPLE_UNROLL + n for n in range(SAMPLE_UNROLL)]
        starts = [pl.multiple_of(i * DEC_SEQ, DEC_SEQ) for i in ids]
        scores = []
        for i, r0 in zip(ids, starts):
            qrows = qbuf[pl.ds(r0, DEC_SEQ), :]
            q = _stack_heads([qrows[:, j * LANES:(j + 1) * LANES] for j in range(Q_BLOCKS)], lo)
            kall = jnp.concatenate([ck_ref[i], kc_ref[layer, i, kept:, :], zpad], axis=0).astype(BF16)
            scores.append(lax.dot_general(q, kall, (((1,), (1,)), ((), ())), preferred_element_type=F32))
        probs = [_softmax_sink(s + bias, sink) for s in scores]
        for i, r0, (prob, inv) in zip(ids, starts, probs):
            vall = jnp.concatenate([cv_ref[i], vc_ref[layer, i, kept:, :], zpad], axis=0).astype(BF16)
            o = jnp.dot(prob, vall, preferred_element_type=F32) * inv
            abuf[pl.ds(r0, DEC_SEQ), :] = jnp.concatenate(
                [jnp.where(lo, o[j * DEC_SEQ:(j + 1) * DEC_SEQ],
                           o[(j + Q_BLOCKS) * DEC_SEQ:(j + Q_BLOCKS + 1) * DEC_SEQ])
                 for j in range(Q_BLOCKS)], axis=-1)
        return carry

    lax.fori_loop(0, seqs // SAMPLE_UNROLL, group_body, 0)

    pm = _pool_project([dbuf[g] for g in range(groups)], pw_ref, ps_ref)

    mix = jnp.concatenate([abuf[...], pm], axis=-1).astype(BF16)
    ho_ref[...] = h + jnp.dot(mix, wout_ref[...], preferred_element_type=F32)


def _mixer_short(h, rope, ck, cv, st, sinks, ln, win, pw, ps, wout, kc_prev, vc_prev, sn_prev,
                 *, seqs, layer, name):
    n_rows = h.shape[0]
    rows = seqs * DEC_SEQ
    n_seq = n_rows // DEC_SEQ
    stacked_spec = pl.BlockSpec((layer + 1, seqs, WINDOW, KV_WIDTH), lambda i: (0, i, 0, 0))
    state_spec = pl.BlockSpec((layer + 1, POOL_HIST, seqs, POOL_WIDTH), lambda i: (0, 0, i, 0))
    prev_specs = [pl.BlockSpec((layer, seqs, WINDOW, KV_WIDTH), lambda i: (0, i, 0, 0))] * 2 + [
        pl.BlockSpec((layer, POOL_HIST, seqs, POOL_WIDTH), lambda i: (0, 0, i, 0))] if layer else []
    prev_args = (kc_prev, vc_prev, sn_prev) if layer else ()
    kern = functools.partial(_mixer_short_kernel, seqs=seqs, layer=layer)
    return pl.pallas_call(
        kern,
        grid=(n_rows // rows,),
        in_specs=[
            pl.BlockSpec(memory_space=pltpu.SMEM),
            pl.BlockSpec((rows, D_MODEL), lambda i: (i, 0)),
            _const_spec((rows, 2 * LANES)),
            pl.BlockSpec((None, seqs, WINDOW, KV_WIDTH), lambda i: (layer, i, 0, 0)),
            pl.BlockSpec((None, seqs, WINDOW, KV_WIDTH), lambda i: (layer, i, 0, 0)),
            pl.BlockSpec((None, POOL_HIST, seqs, POOL_WIDTH), lambda i: (layer, 0, i, 0)),
        ] + _mixer_weight_specs(layer) + prev_specs,
        out_specs=[
            pl.BlockSpec((rows, D_MODEL), lambda i: (i, 0)),
            stacked_spec,
            stacked_spec,
            state_spec,
        ],
        out_shape=[
            jax.ShapeDtypeStruct((n_rows, D_MODEL), F32),
            jax.ShapeDtypeStruct((layer + 1, n_seq, WINDOW, KV_WIDTH), F32),
            jax.ShapeDtypeStruct((layer + 1, n_seq, WINDOW, KV_WIDTH), F32),
            jax.ShapeDtypeStruct((layer + 1, POOL_HIST, n_seq, POOL_WIDTH), F32),
        ],
        scratch_shapes=[
            pltpu.VMEM((rows, ATTN_WIDTH), F32),
            pltpu.VMEM((rows, ATTN_WIDTH), F32),
            pltpu.VMEM((len(POOL_WINDOWS), rows, POOL_GROUP), F32),
            pltpu.VMEM((len(POOL_WINDOWS), rows, POOL_GROUP), F32),
        ],
        compiler_params=pltpu.CompilerParams(
            dimension_semantics=("arbitrary",), vmem_limit_bytes=VMEM_LIMIT),
        name=name,
    )(sinks, h, rope, ck, cv, st, ln, win, pw, ps, wout, *prev_args)


def _mlp_kernel(h_ref, ln_ref, wup_ref, wdown_ref, lnf_ref, o_ref, *, final_norm):
    h = h_ref[...]
    xn = _rmsnorm(h, ln_ref[...]).astype(BF16)

    def hidden(c):
        a = jnp.maximum(jnp.dot(xn, wup_ref[:, c * MLP_CHUNK:(c + 1) * MLP_CHUNK],
                                preferred_element_type=F32), 0.0)
        return (a * a).astype(BF16)

    def down(a, c):
        return jnp.dot(a, wdown_ref[c * MLP_CHUNK:(c + 1) * MLP_CHUNK, :], preferred_element_type=F32)

    out = h
    a_prev = hidden(0)
    for c in range(1, D_FF // MLP_CHUNK):
        a_next = hidden(c)
        out = out + down(a_prev, c - 1)
        a_prev = a_next
    out = out + down(a_prev, D_FF // MLP_CHUNK - 1)
    if final_norm:
        out = _rmsnorm(out, lnf_ref[...])
    o_ref[...] = out


def _mlp(h, ln, wup, wdown, lnf, *, final_norm, layer, name):
    n_rows = h.shape[0]
    tile = min(MLP_TILE, n_rows)
    kern = functools.partial(_mlp_kernel, final_norm=final_norm)
    return pl.pallas_call(
        kern,
        grid=(n_rows // tile,),
        in_specs=[
            pl.BlockSpec((tile, D_MODEL), lambda i: (i, 0)),
            _layer_spec((1, D_MODEL), layer),
            _layer_spec((D_MODEL, D_FF), layer),
            _layer_spec((D_FF, D_MODEL), layer),
            _const_spec((1, D_MODEL)),
        ],
        out_specs=pl.BlockSpec((tile, D_MODEL), lambda i: (i, 0)),
        out_shape=jax.ShapeDtypeStruct((n_rows, D_MODEL), F32),
        compiler_params=pltpu.CompilerParams(
            dimension_semantics=("arbitrary",), vmem_limit_bytes=VMEM_LIMIT),
        name=name,
    )(h, ln, wup, wdown, lnf)


def _rope_table(pos):
    n = pos.shape[0]
    inv_freq = ROPE_THETA ** (-jnp.arange(0, ROT_DIM, 2, dtype=F32) / ROT_DIM)
    ang = pos.astype(F32)[:, None] * inv_freq[None, :]
    cos, sin = jnp.cos(ang), jnp.sin(ang)
    rest = HEAD_DIM - ROT_DIM
    c = jnp.concatenate([cos, cos, jnp.ones((n, rest), F32)], axis=-1)
    s = jnp.concatenate([-sin, sin, jnp.zeros((n, rest), F32)], axis=-1)
    reps = LANES // HEAD_DIM
    return jnp.concatenate([jnp.tile(c, (1, reps)), jnp.tile(s, (1, reps))], axis=-1)


def _pair_heads(w, axis):
    shape = w.shape
    split = shape[:axis] + (N_HEADS // Q_BLOCKS, Q_BLOCKS, HEAD_DIM) + shape[axis + 1:]
    return jnp.swapaxes(w.reshape(split), axis, axis + 1).reshape(shape)


def kernel(x_prompt, x_sample, cache_k, cache_v, state_pool, meta_tokens, ln1, w_in, attn_sinks,
           pool_w, pool_scale, w_out, ln2, w_up, w_down, ln_f):
    batch, seq, _ = x_prompt.shape
    dec_batch, dec_seq, _ = x_sample.shape
    depth = w_in.shape[0]
    assert dec_seq == DEC_SEQ and seq % PROMPT_TILE == 0 and dec_batch % SAMPLE_SEQS == 0

    pad = WINDOW - N_META
    rope_meta = _rope_table(jnp.arange(WINDOW) - pad)
    rope_prompt = _rope_table(N_META + jnp.arange(seq))
    rope_sample = jnp.tile(_rope_table(PAST_LEN + jnp.arange(DEC_SEQ)), (SAMPLE_SEQS, 1))

    win = jnp.concatenate([_pair_heads(w_in[:, :, :ATTN_WIDTH], 2), w_in[:, :, ATTN_WIDTH:]], axis=2).astype(BF16)
    wout = jnp.concatenate([_pair_heads(w_out[:, :ATTN_WIDTH], 1), w_out[:, ATTN_WIDTH:]], axis=1).astype(BF16)
    wup = w_up.astype(BF16)
    wdown = w_down.astype(BF16)
    pw = pool_w.astype(BF16)
    ps = pool_scale.reshape(depth, 1, POOL_WIDTH)
    l1 = ln1.reshape(depth, 1, D_MODEL)
    l2 = ln2.reshape(depth, 1, D_MODEL)
    lnf = ln_f.reshape(1, D_MODEL)
    sinks = attn_sinks.astype(F32)
    mixer_w = (sinks, l1, win, pw, ps, wout)

    hm = jnp.concatenate([jnp.zeros((pad, D_MODEL), F32), meta_tokens.astype(F32)], axis=0)[None]
    hp = x_prompt
    hs = x_sample.reshape(dec_batch * dec_seq, D_MODEL)
    zero_kv = jnp.zeros((1, WINDOW, KV_WIDTH), F32)
    zero_u = jnp.zeros((1, HIST_ROWS, POOL_WIDTH), F32)
    st = jnp.transpose(state_pool, (0, 2, 1, 3))
    ck = cache_k.reshape(depth, dec_batch, WINDOW, KV_WIDTH)
    cv = cache_v.reshape(depth, dec_batch, WINDOW, KV_WIDTH)

    pk, pv, pu = [], [], []
    kc = vc = sn = None
    for l in range(depth):
        last = l == depth - 1
        hm_mid, km, vm, um = _mixer_long(hm, rope_meta, zero_kv, zero_kv, zero_u, *mixer_w, tile=WINDOW,
                                         base_pos=-pad, layer=l, name=f"mixer_meta_{l}")
        hp_mid, kt, vt, ut = _mixer_long(hp, rope_prompt, km, vm, um, *mixer_w, tile=PROMPT_TILE,
                                         base_pos=N_META, layer=l, name=f"mixer_prompt_{l}")
        hs_mid, kc, vc, sn = _mixer_short(hs, rope_sample, ck, cv, st, *mixer_w, kc, vc, sn, seqs=SAMPLE_SEQS,
                                          layer=l, name=f"mixer_sample_{l}")

        mlp_w = (l2, wup, wdown, lnf)
        if not last:
            hm = _mlp(hm_mid[0], *mlp_w, final_norm=False, layer=l, name=f"mlp_meta_{l}")[None]
        hp = _mlp(hp_mid.reshape(batch * seq, D_MODEL), *mlp_w, final_norm=last, layer=l,
                  name=f"mlp_prompt_{l}").reshape(batch, seq, D_MODEL)
        hs = _mlp(hs_mid, *mlp_w, final_norm=last, layer=l, name=f"mlp_sample_{l}")

        pk.append(kt.reshape(batch, WINDOW, N_KV_HEADS, HEAD_DIM))
        pv.append(vt.reshape(batch, WINDOW, N_KV_HEADS, HEAD_DIM))
        pu.append(ut[:, 1:, :])

    y_sample = hs.reshape(dec_batch, dec_seq, D_MODEL)
    cache_shape = (depth, dec_batch, WINDOW, N_KV_HEADS, HEAD_DIM)
    return (hp, y_sample, jnp.stack(pk), jnp.stack(pv), jnp.stack(pu),
            kc.reshape(cache_shape), vc.reshape(cache_shape), jnp.transpose(sn, (0, 2, 1, 3)))
```

```python
import functools

import jax
import jax.numpy as jnp
from jax import lax
from jax.experimental import pallas as pl
from jax.experimental.pallas import tpu as pltpu

D_MODEL = 1024
N_HEADS = 8
N_KV_HEADS = 2
HEAD_DIM = 64
ATTN_WIDTH = N_HEADS * HEAD_DIM
KV_WIDTH = N_KV_HEADS * HEAD_DIM
POOL_WINDOWS = (2, 4, 8, 16)
POOL_WIDTH = D_MODEL - ATTN_WIDTH
POOL_GROUP = POOL_WIDTH // len(POOL_WINDOWS)
POOL_HIST = max(POOL_WINDOWS) - 1
IN_WIDTH = ATTN_WIDTH + 2 * KV_WIDTH + POOL_WIDTH
WINDOW = 128
ROT_DIM = HEAD_DIM // 4
ROPE_THETA = 500000.0
D_FF = 4 * D_MODEL
N_META = 16
RMS_EPS = 1e-5
PAST_LEN = 16384
DEC_SEQ = 8

LANES = 128
HIST_ROWS = POOL_HIST + 1
Q_BLOCKS = ATTN_WIDTH // LANES
LOG2E = 1.4426950408889634
Q_SCALE = HEAD_DIM ** -0.5 * LOG2E
NEG_INF = float("-inf")

PROMPT_TILE = 512
SAMPLE_SEQS = 32
SAMPLE_UNROLL = 16
MLP_TILE = 1024
MLP_SMALL_TILE = 256
MLP_CHUNK = 1024
VMEM_LIMIT = 56 * 1024 * 1024

F32 = jnp.float32
BF16 = jnp.bfloat16


def _rmsnorm(x, g):
    r = lax.rsqrt(jnp.mean(x * x, axis=-1, keepdims=True) + RMS_EPS)
    return x * r * g


def _rope(x, rope):
    c = rope[:, 0:LANES]
    s = rope[:, LANES:2 * LANES]
    half = ROT_DIM // 2
    lane = lax.broadcasted_iota(jnp.int32, (1, LANES), 1)
    on_x1 = jnp.bitwise_and(lane, HEAD_DIM - 1) < half
    partner = jnp.where(on_x1, pltpu.roll(x, LANES - half, 1), pltpu.roll(x, half, 1))
    return x * c + partner * s


def _in_proj(h, ln_ref, win_ref):
    xn = _rmsnorm(h, ln_ref[...]).astype(BF16)
    return jnp.dot(xn, win_ref[...], preferred_element_type=F32)


def _split_rope(p, rope):
    qb = [_rope(p[:, j * LANES:(j + 1) * LANES], rope) * Q_SCALE for j in range(Q_BLOCKS)]
    k = _rope(p[:, ATTN_WIDTH:ATTN_WIDTH + KV_WIDTH], rope)
    v = p[:, ATTN_WIDTH + KV_WIDTH:ATTN_WIDTH + 2 * KV_WIDTH]
    u = p[:, ATTN_WIDTH + 2 * KV_WIDTH:]
    return qb, k, v, u


def _stack_heads(qrows, lo):
    zero = jnp.zeros_like(qrows[0])
    parts = [jnp.where(lo, q, zero) for q in qrows] + [jnp.where(lo, zero, q) for q in qrows]
    return jnp.concatenate(parts, axis=0).astype(BF16)


def _softmax_sink(s, sink):
    m = jnp.maximum(jnp.max(s, axis=-1, keepdims=True), sink)
    e = jnp.exp2(s - m)
    l = jnp.sum(e, axis=-1, keepdims=True) + jnp.exp2(sink - m)
    return e.astype(BF16), 1.0 / l


def _window_sum(x, w, axis):
    span = 1
    while span < w:
        x = x + pltpu.roll(x, span, axis)
        span *= 2
    return x


def _pool_project(d_groups, pw_ref, ps_ref):
    outs = [jnp.dot(d.astype(BF16), pw_ref[g], preferred_element_type=F32)
            for g, d in enumerate(d_groups)]
    return jnp.concatenate(outs, axis=-1) * ps_ref[...]


def _project_finish(p, rope_ref, kin_ref, vin_ref, uin_ref, pw_ref, ps_ref,
                    kt_ref, vt_ref, ut_ref, ubuf, dst, src, *, tile, first, p0):
    dq, dk, dv, dm, _ = dst
    _, sk, sv, _, _ = src
    qb, k, v, u = _split_rope(p, rope_ref[...])
    lo = lax.broadcasted_iota(jnp.int32, (1, LANES), 1) < HEAD_DIM
    for r in range(tile // WINDOW):
        rows = slice(r * WINDOW, (r + 1) * WINDOW)
        dq[r] = _stack_heads([qb[j][rows] for j in range(Q_BLOCKS)], lo)
    dk[0:WINDOW, :] = jnp.where(first, kin_ref[0].astype(BF16), sk[tile:tile + WINDOW, :])
    dv[0:WINDOW, :] = jnp.where(first, vin_ref[0].astype(BF16), sv[tile:tile + WINDOW, :])
    dk[WINDOW:WINDOW + tile, :] = k.astype(BF16)
    dv[WINDOW:WINDOW + tile, :] = v.astype(BF16)
    kt_ref[0] = k[tile - WINDOW:, :]
    vt_ref[0] = v[tile - WINDOW:, :]
    ut_ref[0] = u[tile - HIST_ROWS:, :]

    ubuf[0:HIST_ROWS, :] = jnp.where(first, uin_ref[0], ubuf[tile:tile + HIST_ROWS, :])
    ubuf[HIST_ROWS:HIST_ROWS + tile, :] = u
    pos = p0 + lax.broadcasted_iota(jnp.int32, (tile, 1), 0)
    d_groups = []
    for g, w in enumerate(POOL_WINDOWS):
        cols = slice(g * POOL_GROUP, (g + 1) * POOL_GROUP)
        wsum = _window_sum(ubuf[:, cols], w, 0)[HIST_ROWS:]
        cnt = jnp.clip(pos + 1, 1, w).astype(F32)
        d_groups.append(wsum / cnt - u[:, cols])
    dm[...] = _pool_project(d_groups, pw_ref, ps_ref).astype(BF16)


def _attend_scores(src, *, tile):
    sq, sk, _, _, _ = src
    return [lax.dot_general(sq[r], sk[r * WINDOW:(r + 2) * WINDOW, :], (((1,), (1,)), ((), ())),
                            preferred_element_type=F32) for r in range(tile // WINDOW)]


def _attend_values(scores, sink_ref, src, *, layer, p0):
    _, _, sv, _, _ = src
    lo = lax.broadcasted_iota(jnp.int32, (1, LANES), 1) < HEAD_DIM
    qi = lax.broadcasted_iota(jnp.int32, (WINDOW, 2 * WINDOW), 0)
    kc = lax.broadcasted_iota(jnp.int32, (WINDOW, 2 * WINDOW), 1)
    band = (kc >= qi) & (kc <= qi + WINDOW)
    attn_rows = []
    for r, s in enumerate(scores):
        vb = sv[r * WINDOW:(r + 2) * WINDOW, :]
        kpos = p0 + (r - 1) * WINDOW + kc
        bias = jnp.where(band & (kpos >= 0), 0.0, NEG_INF)
        probs, inv = zip(*[_softmax_sink(s[hh * WINDOW:(hh + 1) * WINDOW] + bias, sink_ref[layer, hh] * LOG2E)
                           for hh in range(N_HEADS)])
        o = jnp.dot(jnp.concatenate(probs, axis=0), vb, preferred_element_type=F32)
        heads = [o[hh * WINDOW:(hh + 1) * WINDOW] * inv[hh] for hh in range(N_HEADS)]
        attn_rows.append(jnp.concatenate(
            [jnp.where(lo, heads[j], heads[j + Q_BLOCKS]) for j in range(Q_BLOCKS)], axis=-1))
    return jnp.concatenate(attn_rows, axis=0).astype(BF16)


def _mixer_long_kernel(sink_ref, hp_ref, hr_ref, rope_ref, kin_ref, vin_ref, uin_ref, ln_ref, win_ref,
                       pw_ref, ps_ref, wout_ref, ho_ref, kt_ref, vt_ref, ut_ref,
                       qa, ka, va, ma, xa, qb, kb, vb, mb, xb, ubuf,
                       *, tile, base_pos, layer, tiles_per_seq, n_tiles):
    i = pl.program_id(0)
    tp = lax.rem(jnp.minimum(i, n_tiles - 1), tiles_per_seq)
    ts = lax.rem(jnp.clip(i - 1, 0, n_tiles - 1), tiles_per_seq)
    set_a, set_b = (qa, ka, va, ma, xa), (qb, kb, vb, mb, xb)

    @pl.when(i == 0)
    def _():
        for ref in (kb, vb, ubuf):
            ref[...] = jnp.zeros(ref.shape, ref.dtype)

    def step(dst, src, project, attend, output):
        if attend:
            scores = _attend_scores(src, tile=tile)
        if project:
            p = _in_proj(hp_ref[0], ln_ref, win_ref)
        if output:
            ho_ref[0] = hr_ref[0] + jnp.dot(dst[4][...], wout_ref[...], preferred_element_type=F32)
        if attend:
            attn = _attend_values(scores, sink_ref, src, layer=layer, p0=base_pos + ts * tile)
            src[4][...] = jnp.concatenate([attn, src[3][...]], axis=-1)
        if project:
            _project_finish(p, rope_ref, kin_ref, vin_ref, uin_ref, pw_ref, ps_ref, kt_ref, vt_ref, ut_ref,
                            ubuf, dst, src, tile=tile, first=tp == 0, p0=base_pos + tp * tile)

    variants = {}
    for idx in range(n_tiles + 2):
        key = (idx < n_tiles, 1 <= idx <= n_tiles, idx >= 2, idx % 2)
        variants.setdefault(key, []).append(idx)
    parity = lax.rem(i, 2)
    for (project, attend, output, par), steps in variants.items():
        dst, src = (set_a, set_b) if par == 0 else (set_b, set_a)
        cond = (i >= steps[0]) & (i <= steps[-1]) & (parity == par)
        pl.when(cond)(functools.partial(step, dst, src, project, attend, output))


def _const_spec(shape):
    nd = len(shape)
    return pl.BlockSpec(shape, lambda *_: (0,) * nd, pipeline_mode=pl.Buffered(1))


def _layer_spec(shape, layer):
    nd = len(shape)
    return pl.BlockSpec((None,) + shape, lambda *_: (layer,) + (0,) * nd, pipeline_mode=pl.Buffered(1))


def _mixer_weight_specs(layer):
    return [
        _layer_spec((1, D_MODEL), layer),
        _layer_spec((D_MODEL, IN_WIDTH), layer),
        _layer_spec((len(POOL_WINDOWS), POOL_GROUP, POOL_GROUP), layer),
        _layer_spec((1, POOL_WIDTH), layer),
        _layer_spec((D_MODEL, D_MODEL), layer),
    ]


def _mixer_long(h, rope, kin, vin, uin, sinks, ln, win, pw, ps, wout, *, tile, base_pos, layer, name):
    n_seq, seq_len, _ = h.shape
    tps = seq_len // tile
    n_tiles = n_seq * tps
    shared_init = kin.shape[0] == 1

    def proj_tile(i):
        return jnp.minimum(i, n_tiles - 1)

    def out_tile(i):
        return jnp.maximum(i - 2, 0)

    def init_idx(i):
        return (0 if shared_init else proj_tile(i) // tps, 0, 0)

    kern = functools.partial(_mixer_long_kernel, tile=tile, base_pos=base_pos, layer=layer,
                             tiles_per_seq=tps, n_tiles=n_tiles)
    buffer_set = [
        pltpu.VMEM((tile // WINDOW, N_HEADS * WINDOW, LANES), BF16),
        pltpu.VMEM((WINDOW + tile, KV_WIDTH), BF16),
        pltpu.VMEM((WINDOW + tile, KV_WIDTH), BF16),
        pltpu.VMEM((tile, POOL_WIDTH), BF16),
        pltpu.VMEM((tile, D_MODEL), BF16),
    ]
    return pl.pallas_call(
        kern,
        grid=(n_tiles + 2,),
        in_specs=[
            pl.BlockSpec(memory_space=pltpu.SMEM),
            pl.BlockSpec((1, tile, D_MODEL), lambda i: (proj_tile(i) // tps, proj_tile(i) % tps, 0)),
            pl.BlockSpec((1, tile, D_MODEL), lambda i: (out_tile(i) // tps, out_tile(i) % tps, 0)),
            pl.BlockSpec((tile, 2 * LANES), lambda i: (proj_tile(i) % tps, 0)),
            pl.BlockSpec((1, WINDOW, KV_WIDTH), init_idx),
            pl.BlockSpec((1, WINDOW, KV_WIDTH), init_idx),
            pl.BlockSpec((1, HIST_ROWS, POOL_WIDTH), init_idx),
        ] + _mixer_weight_specs(layer),
        out_specs=[
            pl.BlockSpec((1, tile, D_MODEL), lambda i: (out_tile(i) // tps, out_tile(i) % tps, 0)),
            pl.BlockSpec((1, WINDOW, KV_WIDTH), lambda i: (proj_tile(i) // tps, 0, 0)),
            pl.BlockSpec((1, WINDOW, KV_WIDTH), lambda i: (proj_tile(i) // tps, 0, 0)),
            pl.BlockSpec((1, HIST_ROWS, POOL_WIDTH), lambda i: (proj_tile(i) // tps, 0, 0)),
        ],
        out_shape=[
            jax.ShapeDtypeStruct((n_seq, seq_len, D_MODEL), F32),
            jax.ShapeDtypeStruct((n_seq, WINDOW, KV_WIDTH), F32),
            jax.ShapeDtypeStruct((n_seq, WINDOW, KV_WIDTH), F32),
            jax.ShapeDtypeStruct((n_seq, HIST_ROWS, POOL_WIDTH), F32),
        ],
        scratch_shapes=buffer_set + buffer_set + [pltpu.VMEM((HIST_ROWS + tile, POOL_WIDTH), F32)],
        compiler_params=pltpu.CompilerParams(
            dimension_semantics=("arbitrary",), vmem_limit_bytes=VMEM_LIMIT),
        name=name,
    )(sinks, h, h, rope, kin, vin, uin, ln, win, pw, ps, wout)


def _mixer_short_kernel(sink_ref, h_ref, rope_ref, ck_ref, cv_ref, st_ref, ln_ref, win_ref,
                        pw_ref, ps_ref, wout_ref, *rest, seqs, layer):
    if layer:
        kprev_ref, vprev_ref, sprev_ref = rest[:3]
        rest = rest[3:]
    ho_ref, kc_ref, vc_ref, sn_ref, qbuf, abuf, ubuf, dbuf = rest
    if layer:
        kc_ref[0:layer] = kprev_ref[...]
        vc_ref[0:layer] = vprev_ref[...]
        sn_ref[0:layer] = sprev_ref[...]
    rows = seqs * DEC_SEQ
    h = h_ref[...]
    qb, k, v, u = _split_rope(_in_proj(h, ln_ref, win_ref), rope_ref[...])
    for j in range(Q_BLOCKS):
        qbuf[:, j * LANES:(j + 1) * LANES] = qb[j]
    kept = WINDOW - DEC_SEQ
    kc_ref[layer, :, 0:kept, :] = ck_ref[:, DEC_SEQ:, :]
    vc_ref[layer, :, 0:kept, :] = cv_ref[:, DEC_SEQ:, :]
    kc_ref[layer, :, kept:, :] = k.reshape(seqs, DEC_SEQ, KV_WIDTH)
    vc_ref[layer, :, kept:, :] = v.reshape(seqs, DEC_SEQ, KV_WIDTH)

    groups = len(POOL_WINDOWS)
    for g in range(groups):
        ubuf[g] = u[:, g * POOL_GROUP:(g + 1) * POOL_GROUP]
    new_slabs = [[ubuf[g, pl.ds(t, seqs, stride=DEC_SEQ), :] for g in range(groups)] for t in range(DEC_SEQ)]
    for r in range(POOL_HIST):
        src_row = r + DEC_SEQ
        sn_ref[layer, r] = (st_ref[src_row] if src_row < POOL_HIST
                            else jnp.concatenate(new_slabs[src_row - POOL_HIST], axis=-1))
    for g, w in enumerate(POOL_WINDOWS):
        cols = slice(g * POOL_GROUP, (g + 1) * POOL_GROUP)
        slabs = [st_ref[r, :, cols] for r in range(POOL_HIST)] + [new_slabs[t][g] for t in range(DEC_SEQ)]
        cnt = float(min(PAST_LEN + 1, w))
        for t in range(DEC_SEQ):
            wsum = slabs[POOL_HIST + t]
            for back in range(1, w):
                wsum = wsum + slabs[POOL_HIST + t - back]
            dbuf[g, pl.ds(t, seqs, stride=DEC_SEQ), :] = wsum / cnt - slabs[POOL_HIST + t]

    stacked = N_HEADS * DEC_SEQ
    keys = WINDOW + 2 * DEC_SEQ
    lo = lax.broadcasted_iota(jnp.int32, (1, LANES), 1) < HEAD_DIM
    srow = lax.broadcasted_iota(jnp.int32, (stacked, keys), 0)
    kc = lax.broadcasted_iota(jnp.int32, (stacked, keys), 1)
    qi = jnp.bitwise_and(srow, DEC_SEQ - 1)
    bias = jnp.where((kc >= qi) & (kc <= qi + WINDOW), 0.0, NEG_INF)
    hrow = jnp.right_shift(lax.broadcasted_iota(jnp.int32, (stacked, 1), 0), DEC_SEQ.bit_length() - 1)
    sink = jnp.zeros((stacked, 1), F32)
    for hh in range(N_HEADS):
        sink = jnp.where(hrow == hh, sink_ref[layer, hh] * LOG2E, sink)
    zpad = jnp.zeros((DEC_SEQ, KV_WIDTH), F32)

    def group_body(gi, carry):
        ids = [gi * SAMPLE_UNROLL + n for n in range(SAMPLE_UNROLL)]
        starts = [pl.multiple_of(i * DEC_SEQ, DEC_SEQ) for i in ids]
        scores = []
        for i, r0 in zip(ids, starts):
            qrows = qbuf[pl.ds(r0, DEC_SEQ), :]
            q = _stack_heads([qrows[:, j * LANES:(j + 1) * LANES] for j in range(Q_BLOCKS)], lo)
            kall = jnp.concatenate([ck_ref[i], kc_ref[layer, i, kept:, :], zpad], axis=0).astype(BF16)
            scores.append(lax.dot_general(q, kall, (((1,), (1,)), ((), ())), preferred_element_type=F32))
        probs = [_softmax_sink(s + bias, sink) for s in scores]
        for i, r0, (prob, inv) in zip(ids, starts, probs):
            vall = jnp.concatenate([cv_ref[i], vc_ref[layer, i, kept:, :], zpad], axis=0).astype(BF16)
            o = jnp.dot(prob, vall, preferred_element_type=F32) * inv
            abuf[pl.ds(r0, DEC_SEQ), :] = jnp.concatenate(
                [jnp.where(lo, o[j * DEC_SEQ:(j + 1) * DEC_SEQ],
                           o[(j + Q_BLOCKS) * DEC_SEQ:(j + Q_BLOCKS + 1) * DEC_SEQ])
                 for j in range(Q_BLOCKS)], axis=-1)
        return carry

    lax.fori_loop(0, seqs // SAMPLE_UNROLL, group_body, 0)

    pm = _pool_project([dbuf[g] for g in range(groups)], pw_ref, ps_ref)

    mix = jnp.concatenate([abuf[...], pm], axis=-1).astype(BF16)
    ho_ref[...] = h + jnp.dot(mix, wout_ref[...], preferred_element_type=F32)


def _mixer_short(h, rope, ck, cv, st, sinks, ln, win, pw, ps, wout, kc_prev, vc_prev, sn_prev,
                 *, seqs, layer, name):
    n_rows = h.shape[0]
    rows = seqs * DEC_SEQ
    n_seq = n_rows // DEC_SEQ
    stacked_spec = pl.BlockSpec((layer + 1, seqs, WINDOW, KV_WIDTH), lambda i: (0, i, 0, 0))
    state_spec = pl.BlockSpec((layer + 1, POOL_HIST, seqs, POOL_WIDTH), lambda i: (0, 0, i, 0))
    prev_specs = [pl.BlockSpec((layer, seqs, WINDOW, KV_WIDTH), lambda i: (0, i, 0, 0))] * 2 + [
        pl.BlockSpec((layer, POOL_HIST, seqs, POOL_WIDTH), lambda i: (0, 0, i, 0))] if layer else []
    prev_args = (kc_prev, vc_prev, sn_prev) if layer else ()
    kern = functools.partial(_mixer_short_kernel, seqs=seqs, layer=layer)
    return pl.pallas_call(
        kern,
        grid=(n_rows // rows,),
        in_specs=[
            pl.BlockSpec(memory_space=pltpu.SMEM),
            pl.BlockSpec((rows, D_MODEL), lambda i: (i, 0)),
            _const_spec((rows, 2 * LANES)),
            pl.BlockSpec((None, seqs, WINDOW, KV_WIDTH), lambda i: (layer, i, 0, 0)),
            pl.BlockSpec((None, seqs, WINDOW, KV_WIDTH), lambda i: (layer, i, 0, 0)),
            pl.BlockSpec((None, POOL_HIST, seqs, POOL_WIDTH), lambda i: (layer, 0, i, 0)),
        ] + _mixer_weight_specs(layer) + prev_specs,
        out_specs=[
            pl.BlockSpec((rows, D_MODEL), lambda i: (i, 0)),
            stacked_spec,
            stacked_spec,
            state_spec,
        ],
        out_shape=[
            jax.ShapeDtypeStruct((n_rows, D_MODEL), F32),
            jax.ShapeDtypeStruct((layer + 1, n_seq, WINDOW, KV_WIDTH), F32),
            jax.ShapeDtypeStruct((layer + 1, n_seq, WINDOW, KV_WIDTH), F32),
            jax.ShapeDtypeStruct((layer + 1, POOL_HIST, n_seq, POOL_WIDTH), F32),
        ],
        scratch_shapes=[
            pltpu.VMEM((rows, ATTN_WIDTH), F32),
            pltpu.VMEM((rows, ATTN_WIDTH), F32),
            pltpu.VMEM((len(POOL_WINDOWS), rows, POOL_GROUP), F32),
            pltpu.VMEM((len(POOL_WINDOWS), rows, POOL_GROUP), F32),
        ],
        compiler_params=pltpu.CompilerParams(
            dimension_semantics=("arbitrary",), vmem_limit_bytes=VMEM_LIMIT),
        name=name,
    )(sinks, h, rope, ck, cv, st, ln, win, pw, ps, wout, *prev_args)


def _mlp_rows(h, ln_ref, wup_ref, wdown_ref, lnf_ref, final_norm):
    xn = _rmsnorm(h, ln_ref[...]).astype(BF16)

    def hidden(c):
        a = jnp.maximum(jnp.dot(xn, wup_ref[:, c * MLP_CHUNK:(c + 1) * MLP_CHUNK],
                                preferred_element_type=F32), 0.0)
        return (a * a).astype(BF16)

    def down(a, c):
        return jnp.dot(a, wdown_ref[c * MLP_CHUNK:(c + 1) * MLP_CHUNK, :], preferred_element_type=F32)

    out = h
    a_prev = hidden(0)
    for c in range(1, D_FF // MLP_CHUNK):
        a_next = hidden(c)
        out = out + down(a_prev, c - 1)
        a_prev = a_next
    out = out + down(a_prev, D_FF // MLP_CHUNK - 1)
    if final_norm:
        out = _rmsnorm(out, lnf_ref[...])
    return out


def _mlp_kernel(ln_ref, wup_ref, wdown_ref, lnf_ref, *refs, final_norm, step_ranges):
    n_groups = len(step_ranges)
    i = pl.program_id(0)
    for g, (first, last) in enumerate(step_ranges):
        h_ref, o_ref = refs[g], refs[n_groups + g]

        @pl.when((i >= first) & (i <= last))
        def _(h_ref=h_ref, o_ref=o_ref):
            o_ref[...] = _mlp_rows(h_ref[...], ln_ref, wup_ref, wdown_ref, lnf_ref, final_norm)


def _mlp(groups, ln, wup, wdown, lnf, *, final_norm, layer, name):
    tiles = [min(MLP_TILE if g == 0 else MLP_SMALL_TILE, h.shape[0]) for g, h in enumerate(groups)]
    steps = [h.shape[0] // t for h, t in zip(groups, tiles)]
    firsts = [sum(steps[:g]) for g in range(len(groups))]
    step_ranges = tuple((f, f + n - 1) for f, n in zip(firsts, steps))

    def row_spec(tile, first, n):
        return pl.BlockSpec((tile, D_MODEL), lambda i: (jnp.clip(i - first, 0, n - 1), 0))

    row_specs = [row_spec(t, f, n) for t, f, n in zip(tiles, firsts, steps)]
    kern = functools.partial(_mlp_kernel, final_norm=final_norm, step_ranges=step_ranges)
    return pl.pallas_call(
        kern,
        grid=(sum(steps),),
        in_specs=[
            _layer_spec((1, D_MODEL), layer),
            _layer_spec((D_MODEL, D_FF), layer),
            _layer_spec((D_FF, D_MODEL), layer),
            _const_spec((1, D_MODEL)),
        ] + row_specs,
        out_specs=row_specs,
        out_shape=[jax.ShapeDtypeStruct(h.shape, F32) for h in groups],
        compiler_params=pltpu.CompilerParams(
            dimension_semantics=("arbitrary",), vmem_limit_bytes=VMEM_LIMIT),
        name=name,
    )(ln, wup, wdown, lnf, *groups)


def _rope_table(pos):
    n = pos.shape[0]
    inv_freq = ROPE_THETA ** (-jnp.arange(0, ROT_DIM, 2, dtype=F32) / ROT_DIM)
    ang = pos.astype(F32)[:, None] * inv_freq[None, :]
    cos, sin = jnp.cos(ang), jnp.sin(ang)
    rest = HEAD_DIM - ROT_DIM
    c = jnp.concatenate([cos, cos, jnp.ones((n, rest), F32)], axis=-1)
    s = jnp.concatenate([-sin, sin, jnp.zeros((n, rest), F32)], axis=-1)
    reps = LANES // HEAD_DIM
    return jnp.concatenate([jnp.tile(c, (1, reps)), jnp.tile(s, (1, reps))], axis=-1)


def _pair_heads(w, axis):
    shape = w.shape
    split = shape[:axis] + (N_HEADS // Q_BLOCKS, Q_BLOCKS, HEAD_DIM) + shape[axis + 1:]
    return jnp.swapaxes(w.reshape(split), axis, axis + 1).reshape(shape)


def kernel(x_prompt, x_sample, cache_k, cache_v, state_pool, meta_tokens, ln1, w_in, attn_sinks,
           pool_w, pool_scale, w_out, ln2, w_up, w_down, ln_f):
    batch, seq, _ = x_prompt.shape
    dec_batch, dec_seq, _ = x_sample.shape
    depth = w_in.shape[0]
    assert dec_seq == DEC_SEQ and seq % PROMPT_TILE == 0 and dec_batch % SAMPLE_SEQS == 0

    pad = WINDOW - N_META
    rope_meta = _rope_table(jnp.arange(WINDOW) - pad)
    rope_prompt = _rope_table(N_META + jnp.arange(seq))
    rope_sample = jnp.tile(_rope_table(PAST_LEN + jnp.arange(DEC_SEQ)), (SAMPLE_SEQS, 1))

    win = jnp.concatenate([_pair_heads(w_in[:, :, :ATTN_WIDTH], 2), w_in[:, :, ATTN_WIDTH:]], axis=2).astype(BF16)
    wout = jnp.concatenate([_pair_heads(w_out[:, :ATTN_WIDTH], 1), w_out[:, ATTN_WIDTH:]], axis=1).astype(BF16)
    wup = w_up.astype(BF16)
    wdown = w_down.astype(BF16)
    pw = pool_w.astype(BF16)
    ps = pool_scale.reshape(depth, 1, POOL_WIDTH)
    l1 = ln1.reshape(depth, 1, D_MODEL)
    l2 = ln2.reshape(depth, 1, D_MODEL)
    lnf = ln_f.reshape(1, D_MODEL)
    sinks = attn_sinks.astype(F32)
    mixer_w = (sinks, l1, win, pw, ps, wout)

    hm = jnp.concatenate([jnp.zeros((pad, D_MODEL), F32), meta_tokens.astype(F32)], axis=0)[None]
    hp = x_prompt
    hs = x_sample.reshape(dec_batch * dec_seq, D_MODEL)
    zero_kv = jnp.zeros((1, WINDOW, KV_WIDTH), F32)
    zero_u = jnp.zeros((1, HIST_ROWS, POOL_WIDTH), F32)
    st = jnp.transpose(state_pool, (0, 2, 1, 3))
    ck = cache_k.reshape(depth, dec_batch, WINDOW, KV_WIDTH)
    cv = cache_v.reshape(depth, dec_batch, WINDOW, KV_WIDTH)

    pk, pv, pu = [], [], []
    kc = vc = sn = None
    for l in range(depth):
        last = l == depth - 1
        hm_mid, km, vm, um = _mixer_long(hm, rope_meta, zero_kv, zero_kv, zero_u, *mixer_w, tile=WINDOW,
                                         base_pos=-pad, layer=l, name=f"mixer_meta_{l}")
        hp_mid, kt, vt, ut = _mixer_long(hp, rope_prompt, km, vm, um, *mixer_w, tile=PROMPT_TILE,
                                         base_pos=N_META, layer=l, name=f"mixer_prompt_{l}")
        hs_mid, kc, vc, sn = _mixer_short(hs, rope_sample, ck, cv, st, *mixer_w, kc, vc, sn, seqs=SAMPLE_SEQS,
                                          layer=l, name=f"mixer_sample_{l}")

        groups = [hp_mid.reshape(batch * seq, D_MODEL), hs_mid] + ([] if last else [hm_mid[0]])
        outs = _mlp(groups, l2, wup, wdown, lnf, final_norm=last, layer=l, name=f"mlp_{l}")
        hp = outs[0].reshape(batch, seq, D_MODEL)
        hs = outs[1]
        if not last:
            hm = outs[2][None]

        pk.append(kt.reshape(batch, WINDOW, N_KV_HEADS, HEAD_DIM))
        pv.append(vt.reshape(batch, WINDOW, N_KV_HEADS, HEAD_DIM))
        pu.append(ut[:, 1:, :])

    y_sample = hs.reshape(dec_batch, dec_seq, D_MODEL)
    cache_shape = (depth, dec_batch, WINDOW, N_KV_HEADS, HEAD_DIM)
    return (hp, y_sample, jnp.stack(pk), jnp.stack(pv), jnp.stack(pu),
            kc.reshape(cache_shape), vc.reshape(cache_shape), jnp.transpose(sn, (0, 2, 1, 3)))
```

```python
import functools

import jax
import jax.numpy as jnp
from jax import lax
from jax.experimental import pallas as pl
from jax.experimental.pallas import tpu as pltpu

D_MODEL = 1024
N_HEADS = 8
N_KV_HEADS = 2
HEAD_DIM = 64
ATTN_WIDTH = N_HEADS * HEAD_DIM
KV_WIDTH = N_KV_HEADS * HEAD_DIM
POOL_WINDOWS = (2, 4, 8, 16)
POOL_WIDTH = D_MODEL - ATTN_WIDTH
POOL_GROUP = POOL_WIDTH // len(POOL_WINDOWS)
POOL_HIST = max(POOL_WINDOWS) - 1
IN_WIDTH = ATTN_WIDTH + 2 * KV_WIDTH + POOL_WIDTH
WINDOW = 128
ROT_DIM = HEAD_DIM // 4
ROPE_THETA = 500000.0
D_FF = 4 * D_MODEL
N_META = 16
RMS_EPS = 1e-5
PAST_LEN = 16384
DEC_SEQ = 8

LANES = 128
SAMPLE_GROUP = LANES // DEC_SEQ
HIST_ROWS = POOL_HIST + 1
Q_BLOCKS = ATTN_WIDTH // LANES
LOG2E = 1.4426950408889634
Q_SCALE = HEAD_DIM ** -0.5 * LOG2E
NEG_INF = float("-inf")

PROMPT_TILE = 512
SAMPLE_SEQS = 32
MLP_TILE = 1024
MLP_SMALL_TILE = 256
MLP_CHUNK = 1024
VMEM_LIMIT = 56 * 1024 * 1024

F32 = jnp.float32
BF16 = jnp.bfloat16


def _rmsnorm(x, g):
    r = lax.rsqrt(jnp.mean(x * x, axis=-1, keepdims=True) + RMS_EPS)
    return x * r * g


def _rope(x, rope):
    c = rope[:, 0:LANES]
    s = rope[:, LANES:2 * LANES]
    half = ROT_DIM // 2
    lane = lax.broadcasted_iota(jnp.int32, (1, LANES), 1)
    on_x1 = jnp.bitwise_and(lane, HEAD_DIM - 1) < half
    partner = jnp.where(on_x1, pltpu.roll(x, LANES - half, 1), pltpu.roll(x, half, 1))
    return x * c + partner * s


def _in_proj(h, ln_ref, win_ref):
    xn = _rmsnorm(h, ln_ref[...]).astype(BF16)
    return jnp.dot(xn, win_ref[...], preferred_element_type=F32)


def _split_rope(p, rope):
    qb = [_rope(p[:, j * LANES:(j + 1) * LANES], rope) * Q_SCALE for j in range(Q_BLOCKS)]
    k = _rope(p[:, ATTN_WIDTH:ATTN_WIDTH + KV_WIDTH], rope)
    v = p[:, ATTN_WIDTH + KV_WIDTH:ATTN_WIDTH + 2 * KV_WIDTH]
    u = p[:, ATTN_WIDTH + 2 * KV_WIDTH:]
    return qb, k, v, u


def _stack_heads(qrows, lo):
    zero = jnp.zeros_like(qrows[0])
    parts = [jnp.where(lo, q, zero) for q in qrows] + [jnp.where(lo, zero, q) for q in qrows]
    return jnp.concatenate(parts, axis=0).astype(BF16)


def _softmax_sink(s, sink):
    m = jnp.maximum(jnp.max(s, axis=-1, keepdims=True), sink)
    e = jnp.exp2(s - m)
    l = jnp.sum(e, axis=-1, keepdims=True) + jnp.exp2(sink - m)
    return e.astype(BF16), 1.0 / l


def _window_sum(x, w, axis):
    span = 1
    while span < w:
        x = x + pltpu.roll(x, span, axis)
        span *= 2
    return x


def _pool_project(d_groups, pw_ref, ps_ref):
    outs = [jnp.dot(d.astype(BF16), pw_ref[g], preferred_element_type=F32)
            for g, d in enumerate(d_groups)]
    return jnp.concatenate(outs, axis=-1) * ps_ref[...]


def _project_finish(p, rope_ref, kin_ref, vin_ref, uin_ref, pw_ref, ps_ref,
                    kt_ref, vt_ref, ut_ref, ubuf, dst, src, *, tile, first, p0):
    dq, dk, dv, dm, _ = dst
    _, sk, sv, _, _ = src
    qb, k, v, u = _split_rope(p, rope_ref[...])
    lo = lax.broadcasted_iota(jnp.int32, (1, LANES), 1) < HEAD_DIM
    for r in range(tile // WINDOW):
        rows = slice(r * WINDOW, (r + 1) * WINDOW)
        dq[r] = _stack_heads([qb[j][rows] for j in range(Q_BLOCKS)], lo)
    dk[0:WINDOW, :] = jnp.where(first, kin_ref[0].astype(BF16), sk[tile:tile + WINDOW, :])
    dv[0:WINDOW, :] = jnp.where(first, vin_ref[0].astype(BF16), sv[tile:tile + WINDOW, :])
    dk[WINDOW:WINDOW + tile, :] = k.astype(BF16)
    dv[WINDOW:WINDOW + tile, :] = v.astype(BF16)
    kt_ref[0] = k[tile - WINDOW:, :]
    vt_ref[0] = v[tile - WINDOW:, :]
    ut_ref[0] = u[tile - HIST_ROWS:, :]

    ubuf[0:HIST_ROWS, :] = jnp.where(first, uin_ref[0], ubuf[tile:tile + HIST_ROWS, :])
    ubuf[HIST_ROWS:HIST_ROWS + tile, :] = u
    pos = p0 + lax.broadcasted_iota(jnp.int32, (tile, 1), 0)
    d_groups = []
    for g, w in enumerate(POOL_WINDOWS):
        cols = slice(g * POOL_GROUP, (g + 1) * POOL_GROUP)
        wsum = _window_sum(ubuf[:, cols], w, 0)[HIST_ROWS:]
        cnt = jnp.clip(pos + 1, 1, w).astype(F32)
        d_groups.append(wsum / cnt - u[:, cols])
    dm[...] = _pool_project(d_groups, pw_ref, ps_ref).astype(BF16)


def _attend_scores(src, *, tile):
    sq, sk, _, _, _ = src
    return [lax.dot_general(sq[r], sk[r * WINDOW:(r + 2) * WINDOW, :], (((1,), (1,)), ((), ())),
                            preferred_element_type=F32) for r in range(tile // WINDOW)]


def _attend_values(scores, sink_ref, src, *, layer, p0):
    _, _, sv, _, _ = src
    lo = lax.broadcasted_iota(jnp.int32, (1, LANES), 1) < HEAD_DIM
    qi = lax.broadcasted_iota(jnp.int32, (WINDOW, 2 * WINDOW), 0)
    kc = lax.broadcasted_iota(jnp.int32, (WINDOW, 2 * WINDOW), 1)
    band = (kc >= qi) & (kc <= qi + WINDOW)
    attn_rows = []
    for r, s in enumerate(scores):
        vb = sv[r * WINDOW:(r + 2) * WINDOW, :]
        kpos = p0 + (r - 1) * WINDOW + kc
        bias = jnp.where(band & (kpos >= 0), 0.0, NEG_INF)
        probs, inv = zip(*[_softmax_sink(s[hh * WINDOW:(hh + 1) * WINDOW] + bias, sink_ref[layer, hh] * LOG2E)
                           for hh in range(N_HEADS)])
        o = jnp.dot(jnp.concatenate(probs, axis=0), vb, preferred_element_type=F32)
        heads = [o[hh * WINDOW:(hh + 1) * WINDOW] * inv[hh] for hh in range(N_HEADS)]
        attn_rows.append(jnp.concatenate(
            [jnp.where(lo, heads[j], heads[j + Q_BLOCKS]) for j in range(Q_BLOCKS)], axis=-1))
    return jnp.concatenate(attn_rows, axis=0).astype(BF16)


def _mixer_long_kernel(sink_ref, hp_ref, hr_ref, rope_ref, kin_ref, vin_ref, uin_ref, ln_ref, win_ref,
                       pw_ref, ps_ref, wout_ref, ho_ref, kt_ref, vt_ref, ut_ref,
                       qa, ka, va, ma, xa, qb, kb, vb, mb, xb, ubuf,
                       *, tile, base_pos, layer, tiles_per_seq, n_tiles):
    i = pl.program_id(0)
    tp = lax.rem(jnp.minimum(i, n_tiles - 1), tiles_per_seq)
    ts = lax.rem(jnp.clip(i - 1, 0, n_tiles - 1), tiles_per_seq)
    set_a, set_b = (qa, ka, va, ma, xa), (qb, kb, vb, mb, xb)

    @pl.when(i == 0)
    def _():
        for ref in (kb, vb, ubuf):
            ref[...] = jnp.zeros(ref.shape, ref.dtype)

    def step(dst, src, project, attend, output):
        if attend:
            scores = _attend_scores(src, tile=tile)
        if project:
            p = _in_proj(hp_ref[0], ln_ref, win_ref)
        if output:
            ho_ref[0] = hr_ref[0] + jnp.dot(dst[4][...], wout_ref[...], preferred_element_type=F32)
        if attend:
            attn = _attend_values(scores, sink_ref, src, layer=layer, p0=base_pos + ts * tile)
            src[4][...] = jnp.concatenate([attn, src[3][...]], axis=-1)
        if project:
            _project_finish(p, rope_ref, kin_ref, vin_ref, uin_ref, pw_ref, ps_ref, kt_ref, vt_ref, ut_ref,
                            ubuf, dst, src, tile=tile, first=tp == 0, p0=base_pos + tp * tile)

    variants = {}
    for idx in range(n_tiles + 2):
        key = (idx < n_tiles, 1 <= idx <= n_tiles, idx >= 2, idx % 2)
        variants.setdefault(key, []).append(idx)
    parity = lax.rem(i, 2)
    for (project, attend, output, par), steps in variants.items():
        dst, src = (set_a, set_b) if par == 0 else (set_b, set_a)
        cond = (i >= steps[0]) & (i <= steps[-1]) & (parity == par)
        pl.when(cond)(functools.partial(step, dst, src, project, attend, output))


def _const_spec(shape):
    nd = len(shape)
    return pl.BlockSpec(shape, lambda *_: (0,) * nd, pipeline_mode=pl.Buffered(1))


def _layer_spec(shape, layer):
    nd = len(shape)
    return pl.BlockSpec((None,) + shape, lambda *_: (layer,) + (0,) * nd, pipeline_mode=pl.Buffered(1))


def _mixer_weight_specs(layer):
    return [
        _layer_spec((1, D_MODEL), layer),
        _layer_spec((D_MODEL, IN_WIDTH), layer),
        _layer_spec((len(POOL_WINDOWS), POOL_GROUP, POOL_GROUP), layer),
        _layer_spec((1, POOL_WIDTH), layer),
        _layer_spec((D_MODEL, D_MODEL), layer),
    ]


def _mixer_long(h, rope, kin, vin, uin, sinks, ln, win, pw, ps, wout, *, tile, base_pos, layer, name):
    n_seq, seq_len, _ = h.shape
    tps = seq_len // tile
    n_tiles = n_seq * tps
    shared_init = kin.shape[0] == 1

    def proj_tile(i):
        return jnp.minimum(i, n_tiles - 1)

    def out_tile(i):
        return jnp.maximum(i - 2, 0)

    def init_idx(i):
        return (0 if shared_init else proj_tile(i) // tps, 0, 0)

    kern = functools.partial(_mixer_long_kernel, tile=tile, base_pos=base_pos, layer=layer,
                             tiles_per_seq=tps, n_tiles=n_tiles)
    buffer_set = [
        pltpu.VMEM((tile // WINDOW, N_HEADS * WINDOW, LANES), BF16),
        pltpu.VMEM((WINDOW + tile, KV_WIDTH), BF16),
        pltpu.VMEM((WINDOW + tile, KV_WIDTH), BF16),
        pltpu.VMEM((tile, POOL_WIDTH), BF16),
        pltpu.VMEM((tile, D_MODEL), BF16),
    ]
    return pl.pallas_call(
        kern,
        grid=(n_tiles + 2,),
        in_specs=[
            pl.BlockSpec(memory_space=pltpu.SMEM),
            pl.BlockSpec((1, tile, D_MODEL), lambda i: (proj_tile(i) // tps, proj_tile(i) % tps, 0)),
            pl.BlockSpec((1, tile, D_MODEL), lambda i: (out_tile(i) // tps, out_tile(i) % tps, 0)),
            pl.BlockSpec((tile, 2 * LANES), lambda i: (proj_tile(i) % tps, 0)),
            pl.BlockSpec((1, WINDOW, KV_WIDTH), init_idx),
            pl.BlockSpec((1, WINDOW, KV_WIDTH), init_idx),
            pl.BlockSpec((1, HIST_ROWS, POOL_WIDTH), init_idx),
        ] + _mixer_weight_specs(layer),
        out_specs=[
            pl.BlockSpec((1, tile, D_MODEL), lambda i: (out_tile(i) // tps, out_tile(i) % tps, 0)),
            pl.BlockSpec((1, WINDOW, KV_WIDTH), lambda i: (proj_tile(i) // tps, 0, 0)),
            pl.BlockSpec((1, WINDOW, KV_WIDTH), lambda i: (proj_tile(i) // tps, 0, 0)),
            pl.BlockSpec((1, HIST_ROWS, POOL_WIDTH), lambda i: (proj_tile(i) // tps, 0, 0)),
        ],
        out_shape=[
            jax.ShapeDtypeStruct((n_seq, seq_len, D_MODEL), F32),
            jax.ShapeDtypeStruct((n_seq, WINDOW, KV_WIDTH), F32),
            jax.ShapeDtypeStruct((n_seq, WINDOW, KV_WIDTH), F32),
            jax.ShapeDtypeStruct((n_seq, HIST_ROWS, POOL_WIDTH), F32),
        ],
        scratch_shapes=buffer_set + buffer_set + [pltpu.VMEM((HIST_ROWS + tile, POOL_WIDTH), F32)],
        compiler_params=pltpu.CompilerParams(
            dimension_semantics=("arbitrary",), vmem_limit_bytes=VMEM_LIMIT),
        name=name,
    )(sinks, h, h, rope, kin, vin, uin, ln, win, pw, ps, wout)


def _mixer_short_kernel(sink_ref, h_ref, rope_ref, ck_ref, cv_ref, st_ref, ln_ref, win_ref,
                        pw_ref, ps_ref, wout_ref, *rest, seqs, layer):
    if layer:
        kprev_ref, vprev_ref, sprev_ref = rest[:3]
        rest = rest[3:]
    ho_ref, kc_ref, vc_ref, sn_ref, qbuf, abuf, ubuf, dbuf, ktbuf, vtbuf = rest
    if layer:
        kc_ref[0:layer] = kprev_ref[...]
        vc_ref[0:layer] = vprev_ref[...]
        sn_ref[0:layer] = sprev_ref[...]
    rows = seqs * DEC_SEQ
    h = h_ref[...]
    qb, k, v, u = _split_rope(_in_proj(h, ln_ref, win_ref), rope_ref[...])
    for j in range(Q_BLOCKS):
        qbuf[:, j * LANES:(j + 1) * LANES] = qb[j]
    kt, vt = k.T, v.T
    for b in range(rows // LANES):
        ktbuf[b] = kt[:, b * LANES:(b + 1) * LANES]
        vtbuf[b] = vt[:, b * LANES:(b + 1) * LANES]

    groups = len(POOL_WINDOWS)
    for g in range(groups):
        ubuf[g] = u[:, g * POOL_GROUP:(g + 1) * POOL_GROUP]
    new_slabs = [[ubuf[g, pl.ds(t, seqs, stride=DEC_SEQ), :] for g in range(groups)] for t in range(DEC_SEQ)]
    for r in range(POOL_HIST):
        src_row = r + DEC_SEQ
        sn_ref[layer, r] = (st_ref[src_row] if src_row < POOL_HIST
                            else jnp.concatenate(new_slabs[src_row - POOL_HIST], axis=-1))
    for g, w in enumerate(POOL_WINDOWS):
        cols = slice(g * POOL_GROUP, (g + 1) * POOL_GROUP)
        slabs = [st_ref[r, :, cols] for r in range(POOL_HIST)] + [new_slabs[t][g] for t in range(DEC_SEQ)]
        cnt = float(min(PAST_LEN + 1, w))
        for t in range(DEC_SEQ):
            wsum = slabs[POOL_HIST + t]
            for back in range(1, w):
                wsum = wsum + slabs[POOL_HIST + t - back]
            dbuf[g, pl.ds(t, seqs, stride=DEC_SEQ), :] = wsum / cnt - slabs[POOL_HIST + t]

    stacked = N_HEADS * DEC_SEQ
    kept = WINDOW - DEC_SEQ
    lane = lax.broadcasted_iota(jnp.int32, (1, LANES), 1)
    lo = lane < HEAD_DIM
    old = lane < kept
    srow = lax.broadcasted_iota(jnp.int32, (stacked, 2 * LANES), 0)
    kc = lax.broadcasted_iota(jnp.int32, (stacked, 2 * LANES), 1)
    qi = jnp.bitwise_and(srow, DEC_SEQ - 1)
    visible = ((kc < LANES) & (kc >= qi)) | ((kc >= LANES + kept) & (kc - (LANES + kept) <= qi))
    bias = jnp.where(visible, 0.0, NEG_INF)
    hrow = jnp.right_shift(lax.broadcasted_iota(jnp.int32, (stacked, 1), 0), DEC_SEQ.bit_length() - 1)
    sink = jnp.zeros((stacked, 1), F32)
    for hh in range(N_HEADS):
        sink = jnp.where(hrow == hh, sink_ref[layer, hh] * LOG2E, sink)

    def shifted(cache_ref, new_ref, gi, i, n):
        old_t = cache_ref[i]
        new_t = pltpu.roll(new_ref[gi], (kept - DEC_SEQ * n) % LANES, 1)
        return old_t, new_t, jnp.where(old, pltpu.roll(old_t, kept, 1), new_t)

    def group_body(gi, carry):
        ids = [gi * SAMPLE_GROUP + n for n in range(SAMPLE_GROUP)]
        starts = [pl.multiple_of(i * DEC_SEQ, DEC_SEQ) for i in ids]
        scores = []
        for n, (i, r0) in enumerate(zip(ids, starts)):
            qrows = qbuf[pl.ds(r0, DEC_SEQ), :]
            q = _stack_heads([qrows[:, j * LANES:(j + 1) * LANES] for j in range(Q_BLOCKS)], lo)
            old_t, new_t, kc_ref[layer, i] = shifted(ck_ref, ktbuf, gi, i, n)
            kall = jnp.concatenate([old_t, new_t], axis=1).astype(BF16)
            scores.append(jnp.dot(q, kall, preferred_element_type=F32))
        probs = [_softmax_sink(s + bias, sink) for s in scores]
        for n, (i, r0, (prob, inv)) in enumerate(zip(ids, starts, probs)):
            old_t, new_t, vc_ref[layer, i] = shifted(cv_ref, vtbuf, gi, i, n)
            vall = jnp.concatenate([old_t, new_t], axis=1).astype(BF16)
            o = lax.dot_general(prob, vall, (((1,), (1,)), ((), ())), preferred_element_type=F32) * inv
            abuf[pl.ds(r0, DEC_SEQ), :] = jnp.concatenate(
                [jnp.where(lo, o[j * DEC_SEQ:(j + 1) * DEC_SEQ],
                           o[(j + Q_BLOCKS) * DEC_SEQ:(j + Q_BLOCKS + 1) * DEC_SEQ])
                 for j in range(Q_BLOCKS)], axis=-1)
        return carry

    lax.fori_loop(0, seqs // SAMPLE_GROUP, group_body, 0)

    pm = _pool_project([dbuf[g] for g in range(groups)], pw_ref, ps_ref)

    mix = jnp.concatenate([abuf[...], pm], axis=-1).astype(BF16)
    ho_ref[...] = h + jnp.dot(mix, wout_ref[...], preferred_element_type=F32)


def _mixer_short(h, rope, ck, cv, st, sinks, ln, win, pw, ps, wout, kc_prev, vc_prev, sn_prev,
                 *, seqs, layer, name):
    n_rows = h.shape[0]
    rows = seqs * DEC_SEQ
    n_seq = n_rows // DEC_SEQ
    stacked_spec = pl.BlockSpec((layer + 1, seqs, KV_WIDTH, WINDOW), lambda i: (0, i, 0, 0))
    state_spec = pl.BlockSpec((layer + 1, POOL_HIST, seqs, POOL_WIDTH), lambda i: (0, 0, i, 0))
    prev_specs = [pl.BlockSpec((layer, seqs, KV_WIDTH, WINDOW), lambda i: (0, i, 0, 0))] * 2 + [
        pl.BlockSpec((layer, POOL_HIST, seqs, POOL_WIDTH), lambda i: (0, 0, i, 0))] if layer else []
    prev_args = (kc_prev, vc_prev, sn_prev) if layer else ()
    kern = functools.partial(_mixer_short_kernel, seqs=seqs, layer=layer)
    return pl.pallas_call(
        kern,
        grid=(n_rows // rows,),
        in_specs=[
            pl.BlockSpec(memory_space=pltpu.SMEM),
            pl.BlockSpec((rows, D_MODEL), lambda i: (i, 0)),
            _const_spec((rows, 2 * LANES)),
            pl.BlockSpec((None, seqs, KV_WIDTH, WINDOW), lambda i: (layer, i, 0, 0)),
            pl.BlockSpec((None, seqs, KV_WIDTH, WINDOW), lambda i: (layer, i, 0, 0)),
            pl.BlockSpec((None, POOL_HIST, seqs, POOL_WIDTH), lambda i: (layer, 0, i, 0)),
        ] + _mixer_weight_specs(layer) + prev_specs,
        out_specs=[
            pl.BlockSpec((rows, D_MODEL), lambda i: (i, 0)),
            stacked_spec,
            stacked_spec,
            state_spec,
        ],
        out_shape=[
            jax.ShapeDtypeStruct((n_rows, D_MODEL), F32),
            jax.ShapeDtypeStruct((layer + 1, n_seq, KV_WIDTH, WINDOW), F32),
            jax.ShapeDtypeStruct((layer + 1, n_seq, KV_WIDTH, WINDOW), F32),
            jax.ShapeDtypeStruct((layer + 1, POOL_HIST, n_seq, POOL_WIDTH), F32),
        ],
        scratch_shapes=[
            pltpu.VMEM((rows, ATTN_WIDTH), F32),
            pltpu.VMEM((rows, ATTN_WIDTH), F32),
            pltpu.VMEM((len(POOL_WINDOWS), rows, POOL_GROUP), F32),
            pltpu.VMEM((len(POOL_WINDOWS), rows, POOL_GROUP), F32),
            pltpu.VMEM((rows // LANES, KV_WIDTH, LANES), F32),
            pltpu.VMEM((rows // LANES, KV_WIDTH, LANES), F32),
        ],
        compiler_params=pltpu.CompilerParams(
            dimension_semantics=("arbitrary",), vmem_limit_bytes=VMEM_LIMIT),
        name=name,
    )(sinks, h, rope, ck, cv, st, ln, win, pw, ps, wout, *prev_args)


def _mlp_rows(h, ln_ref, wup_ref, wdown_ref, lnf_ref, final_norm):
    xn = _rmsnorm(h, ln_ref[...]).astype(BF16)

    def hidden(c):
        a = jnp.maximum(jnp.dot(xn, wup_ref[:, c * MLP_CHUNK:(c + 1) * MLP_CHUNK],
                                preferred_element_type=F32), 0.0)
        return (a * a).astype(BF16)

    def down(a, c):
        return jnp.dot(a, wdown_ref[c * MLP_CHUNK:(c + 1) * MLP_CHUNK, :], preferred_element_type=F32)

    out = h
    a_prev = hidden(0)
    for c in range(1, D_FF // MLP_CHUNK):
        a_next = hidden(c)
        out = out + down(a_prev, c - 1)
        a_prev = a_next
    out = out + down(a_prev, D_FF // MLP_CHUNK - 1)
    if final_norm:
        out = _rmsnorm(out, lnf_ref[...])
    return out


def _mlp_kernel(ln_ref, wup_ref, wdown_ref, lnf_ref, *refs, final_norm, step_ranges):
    n_groups = len(step_ranges)
    i = pl.program_id(0)
    for g, (first, last) in enumerate(step_ranges):
        h_ref, o_ref = refs[g], refs[n_groups + g]

        @pl.when((i >= first) & (i <= last))
        def _(h_ref=h_ref, o_ref=o_ref):
            o_ref[...] = _mlp_rows(h_ref[...], ln_ref, wup_ref, wdown_ref, lnf_ref, final_norm)


def _mlp(groups, ln, wup, wdown, lnf, *, final_norm, layer, name):
    tiles = [min(MLP_TILE if g == 0 else MLP_SMALL_TILE, h.shape[0]) for g, h in enumerate(groups)]
    steps = [h.shape[0] // t for h, t in zip(groups, tiles)]
    firsts = [sum(steps[:g]) for g in range(len(groups))]
    step_ranges = tuple((f, f + n - 1) for f, n in zip(firsts, steps))

    def row_spec(tile, first, n):
        return pl.BlockSpec((tile, D_MODEL), lambda i: (jnp.clip(i - first, 0, n - 1), 0))

    row_specs = [row_spec(t, f, n) for t, f, n in zip(tiles, firsts, steps)]
    kern = functools.partial(_mlp_kernel, final_norm=final_norm, step_ranges=step_ranges)
    return pl.pallas_call(
        kern,
        grid=(sum(steps),),
        in_specs=[
            _layer_spec((1, D_MODEL), layer),
            _layer_spec((D_MODEL, D_FF), layer),
            _layer_spec((D_FF, D_MODEL), layer),
            _const_spec((1, D_MODEL)),
        ] + row_specs,
        out_specs=row_specs,
        out_shape=[jax.ShapeDtypeStruct(h.shape, F32) for h in groups],
        compiler_params=pltpu.CompilerParams(
            dimension_semantics=("arbitrary",), vmem_limit_bytes=VMEM_LIMIT),
        name=name,
    )(ln, wup, wdown, lnf, *groups)


def _rope_table(pos):
    n = pos.shape[0]
    inv_freq = ROPE_THETA ** (-jnp.arange(0, ROT_DIM, 2, dtype=F32) / ROT_DIM)
    ang = pos.astype(F32)[:, None] * inv_freq[None, :]
    cos, sin = jnp.cos(ang), jnp.sin(ang)
    rest = HEAD_DIM - ROT_DIM
    c = jnp.concatenate([cos, cos, jnp.ones((n, rest), F32)], axis=-1)
    s = jnp.concatenate([-sin, sin, jnp.zeros((n, rest), F32)], axis=-1)
    reps = LANES // HEAD_DIM
    return jnp.concatenate([jnp.tile(c, (1, reps)), jnp.tile(s, (1, reps))], axis=-1)


def _pair_heads(w, axis):
    shape = w.shape
    split = shape[:axis] + (N_HEADS // Q_BLOCKS, Q_BLOCKS, HEAD_DIM) + shape[axis + 1:]
    return jnp.swapaxes(w.reshape(split), axis, axis + 1).reshape(shape)


def kernel(x_prompt, x_sample, cache_k, cache_v, state_pool, meta_tokens, ln1, w_in, attn_sinks,
           pool_w, pool_scale, w_out, ln2, w_up, w_down, ln_f):
    batch, seq, _ = x_prompt.shape
    dec_batch, dec_seq, _ = x_sample.shape
    depth = w_in.shape[0]
    assert dec_seq == DEC_SEQ and seq % PROMPT_TILE == 0 and dec_batch % SAMPLE_SEQS == 0

    pad = WINDOW - N_META
    rope_meta = _rope_table(jnp.arange(WINDOW) - pad)
    rope_prompt = _rope_table(N_META + jnp.arange(seq))
    rope_sample = jnp.tile(_rope_table(PAST_LEN + jnp.arange(DEC_SEQ)), (SAMPLE_SEQS, 1))

    win = jnp.concatenate([_pair_heads(w_in[:, :, :ATTN_WIDTH], 2), w_in[:, :, ATTN_WIDTH:]], axis=2).astype(BF16)
    wout = jnp.concatenate([_pair_heads(w_out[:, :ATTN_WIDTH], 1), w_out[:, ATTN_WIDTH:]], axis=1).astype(BF16)
    wup = w_up.astype(BF16)
    wdown = w_down.astype(BF16)
    pw = pool_w.astype(BF16)
    ps = pool_scale.reshape(depth, 1, POOL_WIDTH)
    l1 = ln1.reshape(depth, 1, D_MODEL)
    l2 = ln2.reshape(depth, 1, D_MODEL)
    lnf = ln_f.reshape(1, D_MODEL)
    sinks = attn_sinks.astype(F32)
    mixer_w = (sinks, l1, win, pw, ps, wout)

    hm = jnp.concatenate([jnp.zeros((pad, D_MODEL), F32), meta_tokens.astype(F32)], axis=0)[None]
    hp = x_prompt
    hs = x_sample.reshape(dec_batch * dec_seq, D_MODEL)
    zero_kv = jnp.zeros((1, WINDOW, KV_WIDTH), F32)
    zero_u = jnp.zeros((1, HIST_ROWS, POOL_WIDTH), F32)
    st = jnp.transpose(state_pool, (0, 2, 1, 3))
    ck = jnp.transpose(cache_k, (0, 1, 3, 4, 2)).reshape(depth, dec_batch, KV_WIDTH, WINDOW)
    cv = jnp.transpose(cache_v, (0, 1, 3, 4, 2)).reshape(depth, dec_batch, KV_WIDTH, WINDOW)

    pk, pv, pu = [], [], []
    kc = vc = sn = None
    for l in range(depth):
        last = l == depth - 1
        hm_mid, km, vm, um = _mixer_long(hm, rope_meta, zero_kv, zero_kv, zero_u, *mixer_w, tile=WINDOW,
                                         base_pos=-pad, layer=l, name=f"mixer_meta_{l}")
        hp_mid, kt, vt, ut = _mixer_long(hp, rope_prompt, km, vm, um, *mixer_w, tile=PROMPT_TILE,
                                         base_pos=N_META, layer=l, name=f"mixer_prompt_{l}")
        hs_mid, kc, vc, sn = _mixer_short(hs, rope_sample, ck, cv, st, *mixer_w, kc, vc, sn, seqs=SAMPLE_SEQS,
                                          layer=l, name=f"mixer_sample_{l}")

        groups = [hp_mid.reshape(batch * seq, D_MODEL), hs_mid] + ([] if last else [hm_mid[0]])
        outs = _mlp(groups, l2, wup, wdown, lnf, final_norm=last, layer=l, name=f"mlp_{l}")
        hp = outs[0].reshape(batch, seq, D_MODEL)
        hs = outs[1]
        if not last:
            hm = outs[2][None]

        pk.append(kt.reshape(batch, WINDOW, N_KV_HEADS, HEAD_DIM))
        pv.append(vt.reshape(batch, WINDOW, N_KV_HEADS, HEAD_DIM))
        pu.append(ut[:, 1:, :])

    y_sample = hs.reshape(dec_batch, dec_seq, D_MODEL)
    def window_major(c):
        return jnp.transpose(c.reshape(depth, dec_batch, N_KV_HEADS, HEAD_DIM, WINDOW), (0, 1, 4, 2, 3))

    return (hp, y_sample, jnp.stack(pk), jnp.stack(pv), jnp.stack(pu),
            window_major(kc), window_major(vc), jnp.transpose(sn, (0, 2, 1, 3)))
```

```python
import functools

import jax
import jax.numpy as jnp
from jax import lax
from jax.experimental import pallas as pl
from jax.experimental.pallas import tpu as pltpu

D_MODEL = 1024
N_HEADS = 8
N_KV_HEADS = 2
HEAD_DIM = 64
ATTN_WIDTH = N_HEADS * HEAD_DIM
KV_WIDTH = N_KV_HEADS * HEAD_DIM
POOL_WINDOWS = (2, 4, 8, 16)
POOL_WIDTH = D_MODEL - ATTN_WIDTH
POOL_GROUP = POOL_WIDTH // len(POOL_WINDOWS)
POOL_HIST = max(POOL_WINDOWS) - 1
IN_WIDTH = ATTN_WIDTH + 2 * KV_WIDTH + POOL_WIDTH
WINDOW = 128
ROT_DIM = HEAD_DIM // 4
ROPE_THETA = 500000.0
D_FF = 4 * D_MODEL
N_META = 16
RMS_EPS = 1e-5
PAST_LEN = 16384
DEC_SEQ = 8

LANES = 128
SAMPLE_GROUP = LANES // DEC_SEQ
HIST_ROWS = POOL_HIST + 1
Q_BLOCKS = ATTN_WIDTH // LANES
LOG2E = 1.4426950408889634
Q_SCALE = HEAD_DIM ** -0.5 * LOG2E
NEG_INF = float("-inf")

PROMPT_TILE = 512
SAMPLE_SEQS = 32
MLP_TILE = 1024
MLP_SMALL_TILE = 256
MLP_CHUNK = 1024
VMEM_LIMIT = 56 * 1024 * 1024

F32 = jnp.float32
BF16 = jnp.bfloat16


def _rmsnorm(x, g):
    r = lax.rsqrt(jnp.mean(x * x, axis=-1, keepdims=True) + RMS_EPS)
    return x * r * g


def _rope(x, rope):
    c = rope[:, 0:LANES]
    s = rope[:, LANES:2 * LANES]
    half = ROT_DIM // 2
    lane = lax.broadcasted_iota(jnp.int32, (1, LANES), 1)
    on_x1 = jnp.bitwise_and(lane, HEAD_DIM - 1) < half
    partner = jnp.where(on_x1, pltpu.roll(x, LANES - half, 1), pltpu.roll(x, half, 1))
    return x * c + partner * s


def _in_proj(h, ln_ref, win_ref):
    xn = _rmsnorm(h, ln_ref[...]).astype(BF16)
    return jnp.dot(xn, win_ref[...], preferred_element_type=F32)


def _split_rope(p, rope):
    qb = [_rope(p[:, j * LANES:(j + 1) * LANES], rope) * Q_SCALE for j in range(Q_BLOCKS)]
    k = _rope(p[:, ATTN_WIDTH:ATTN_WIDTH + KV_WIDTH], rope)
    v = p[:, ATTN_WIDTH + KV_WIDTH:ATTN_WIDTH + 2 * KV_WIDTH]
    u = p[:, ATTN_WIDTH + 2 * KV_WIDTH:]
    return qb, k, v, u


def _stack_heads(qrows, lo):
    zero = jnp.zeros_like(qrows[0])
    parts = [jnp.where(lo, q, zero) for q in qrows] + [jnp.where(lo, zero, q) for q in qrows]
    return jnp.concatenate(parts, axis=0).astype(BF16)


def _softmax_sink(s, sink):
    m = jnp.maximum(jnp.max(s, axis=-1, keepdims=True), sink)
    e = jnp.exp2(s - m)
    l = jnp.sum(e, axis=-1, keepdims=True) + jnp.exp2(sink - m)
    return e.astype(BF16), 1.0 / l


def _window_sum(x, w, axis):
    span = 1
    while span < w:
        x = x + pltpu.roll(x, span, axis)
        span *= 2
    return x


def _pool_project(d_groups, pw_ref, ps_ref):
    outs = [jnp.dot(d.astype(BF16), pw_ref[g], preferred_element_type=F32)
            for g, d in enumerate(d_groups)]
    return jnp.concatenate(outs, axis=-1) * ps_ref[...]


def _project_finish(p, rope_ref, kin_ref, vin_ref, uin_ref, pw_ref, ps_ref,
                    kt_ref, vt_ref, ut_ref, ubuf, dst, src, *, tile, first, p0):
    dq, dk, dv, dm, _ = dst
    _, sk, sv, _, _ = src
    qb, k, v, u = _split_rope(p, rope_ref[...])
    lo = lax.broadcasted_iota(jnp.int32, (1, LANES), 1) < HEAD_DIM
    for r in range(tile // WINDOW):
        rows = slice(r * WINDOW, (r + 1) * WINDOW)
        dq[r] = _stack_heads([qb[j][rows] for j in range(Q_BLOCKS)], lo)
    dk[0:WINDOW, :] = jnp.where(first, kin_ref[0].astype(BF16), sk[tile:tile + WINDOW, :])
    dv[0:WINDOW, :] = jnp.where(first, vin_ref[0].astype(BF16), sv[tile:tile + WINDOW, :])
    dk[WINDOW:WINDOW + tile, :] = k.astype(BF16)
    dv[WINDOW:WINDOW + tile, :] = v.astype(BF16)
    kt_ref[0] = k[tile - WINDOW:, :]
    vt_ref[0] = v[tile - WINDOW:, :]
    ut_ref[0] = u[tile - HIST_ROWS:, :]

    ubuf[0:HIST_ROWS, :] = jnp.where(first, uin_ref[0], ubuf[tile:tile + HIST_ROWS, :])
    ubuf[HIST_ROWS:HIST_ROWS + tile, :] = u
    pos = p0 + lax.broadcasted_iota(jnp.int32, (tile, 1), 0)
    d_groups = []
    for g, w in enumerate(POOL_WINDOWS):
        cols = slice(g * POOL_GROUP, (g + 1) * POOL_GROUP)
        wsum = _window_sum(ubuf[:, cols], w, 0)[HIST_ROWS:]
        cnt = jnp.clip(pos + 1, 1, w).astype(F32)
        d_groups.append(wsum / cnt - u[:, cols])
    dm[...] = _pool_project(d_groups, pw_ref, ps_ref).astype(BF16)


def _attend_scores(src, *, tile):
    sq, sk, _, _, _ = src
    return [lax.dot_general(sq[r], sk[r * WINDOW:(r + 2) * WINDOW, :], (((1,), (1,)), ((), ())),
                            preferred_element_type=F32) for r in range(tile // WINDOW)]


def _attend_values(scores, sink_ref, src, *, layer, p0):
    _, _, sv, _, _ = src
    lo = lax.broadcasted_iota(jnp.int32, (1, LANES), 1) < HEAD_DIM
    qi = lax.broadcasted_iota(jnp.int32, (WINDOW, 2 * WINDOW), 0)
    kc = lax.broadcasted_iota(jnp.int32, (WINDOW, 2 * WINDOW), 1)
    band = (kc >= qi) & (kc <= qi + WINDOW)
    attn_rows = []
    for r, s in enumerate(scores):
        vb = sv[r * WINDOW:(r + 2) * WINDOW, :]
        kpos = p0 + (r - 1) * WINDOW + kc
        bias = jnp.where(band & (kpos >= 0), 0.0, NEG_INF)
        probs, inv = zip(*[_softmax_sink(s[hh * WINDOW:(hh + 1) * WINDOW] + bias, sink_ref[layer, hh] * LOG2E)
                           for hh in range(N_HEADS)])
        o = jnp.dot(jnp.concatenate(probs, axis=0), vb, preferred_element_type=F32)
        heads = [o[hh * WINDOW:(hh + 1) * WINDOW] * inv[hh] for hh in range(N_HEADS)]
        attn_rows.append(jnp.concatenate(
            [jnp.where(lo, heads[j], heads[j + Q_BLOCKS]) for j in range(Q_BLOCKS)], axis=-1))
    return jnp.concatenate(attn_rows, axis=0).astype(BF16)


def _mixer_long_kernel(sink_ref, hp_ref, hr_ref, rope_ref, kin_ref, vin_ref, uin_ref, ln_ref, win_ref,
                       pw_ref, ps_ref, wout_ref, ho_ref, kt_ref, vt_ref, ut_ref,
                       qa, ka, va, ma, xa, qb, kb, vb, mb, xb, ubuf,
                       *, tile, base_pos, layer, tiles_per_seq, n_tiles):
    i = pl.program_id(0)
    tp = lax.rem(jnp.minimum(i, n_tiles - 1), tiles_per_seq)
    ts = lax.rem(jnp.clip(i - 1, 0, n_tiles - 1), tiles_per_seq)
    set_a, set_b = (qa, ka, va, ma, xa), (qb, kb, vb, mb, xb)

    @pl.when(i == 0)
    def _():
        for ref in (kb, vb, ubuf):
            ref[...] = jnp.zeros(ref.shape, ref.dtype)

    def step(dst, src, project, attend, output):
        if attend:
            scores = _attend_scores(src, tile=tile)
        if project:
            p = _in_proj(hp_ref[0], ln_ref, win_ref)
        if output:
            ho_ref[0] = hr_ref[0] + jnp.dot(dst[4][...], wout_ref[...], preferred_element_type=F32)
        if attend:
            attn = _attend_values(scores, sink_ref, src, layer=layer, p0=base_pos + ts * tile)
            src[4][...] = jnp.concatenate([attn, src[3][...]], axis=-1)
        if project:
            _project_finish(p, rope_ref, kin_ref, vin_ref, uin_ref, pw_ref, ps_ref, kt_ref, vt_ref, ut_ref,
                            ubuf, dst, src, tile=tile, first=tp == 0, p0=base_pos + tp * tile)

    variants = {}
    for idx in range(n_tiles + 2):
        key = (idx < n_tiles, 1 <= idx <= n_tiles, idx >= 2, idx % 2)
        variants.setdefault(key, []).append(idx)
    parity = lax.rem(i, 2)
    for (project, attend, output, par), steps in variants.items():
        dst, src = (set_a, set_b) if par == 0 else (set_b, set_a)
        cond = (i >= steps[0]) & (i <= steps[-1]) & (parity == par)
        pl.when(cond)(functools.partial(step, dst, src, project, attend, output))


def _const_spec(shape):
    nd = len(shape)
    return pl.BlockSpec(shape, lambda *_: (0,) * nd, pipeline_mode=pl.Buffered(1))


def _layer_spec(shape, layer):
    nd = len(shape)
    return pl.BlockSpec((None,) + shape, lambda *_: (layer,) + (0,) * nd, pipeline_mode=pl.Buffered(1))


def _mixer_weight_specs(layer):
    return [
        _layer_spec((1, D_MODEL), layer),
        _layer_spec((D_MODEL, IN_WIDTH), layer),
        _layer_spec((len(POOL_WINDOWS), POOL_GROUP, POOL_GROUP), layer),
        _layer_spec((1, POOL_WIDTH), layer),
        _layer_spec((D_MODEL, D_MODEL), layer),
    ]


def _mixer_long(h, rope, kin, vin, uin, sinks, ln, win, pw, ps, wout, *, tile, base_pos, layer, name):
    n_seq, seq_len, _ = h.shape
    tps = seq_len // tile
    n_tiles = n_seq * tps
    shared_init = kin.shape[0] == 1

    def proj_tile(i):
        return jnp.minimum(i, n_tiles - 1)

    def out_tile(i):
        return jnp.maximum(i - 2, 0)

    def init_idx(i):
        return (0 if shared_init else proj_tile(i) // tps, 0, 0)

    kern = functools.partial(_mixer_long_kernel, tile=tile, base_pos=base_pos, layer=layer,
                             tiles_per_seq=tps, n_tiles=n_tiles)
    buffer_set = [
        pltpu.VMEM((tile // WINDOW, N_HEADS * WINDOW, LANES), BF16),
        pltpu.VMEM((WINDOW + tile, KV_WIDTH), BF16),
        pltpu.VMEM((WINDOW + tile, KV_WIDTH), BF16),
        pltpu.VMEM((tile, POOL_WIDTH), BF16),
        pltpu.VMEM((tile, D_MODEL), BF16),
    ]
    return pl.pallas_call(
        kern,
        grid=(n_tiles + 2,),
        in_specs=[
            pl.BlockSpec(memory_space=pltpu.SMEM),
            pl.BlockSpec((1, tile, D_MODEL), lambda i: (proj_tile(i) // tps, proj_tile(i) % tps, 0)),
            pl.BlockSpec((1, tile, D_MODEL), lambda i: (out_tile(i) // tps, out_tile(i) % tps, 0)),
            pl.BlockSpec((tile, 2 * LANES), lambda i: (proj_tile(i) % tps, 0)),
            pl.BlockSpec((1, WINDOW, KV_WIDTH), init_idx),
            pl.BlockSpec((1, WINDOW, KV_WIDTH), init_idx),
            pl.BlockSpec((1, HIST_ROWS, POOL_WIDTH), init_idx),
        ] + _mixer_weight_specs(layer),
        out_specs=[
            pl.BlockSpec((1, tile, D_MODEL), lambda i: (out_tile(i) // tps, out_tile(i) % tps, 0)),
            pl.BlockSpec((1, WINDOW, KV_WIDTH), lambda i: (proj_tile(i) // tps, 0, 0)),
            pl.BlockSpec((1, WINDOW, KV_WIDTH), lambda i: (proj_tile(i) // tps, 0, 0)),
            pl.BlockSpec((1, HIST_ROWS, POOL_WIDTH), lambda i: (proj_tile(i) // tps, 0, 0)),
        ],
        out_shape=[
            jax.ShapeDtypeStruct((n_seq, seq_len, D_MODEL), F32),
            jax.ShapeDtypeStruct((n_seq, WINDOW, KV_WIDTH), F32),
            jax.ShapeDtypeStruct((n_seq, WINDOW, KV_WIDTH), F32),
            jax.ShapeDtypeStruct((n_seq, HIST_ROWS, POOL_WIDTH), F32),
        ],
        scratch_shapes=buffer_set + buffer_set + [pltpu.VMEM((HIST_ROWS + tile, POOL_WIDTH), F32)],
        compiler_params=pltpu.CompilerParams(
            dimension_semantics=("arbitrary",), vmem_limit_bytes=VMEM_LIMIT),
        name=name,
    )(sinks, h, h, rope, kin, vin, uin, ln, win, pw, ps, wout)


def _mixer_short_kernel(sink_ref, h_ref, rope_ref, ck_ref, cv_ref, st_ref, ln_ref, win_ref,
                        pw_ref, ps_ref, wout_ref, *rest, seqs, layer):
    if layer:
        kprev_ref, vprev_ref, sprev_ref = rest[:3]
        rest = rest[3:]
    ho_ref, kc_ref, vc_ref, sn_ref, qbuf, abuf, ubuf, dbuf, ktbuf, vtbuf = rest
    if layer:
        kc_ref[0:layer] = kprev_ref[...]
        vc_ref[0:layer] = vprev_ref[...]
        sn_ref[0:layer] = sprev_ref[...]
    rows = seqs * DEC_SEQ
    h = h_ref[...]
    qb, k, v, u = _split_rope(_in_proj(h, ln_ref, win_ref), rope_ref[...])
    for j in range(Q_BLOCKS):
        qbuf[:, j * LANES:(j + 1) * LANES] = qb[j]
    kt, vt = k.T, v.T
    for b in range(rows // LANES):
        ktbuf[b] = kt[:, b * LANES:(b + 1) * LANES]
        vtbuf[b] = vt[:, b * LANES:(b + 1) * LANES]

    groups = len(POOL_WINDOWS)
    for g in range(groups):
        ubuf[g] = u[:, g * POOL_GROUP:(g + 1) * POOL_GROUP]
    new_slabs = [[ubuf[g, pl.ds(t, seqs, stride=DEC_SEQ), :] for g in range(groups)] for t in range(DEC_SEQ)]
    for r in range(POOL_HIST):
        src_row = r + DEC_SEQ
        sn_ref[layer, r] = (st_ref[src_row] if src_row < POOL_HIST
                            else jnp.concatenate(new_slabs[src_row - POOL_HIST], axis=-1))
    for g, w in enumerate(POOL_WINDOWS):
        cols = slice(g * POOL_GROUP, (g + 1) * POOL_GROUP)
        slabs = [st_ref[r, :, cols] for r in range(POOL_HIST)] + [new_slabs[t][g] for t in range(DEC_SEQ)]
        cnt = float(min(PAST_LEN + 1, w))
        for t in range(DEC_SEQ):
            wsum = slabs[POOL_HIST + t]
            for back in range(1, w):
                wsum = wsum + slabs[POOL_HIST + t - back]
            dbuf[g, pl.ds(t, seqs, stride=DEC_SEQ), :] = wsum / cnt - slabs[POOL_HIST + t]

    stacked = N_HEADS * DEC_SEQ
    kept = WINDOW - DEC_SEQ
    lane = lax.broadcasted_iota(jnp.int32, (1, LANES), 1)
    lo = lane < HEAD_DIM
    old = lane < kept
    srow = lax.broadcasted_iota(jnp.int32, (stacked, 2 * LANES), 0)
    kc = lax.broadcasted_iota(jnp.int32, (stacked, 2 * LANES), 1)
    qi = jnp.bitwise_and(srow, DEC_SEQ - 1)
    visible = ((kc < LANES) & (kc >= qi)) | ((kc >= LANES + kept) & (kc - (LANES + kept) <= qi))
    bias = jnp.where(visible, 0.0, NEG_INF)
    hrow = jnp.right_shift(lax.broadcasted_iota(jnp.int32, (stacked, 1), 0), DEC_SEQ.bit_length() - 1)
    sink = jnp.zeros((stacked, 1), F32)
    for hh in range(N_HEADS):
        sink = jnp.where(hrow == hh, sink_ref[layer, hh] * LOG2E, sink)

    def shifted(cache_ref, new_ref, gi, i, n):
        old_t = cache_ref[i]
        new_t = pltpu.roll(new_ref[gi], (kept - DEC_SEQ * n) % LANES, 1)
        return old_t, new_t, jnp.where(old, pltpu.roll(old_t, kept, 1), new_t)

    def group_body(gi, carry):
        ids = [gi * SAMPLE_GROUP + n for n in range(SAMPLE_GROUP)]
        starts = [pl.multiple_of(i * DEC_SEQ, DEC_SEQ) for i in ids]
        scores = []
        for n, (i, r0) in enumerate(zip(ids, starts)):
            qrows = qbuf[pl.ds(r0, DEC_SEQ), :]
            q = _stack_heads([qrows[:, j * LANES:(j + 1) * LANES] for j in range(Q_BLOCKS)], lo)
            old_t, new_t, kc_ref[layer, i] = shifted(ck_ref, ktbuf, gi, i, n)
            kall = jnp.concatenate([old_t, new_t], axis=1).astype(BF16)
            scores.append(jnp.dot(q, kall, preferred_element_type=F32))
        probs = [_softmax_sink(s + bias, sink) for s in scores]
        for n, (i, r0, (prob, inv)) in enumerate(zip(ids, starts, probs)):
            old_t, new_t, vc_ref[layer, i] = shifted(cv_ref, vtbuf, gi, i, n)
            vall = jnp.concatenate([old_t, new_t], axis=1).astype(BF16)
            o = lax.dot_general(prob, vall, (((1,), (1,)), ((), ())), preferred_element_type=F32) * inv
            abuf[pl.ds(r0, DEC_SEQ), :] = jnp.concatenate(
                [jnp.where(lo, o[j * DEC_SEQ:(j + 1) * DEC_SEQ],
                           o[(j + Q_BLOCKS) * DEC_SEQ:(j + Q_BLOCKS + 1) * DEC_SEQ])
                 for j in range(Q_BLOCKS)], axis=-1)
        return carry

    lax.fori_loop(0, seqs // SAMPLE_GROUP, group_body, 0)

    pm = _pool_project([dbuf[g] for g in range(groups)], pw_ref, ps_ref)

    mix = jnp.concatenate([abuf[...], pm], axis=-1).astype(BF16)
    ho_ref[...] = h + jnp.dot(mix, wout_ref[...], preferred_element_type=F32)


def _mixer_short(h, rope, ck, cv, st, sinks, ln, win, pw, ps, wout, kc_prev, vc_prev, sn_prev,
                 *, seqs, layer, name):
    n_rows = h.shape[0]
    rows = seqs * DEC_SEQ
    n_seq = n_rows // DEC_SEQ
    stacked_spec = pl.BlockSpec((layer + 1, seqs, KV_WIDTH, WINDOW), lambda i: (0, i, 0, 0))
    state_spec = pl.BlockSpec((layer + 1, POOL_HIST, seqs, POOL_WIDTH), lambda i: (0, 0, i, 0))
    prev_specs = [pl.BlockSpec((layer, seqs, KV_WIDTH, WINDOW), lambda i: (0, i, 0, 0))] * 2 + [
        pl.BlockSpec((layer, POOL_HIST, seqs, POOL_WIDTH), lambda i: (0, 0, i, 0))] if layer else []
    prev_args = (kc_prev, vc_prev, sn_prev) if layer else ()
    kern = functools.partial(_mixer_short_kernel, seqs=seqs, layer=layer)
    return pl.pallas_call(
        kern,
        grid=(n_rows // rows,),
        in_specs=[
            pl.BlockSpec(memory_space=pltpu.SMEM),
            pl.BlockSpec((rows, D_MODEL), lambda i: (i, 0)),
            _const_spec((rows, 2 * LANES)),
            pl.BlockSpec((None, seqs, KV_WIDTH, WINDOW), lambda i: (layer, i, 0, 0)),
            pl.BlockSpec((None, seqs, KV_WIDTH, WINDOW), lambda i: (layer, i, 0, 0)),
            pl.BlockSpec((None, POOL_HIST, seqs, POOL_WIDTH), lambda i: (layer, 0, i, 0)),
        ] + _mixer_weight_specs(layer) + prev_specs,
        out_specs=[
            pl.BlockSpec((rows, D_MODEL), lambda i: (i, 0)),
            stacked_spec,
            stacked_spec,
            state_spec,
        ],
        out_shape=[
            jax.ShapeDtypeStruct((n_rows, D_MODEL), F32),
            jax.ShapeDtypeStruct((layer + 1, n_seq, KV_WIDTH, WINDOW), F32),
            jax.ShapeDtypeStruct((layer + 1, n_seq, KV_WIDTH, WINDOW), F32),
            jax.ShapeDtypeStruct((layer + 1, POOL_HIST, n_seq, POOL_WIDTH), F32),
        ],
        scratch_shapes=[
            pltpu.VMEM((rows, ATTN_WIDTH), F32),
            pltpu.VMEM((rows, ATTN_WIDTH), F32),
            pltpu.VMEM((len(POOL_WINDOWS), rows, POOL_GROUP), F32),
            pltpu.VMEM((len(POOL_WINDOWS), rows, POOL_GROUP), F32),
            pltpu.VMEM((rows // LANES, KV_WIDTH, LANES), F32),
            pltpu.VMEM((rows // LANES, KV_WIDTH, LANES), F32),
        ],
        compiler_params=pltpu.CompilerParams(
            dimension_semantics=("arbitrary",), vmem_limit_bytes=VMEM_LIMIT),
        name=name,
    )(sinks, h, rope, ck, cv, st, ln, win, pw, ps, wout, *prev_args)


def _mlp_rows(h, ln_ref, wup_ref, wdown_ref, lnf_ref, final_norm):
    xn = _rmsnorm(h, ln_ref[...]).astype(BF16)

    def hidden(c):
        a = jnp.maximum(jnp.dot(xn, wup_ref[:, c * MLP_CHUNK:(c + 1) * MLP_CHUNK],
                                preferred_element_type=F32), 0.0)
        return (a * a).astype(BF16)

    def down(a, c):
        return jnp.dot(a, wdown_ref[c * MLP_CHUNK:(c + 1) * MLP_CHUNK, :], preferred_element_type=F32)

    out = h
    a_prev = hidden(0)
    for c in range(1, D_FF // MLP_CHUNK):
        a_next = hidden(c)
        out = out + down(a_prev, c - 1)
        a_prev = a_next
    out = out + down(a_prev, D_FF // MLP_CHUNK - 1)
    if final_norm:
        out = _rmsnorm(out, lnf_ref[...])
    return out


def _mlp_kernel(ln_ref, wup_ref, wdown_ref, lnf_ref, *refs, final_norm, step_ranges):
    n_groups = len(step_ranges)
    i = pl.program_id(0)
    for g, (first, last) in enumerate(step_ranges):
        h_ref, o_ref = refs[g], refs[n_groups + g]

        @pl.when((i >= first) & (i <= last))
        def _(h_ref=h_ref, o_ref=o_ref):
            o_ref[...] = _mlp_rows(h_ref[...], ln_ref, wup_ref, wdown_ref, lnf_ref, final_norm)


def _mlp(groups, ln, wup, wdown, lnf, *, final_norm, layer, name):
    tiles = [min(MLP_TILE if g == 0 else MLP_SMALL_TILE, h.shape[0]) for g, h in enumerate(groups)]
    steps = [h.shape[0] // t for h, t in zip(groups, tiles)]
    firsts = [sum(steps[:g]) for g in range(len(groups))]
    step_ranges = tuple((f, f + n - 1) for f, n in zip(firsts, steps))

    def row_spec(tile, first, n):
        return pl.BlockSpec((tile, D_MODEL), lambda i: (jnp.clip(i - first, 0, n - 1), 0))

    row_specs = [row_spec(t, f, n) for t, f, n in zip(tiles, firsts, steps)]
    kern = functools.partial(_mlp_kernel, final_norm=final_norm, step_ranges=step_ranges)
    return pl.pallas_call(
        kern,
        grid=(sum(steps),),
        in_specs=[
            _layer_spec((1, D_MODEL), layer),
            _layer_spec((D_MODEL, D_FF), layer),
            _layer_spec((D_FF, D_MODEL), layer),
            _const_spec((1, D_MODEL)),
        ] + row_specs,
        out_specs=row_specs,
        out_shape=[jax.ShapeDtypeStruct(h.shape, F32) for h in groups],
        compiler_params=pltpu.CompilerParams(
            dimension_semantics=("arbitrary",), vmem_limit_bytes=VMEM_LIMIT),
        name=name,
    )(ln, wup, wdown, lnf, *groups)


def _rope_table(pos):
    n = pos.shape[0]
    inv_freq = ROPE_THETA ** (-jnp.arange(0, ROT_DIM, 2, dtype=F32) / ROT_DIM)
    ang = pos.astype(F32)[:, None] * inv_freq[None, :]
    cos, sin = jnp.cos(ang), jnp.sin(ang)
    rest = HEAD_DIM - ROT_DIM
    c = jnp.concatenate([cos, cos, jnp.ones((n, rest), F32)], axis=-1)
    s = jnp.concatenate([-sin, sin, jnp.zeros((n, rest), F32)], axis=-1)
    reps = LANES // HEAD_DIM
    return jnp.concatenate([jnp.tile(c, (1, reps)), jnp.tile(s, (1, reps))], axis=-1)


def _pair_heads(w, axis):
    shape = w.shape
    split = shape[:axis] + (N_HEADS // Q_BLOCKS, Q_BLOCKS, HEAD_DIM) + shape[axis + 1:]
    return jnp.swapaxes(w.reshape(split), axis, axis + 1).reshape(shape)


def kernel(x_prompt, x_sample, cache_k, cache_v, state_pool, meta_tokens, ln1, w_in, attn_sinks,
           pool_w, pool_scale, w_out, ln2, w_up, w_down, ln_f):
    batch, seq, _ = x_prompt.shape
    dec_batch, dec_seq, _ = x_sample.shape
    depth = w_in.shape[0]
    assert dec_seq == DEC_SEQ and seq % PROMPT_TILE == 0 and dec_batch % SAMPLE_SEQS == 0
    assert SAMPLE_SEQS % SAMPLE_GROUP == 0 and PAST_LEN >= WINDOW

    pad = WINDOW - N_META
    rope_meta = _rope_table(jnp.arange(WINDOW) - pad)
    rope_prompt = _rope_table(N_META + jnp.arange(seq))
    rope_sample = jnp.tile(_rope_table(PAST_LEN + jnp.arange(DEC_SEQ)), (SAMPLE_SEQS, 1))

    win = jnp.concatenate([_pair_heads(w_in[:, :, :ATTN_WIDTH], 2), w_in[:, :, ATTN_WIDTH:]], axis=2).astype(BF16)
    wout = jnp.concatenate([_pair_heads(w_out[:, :ATTN_WIDTH], 1), w_out[:, ATTN_WIDTH:]], axis=1).astype(BF16)
    wup = w_up.astype(BF16)
    wdown = w_down.astype(BF16)
    pw = pool_w.astype(BF16)
    ps = pool_scale.reshape(depth, 1, POOL_WIDTH)
    l1 = ln1.reshape(depth, 1, D_MODEL)
    l2 = ln2.reshape(depth, 1, D_MODEL)
    lnf = ln_f.reshape(1, D_MODEL)
    sinks = attn_sinks.astype(F32)
    mixer_w = (sinks, l1, win, pw, ps, wout)

    hm = jnp.concatenate([jnp.zeros((pad, D_MODEL), F32), meta_tokens.astype(F32)], axis=0)[None]
    hp = x_prompt
    hs = x_sample.reshape(dec_batch * dec_seq, D_MODEL)
    zero_kv = jnp.zeros((1, WINDOW, KV_WIDTH), F32)
    zero_u = jnp.zeros((1, HIST_ROWS, POOL_WIDTH), F32)
    st = jnp.transpose(state_pool, (0, 2, 1, 3))
    ck = jnp.transpose(cache_k, (0, 1, 3, 4, 2)).reshape(depth, dec_batch, KV_WIDTH, WINDOW)
    cv = jnp.transpose(cache_v, (0, 1, 3, 4, 2)).reshape(depth, dec_batch, KV_WIDTH, WINDOW)

    pk, pv, pu = [], [], []
    kc = vc = sn = None
    for l in range(depth):
        last = l == depth - 1
        hm_mid, km, vm, um = _mixer_long(hm, rope_meta, zero_kv, zero_kv, zero_u, *mixer_w, tile=WINDOW,
                                         base_pos=-pad, layer=l, name=f"mixer_meta_{l}")
        hp_mid, kt, vt, ut = _mixer_long(hp, rope_prompt, km, vm, um, *mixer_w, tile=PROMPT_TILE,
                                         base_pos=N_META, layer=l, name=f"mixer_prompt_{l}")
        hs_mid, kc, vc, sn = _mixer_short(hs, rope_sample, ck, cv, st, *mixer_w, kc, vc, sn, seqs=SAMPLE_SEQS,
                                          layer=l, name=f"mixer_sample_{l}")

        groups = [hp_mid.reshape(batch * seq, D_MODEL), hs_mid] + ([] if last else [hm_mid[0]])
        outs = _mlp(groups, l2, wup, wdown, lnf, final_norm=last, layer=l, name=f"mlp_{l}")
        hp = outs[0].reshape(batch, seq, D_MODEL)
        hs = outs[1]
        if not last:
            hm = outs[2][None]

        pk.append(kt.reshape(batch, WINDOW, N_KV_HEADS, HEAD_DIM))
        pv.append(vt.reshape(batch, WINDOW, N_KV_HEADS, HEAD_DIM))
        pu.append(ut[:, 1:, :])

    y_sample = hs.reshape(dec_batch, dec_seq, D_MODEL)
    def window_major(c):
        return jnp.transpose(c.reshape(depth, dec_batch, N_KV_HEADS, HEAD_DIM, WINDOW), (0, 1, 4, 2, 3))

    return (hp, y_sample, jnp.stack(pk), jnp.stack(pv), jnp.stack(pu),
            window_major(kc), window_major(vc), jnp.transpose(sn, (0, 2, 1, 3)))
```

```python
import functools

import jax
import jax.numpy as jnp
from jax import lax
from jax.experimental import pallas as pl
from jax.experimental.pallas import tpu as pltpu

D_MODEL = 1024
N_HEADS = 8
N_KV_HEADS = 2
HEAD_DIM = 64
ATTN_WIDTH = N_HEADS * HEAD_DIM
KV_WIDTH = N_KV_HEADS * HEAD_DIM
POOL_WINDOWS = (2, 4, 8, 16)
POOL_WIDTH = D_MODEL - ATTN_WIDTH
POOL_GROUP = POOL_WIDTH // len(POOL_WINDOWS)
POOL_HIST = max(POOL_WINDOWS) - 1
IN_WIDTH = ATTN_WIDTH + 2 * KV_WIDTH + POOL_WIDTH
WINDOW = 128
ROT_DIM = HEAD_DIM // 4
ROPE_THETA = 500000.0
D_FF = 4 * D_MODEL
N_META = 16
RMS_EPS = 1e-5
PAST_LEN = 16384
DEC_SEQ = 8

LANES = 128
SAMPLE_GROUP = LANES // DEC_SEQ
HIST_ROWS = POOL_HIST + 1
Q_BLOCKS = ATTN_WIDTH // LANES
LOG2E = 1.4426950408889634
Q_SCALE = HEAD_DIM ** -0.5 * LOG2E
NEG_INF = float("-inf")

PROMPT_TILE = 512
SAMPLE_SEQS = 32
MLP_TILE = 1024
MLP_SMALL_TILE = 256
MLP_CHUNK = 1024
VMEM_LIMIT = 56 * 1024 * 1024

F32 = jnp.float32
BF16 = jnp.bfloat16


def _rmsnorm(x, g):
    r = lax.rsqrt(jnp.mean(x * x, axis=-1, keepdims=True) + RMS_EPS)
    return x * r * g


def _rope(x, rope):
    c = rope[:, 0:LANES]
    s = rope[:, LANES:2 * LANES]
    half = ROT_DIM // 2
    lane = lax.broadcasted_iota(jnp.int32, (1, LANES), 1)
    on_x1 = jnp.bitwise_and(lane, HEAD_DIM - 1) < half
    partner = jnp.where(on_x1, pltpu.roll(x, LANES - half, 1), pltpu.roll(x, half, 1))
    return x * c + partner * s


def _in_proj(h, ln_ref, win_ref):
    xn = _rmsnorm(h, ln_ref[...]).astype(BF16)
    return jnp.dot(xn, win_ref[...], preferred_element_type=F32)


def _split_rope(p, rope):
    qb = [_rope(p[:, j * LANES:(j + 1) * LANES], rope) * Q_SCALE for j in range(Q_BLOCKS)]
    k = _rope(p[:, ATTN_WIDTH:ATTN_WIDTH + KV_WIDTH], rope)
    v = p[:, ATTN_WIDTH + KV_WIDTH:ATTN_WIDTH + 2 * KV_WIDTH]
    u = p[:, ATTN_WIDTH + 2 * KV_WIDTH:]
    return qb, k, v, u


def _stack_heads(qrows, lo):
    zero = jnp.zeros_like(qrows[0])
    parts = [jnp.where(lo, q, zero) for q in qrows] + [jnp.where(lo, zero, q) for q in qrows]
    return jnp.concatenate(parts, axis=0).astype(BF16)


def _softmax_sink(s, sink):
    m = jnp.maximum(jnp.max(s, axis=-1, keepdims=True), sink)
    e = jnp.exp2(s - m)
    l = jnp.sum(e, axis=-1, keepdims=True) + jnp.exp2(sink - m)
    return e.astype(BF16), 1.0 / l


def _window_sum(x, w, axis):
    span = 1
    while span < w:
        x = x + pltpu.roll(x, span, axis)
        span *= 2
    return x


def _pool_project(d_groups, pw_ref, ps_ref):
    outs = [jnp.dot(d.astype(BF16), pw_ref[g], preferred_element_type=F32)
            for g, d in enumerate(d_groups)]
    return jnp.concatenate(outs, axis=-1) * ps_ref[...]


def _project_finish(p, rope_ref, kin_ref, vin_ref, uin_ref, pw_ref, ps_ref,
                    kt_ref, vt_ref, ut_ref, ubuf, dst, src, *, tile, first, p0):
    dq, dk, dvt, dm, _ = dst
    _, sk, svt, _, _ = src
    qb, k, v, u = _split_rope(p, rope_ref[...])
    lo = lax.broadcasted_iota(jnp.int32, (1, LANES), 1) < HEAD_DIM
    for r in range(tile // WINDOW):
        rows = slice(r * WINDOW, (r + 1) * WINDOW)
        dq[r] = _stack_heads([qb[j][rows] for j in range(Q_BLOCKS)], lo)
    dk[0:WINDOW, :] = jnp.where(first, kin_ref[0].astype(BF16), sk[tile:tile + WINDOW, :])
    dvt[:, 0:WINDOW] = jnp.where(first, vin_ref[0].T.astype(BF16), svt[:, tile:tile + WINDOW])
    dk[WINDOW:WINDOW + tile, :] = k.astype(BF16)
    dvt[:, WINDOW:WINDOW + tile] = v.T.astype(BF16)
    kt_ref[0] = k[tile - WINDOW:, :]
    vt_ref[0] = v[tile - WINDOW:, :]
    ut_ref[0] = u[tile - HIST_ROWS:, :]

    ubuf[0:HIST_ROWS, :] = jnp.where(first, uin_ref[0], ubuf[tile:tile + HIST_ROWS, :])
    ubuf[HIST_ROWS:HIST_ROWS + tile, :] = u
    pos = p0 + lax.broadcasted_iota(jnp.int32, (tile, 1), 0)
    d_groups = []
    for g, w in enumerate(POOL_WINDOWS):
        cols = slice(g * POOL_GROUP, (g + 1) * POOL_GROUP)
        wsum = _window_sum(ubuf[:, cols], w, 0)[HIST_ROWS:]
        cnt = jnp.clip(pos + 1, 1, w).astype(F32)
        d_groups.append(wsum / cnt - u[:, cols])
    dm[...] = _pool_project(d_groups, pw_ref, ps_ref).astype(BF16)


def _attend_scores(src, *, tile):
    sq, sk, _, _, _ = src
    return [lax.dot_general(sk[r * WINDOW:(r + 2) * WINDOW, :], sq[r], (((1,), (1,)), ((), ())),
                            preferred_element_type=F32) for r in range(tile // WINDOW)]


def _softmax_sink_keys_major(s, sink):
    m = jnp.maximum(jnp.max(s, axis=0, keepdims=True), sink)
    e = jnp.exp2(s - m)
    l = jnp.sum(e, axis=0, keepdims=True) + jnp.exp2(sink - m)
    return e.astype(BF16), 1.0 / l


def _attend_values(scores, sink_ref, src, *, layer, p0):
    _, _, svt, _, _ = src
    first_kv = lax.broadcasted_iota(jnp.int32, (KV_WIDTH, 1), 0) < HEAD_DIM
    kc = lax.broadcasted_iota(jnp.int32, (2 * WINDOW, WINDOW), 0)
    qi = lax.broadcasted_iota(jnp.int32, (2 * WINDOW, WINDOW), 1)
    band = (kc >= qi) & (kc <= qi + WINDOW)
    attn_rows = []
    for r, s in enumerate(scores):
        kpos = p0 + (r - 1) * WINDOW + kc
        bias = jnp.where(band & (kpos >= 0), 0.0, NEG_INF)
        probs, inv = zip(*[_softmax_sink_keys_major(s[:, hh * WINDOW:(hh + 1) * WINDOW] + bias,
                                                    sink_ref[layer, hh] * LOG2E) for hh in range(N_HEADS)])
        o = jnp.dot(svt[:, r * WINDOW:(r + 2) * WINDOW], jnp.concatenate(probs, axis=1),
                    preferred_element_type=F32) * jnp.concatenate(inv, axis=1)
        attn_rows.append(jnp.concatenate(
            [jnp.where(first_kv, o[:, j * WINDOW:(j + 1) * WINDOW],
                       o[:, (j + Q_BLOCKS) * WINDOW:(j + Q_BLOCKS + 1) * WINDOW]).T
             for j in range(Q_BLOCKS)], axis=-1))
    return jnp.concatenate(attn_rows, axis=0).astype(BF16)


def _mixer_long_kernel(sink_ref, hp_ref, hr_ref, rope_ref, kin_ref, vin_ref, uin_ref, ln_ref, win_ref,
                       pw_ref, ps_ref, wout_ref, ho_ref, kt_ref, vt_ref, ut_ref,
                       qa, ka, va, ma, xa, qb, kb, vb, mb, xb, ubuf,
                       *, tile, base_pos, layer, tiles_per_seq, n_tiles):
    i = pl.program_id(0)
    tp = lax.rem(jnp.minimum(i, n_tiles - 1), tiles_per_seq)
    ts = lax.rem(jnp.clip(i - 1, 0, n_tiles - 1), tiles_per_seq)
    set_a, set_b = (qa, ka, va, ma, xa), (qb, kb, vb, mb, xb)

    @pl.when(i == 0)
    def _():
        for ref in (kb, vb, ubuf):
            ref[...] = jnp.zeros(ref.shape, ref.dtype)

    def step(dst, src, project, attend, output):
        if attend:
            scores = _attend_scores(src, tile=tile)
        if project:
            p = _in_proj(hp_ref[0], ln_ref, win_ref)
        if output:
            ho_ref[0] = hr_ref[0] + jnp.dot(dst[4][...], wout_ref[...], preferred_element_type=F32)
        if attend:
            attn = _attend_values(scores, sink_ref, src, layer=layer, p0=base_pos + ts * tile)
            src[4][...] = jnp.concatenate([attn, src[3][...]], axis=-1)
        if project:
            _project_finish(p, rope_ref, kin_ref, vin_ref, uin_ref, pw_ref, ps_ref, kt_ref, vt_ref, ut_ref,
                            ubuf, dst, src, tile=tile, first=tp == 0, p0=base_pos + tp * tile)

    variants = {}
    for idx in range(n_tiles + 2):
        key = (idx < n_tiles, 1 <= idx <= n_tiles, idx >= 2, idx % 2)
        variants.setdefault(key, []).append(idx)
    parity = lax.rem(i, 2)
    for (project, attend, output, par), steps in variants.items():
        dst, src = (set_a, set_b) if par == 0 else (set_b, set_a)
        cond = (i >= steps[0]) & (i <= steps[-1]) & (parity == par)
        pl.when(cond)(functools.partial(step, dst, src, project, attend, output))


def _const_spec(shape):
    nd = len(shape)
    return pl.BlockSpec(shape, lambda *_: (0,) * nd, pipeline_mode=pl.Buffered(1))


def _layer_spec(shape, layer):
    nd = len(shape)
    return pl.BlockSpec((None,) + shape, lambda *_: (layer,) + (0,) * nd, pipeline_mode=pl.Buffered(1))


def _mixer_weight_specs(layer):
    return [
        _layer_spec((1, D_MODEL), layer),
        _layer_spec((D_MODEL, IN_WIDTH), layer),
        _layer_spec((len(POOL_WINDOWS), POOL_GROUP, POOL_GROUP), layer),
        _layer_spec((1, POOL_WIDTH), layer),
        _layer_spec((D_MODEL, D_MODEL), layer),
    ]


def _mixer_long(h, rope, kin, vin, uin, sinks, ln, win, pw, ps, wout, *, tile, base_pos, layer, name):
    n_seq, seq_len, _ = h.shape
    tps = seq_len // tile
    n_tiles = n_seq * tps
    shared_init = kin.shape[0] == 1

    def proj_tile(i):
        return jnp.minimum(i, n_tiles - 1)

    def out_tile(i):
        return jnp.maximum(i - 2, 0)

    def init_idx(i):
        return (0 if shared_init else proj_tile(i) // tps, 0, 0)

    kern = functools.partial(_mixer_long_kernel, tile=tile, base_pos=base_pos, layer=layer,
                             tiles_per_seq=tps, n_tiles=n_tiles)
    buffer_set = [
        pltpu.VMEM((tile // WINDOW, N_HEADS * WINDOW, LANES), BF16),
        pltpu.VMEM((WINDOW + tile, KV_WIDTH), BF16),
        pltpu.VMEM((KV_WIDTH, WINDOW + tile), BF16),
        pltpu.VMEM((tile, POOL_WIDTH), BF16),
        pltpu.VMEM((tile, D_MODEL), BF16),
    ]
    return pl.pallas_call(
        kern,
        grid=(n_tiles + 2,),
        in_specs=[
            pl.BlockSpec(memory_space=pltpu.SMEM),
            pl.BlockSpec((1, tile, D_MODEL), lambda i: (proj_tile(i) // tps, proj_tile(i) % tps, 0)),
            pl.BlockSpec((1, tile, D_MODEL), lambda i: (out_tile(i) // tps, out_tile(i) % tps, 0)),
            pl.BlockSpec((tile, 2 * LANES), lambda i: (proj_tile(i) % tps, 0)),
            pl.BlockSpec((1, WINDOW, KV_WIDTH), init_idx),
            pl.BlockSpec((1, WINDOW, KV_WIDTH), init_idx),
            pl.BlockSpec((1, HIST_ROWS, POOL_WIDTH), init_idx),
        ] + _mixer_weight_specs(layer),
        out_specs=[
            pl.BlockSpec((1, tile, D_MODEL), lambda i: (out_tile(i) // tps, out_tile(i) % tps, 0)),
            pl.BlockSpec((1, WINDOW, KV_WIDTH), lambda i: (proj_tile(i) // tps, 0, 0)),
            pl.BlockSpec((1, WINDOW, KV_WIDTH), lambda i: (proj_tile(i) // tps, 0, 0)),
            pl.BlockSpec((1, HIST_ROWS, POOL_WIDTH), lambda i: (proj_tile(i) // tps, 0, 0)),
        ],
        out_shape=[
            jax.ShapeDtypeStruct((n_seq, seq_len, D_MODEL), F32),
            jax.ShapeDtypeStruct((n_seq, WINDOW, KV_WIDTH), F32),
            jax.ShapeDtypeStruct((n_seq, WINDOW, KV_WIDTH), F32),
            jax.ShapeDtypeStruct((n_seq, HIST_ROWS, POOL_WIDTH), F32),
        ],
        scratch_shapes=buffer_set + buffer_set + [pltpu.VMEM((HIST_ROWS + tile, POOL_WIDTH), F32)],
        compiler_params=pltpu.CompilerParams(
            dimension_semantics=("arbitrary",), vmem_limit_bytes=VMEM_LIMIT),
        name=name,
    )(sinks, h, h, rope, kin, vin, uin, ln, win, pw, ps, wout)


def _mixer_short_kernel(sink_ref, h_ref, rope_ref, ck_ref, cv_ref, st_ref, ln_ref, win_ref,
                        pw_ref, ps_ref, wout_ref, *rest, seqs, layer):
    if layer:
        kprev_ref, vprev_ref, sprev_ref = rest[:3]
        rest = rest[3:]
    ho_ref, kc_ref, vc_ref, sn_ref, qbuf, abuf, ubuf, dbuf, ktbuf, vtbuf = rest
    if layer:
        kc_ref[0:layer] = kprev_ref[...]
        vc_ref[0:layer] = vprev_ref[...]
        sn_ref[0:layer] = sprev_ref[...]
    rows = seqs * DEC_SEQ
    h = h_ref[...]
    qb, k, v, u = _split_rope(_in_proj(h, ln_ref, win_ref), rope_ref[...])
    for j in range(Q_BLOCKS):
        qbuf[:, j * LANES:(j + 1) * LANES] = qb[j]
    kt, vt = k.T, v.T
    for b in range(rows // LANES):
        ktbuf[b] = kt[:, b * LANES:(b + 1) * LANES]
        vtbuf[b] = vt[:, b * LANES:(b + 1) * LANES]

    groups = len(POOL_WINDOWS)
    for g in range(groups):
        ubuf[g] = u[:, g * POOL_GROUP:(g + 1) * POOL_GROUP]
    new_slabs = [[ubuf[g, pl.ds(t, seqs, stride=DEC_SEQ), :] for g in range(groups)] for t in range(DEC_SEQ)]
    for r in range(POOL_HIST):
        src_row = r + DEC_SEQ
        sn_ref[layer, r] = (st_ref[src_row] if src_row < POOL_HIST
                            else jnp.concatenate(new_slabs[src_row - POOL_HIST], axis=-1))
    for g, w in enumerate(POOL_WINDOWS):
        cols = slice(g * POOL_GROUP, (g + 1) * POOL_GROUP)
        slabs = [st_ref[r, :, cols] for r in range(POOL_HIST)] + [new_slabs[t][g] for t in range(DEC_SEQ)]
        cnt = float(min(PAST_LEN + 1, w))
        for t in range(DEC_SEQ):
            wsum = slabs[POOL_HIST + t]
            for back in range(1, w):
                wsum = wsum + slabs[POOL_HIST + t - back]
            dbuf[g, pl.ds(t, seqs, stride=DEC_SEQ), :] = wsum / cnt - slabs[POOL_HIST + t]

    stacked = N_HEADS * DEC_SEQ
    kept = WINDOW - DEC_SEQ
    lane = lax.broadcasted_iota(jnp.int32, (1, LANES), 1)
    lo = lane < HEAD_DIM
    old = lane < kept
    srow = lax.broadcasted_iota(jnp.int32, (stacked, 2 * LANES), 0)
    kc = lax.broadcasted_iota(jnp.int32, (stacked, 2 * LANES), 1)
    qi = jnp.bitwise_and(srow, DEC_SEQ - 1)
    visible = ((kc < LANES) & (kc >= qi)) | ((kc >= LANES + kept) & (kc - (LANES + kept) <= qi))
    bias = jnp.where(visible, 0.0, NEG_INF)
    hrow = jnp.right_shift(lax.broadcasted_iota(jnp.int32, (stacked, 1), 0), DEC_SEQ.bit_length() - 1)
    sink = jnp.zeros((stacked, 1), F32)
    for hh in range(N_HEADS):
        sink = jnp.where(hrow == hh, sink_ref[layer, hh] * LOG2E, sink)

    def shifted(cache_ref, new_ref, gi, i, n):
        old_t = cache_ref[i]
        new_t = pltpu.roll(new_ref[gi], (kept - DEC_SEQ * n) % LANES, 1)
        return old_t, new_t, jnp.where(old, pltpu.roll(old_t, kept, 1), new_t)

    def group_body(gi, carry):
        ids = [gi * SAMPLE_GROUP + n for n in range(SAMPLE_GROUP)]
        starts = [pl.multiple_of(i * DEC_SEQ, DEC_SEQ) for i in ids]
        scores = []
        for n, (i, r0) in enumerate(zip(ids, starts)):
            qrows = qbuf[pl.ds(r0, DEC_SEQ), :]
            q = _stack_heads([qrows[:, j * LANES:(j + 1) * LANES] for j in range(Q_BLOCKS)], lo)
            old_t, new_t, kc_ref[layer, i] = shifted(ck_ref, ktbuf, gi, i, n)
            kall = jnp.concatenate([old_t, new_t], axis=1).astype(BF16)
            scores.append(jnp.dot(q, kall, preferred_element_type=F32))
        probs = [_softmax_sink(s + bias, sink) for s in scores]
        for n, (i, r0, (prob, inv)) in enumerate(zip(ids, starts, probs)):
            old_t, new_t, vc_ref[layer, i] = shifted(cv_ref, vtbuf, gi, i, n)
            vall = jnp.concatenate([old_t, new_t], axis=1).astype(BF16)
            o = lax.dot_general(prob, vall, (((1,), (1,)), ((), ())), preferred_element_type=F32) * inv
            abuf[pl.ds(r0, DEC_SEQ), :] = jnp.concatenate(
                [jnp.where(lo, o[j * DEC_SEQ:(j + 1) * DEC_SEQ],
                           o[(j + Q_BLOCKS) * DEC_SEQ:(j + Q_BLOCKS + 1) * DEC_SEQ])
                 for j in range(Q_BLOCKS)], axis=-1)
        return carry

    lax.fori_loop(0, seqs // SAMPLE_GROUP, group_body, 0)

    pm = _pool_project([dbuf[g] for g in range(groups)], pw_ref, ps_ref)

    mix = jnp.concatenate([abuf[...], pm], axis=-1).astype(BF16)
    ho_ref[...] = h + jnp.dot(mix, wout_ref[...], preferred_element_type=F32)


def _mixer_short(h, rope, ck, cv, st, sinks, ln, win, pw, ps, wout, kc_prev, vc_prev, sn_prev,
                 *, seqs, layer, name):
    n_rows = h.shape[0]
    rows = seqs * DEC_SEQ
    n_seq = n_rows // DEC_SEQ
    stacked_spec = pl.BlockSpec((layer + 1, seqs, KV_WIDTH, WINDOW), lambda i: (0, i, 0, 0))
    state_spec = pl.BlockSpec((layer + 1, POOL_HIST, seqs, POOL_WIDTH), lambda i: (0, 0, i, 0))
    prev_specs = [pl.BlockSpec((layer, seqs, KV_WIDTH, WINDOW), lambda i: (0, i, 0, 0))] * 2 + [
        pl.BlockSpec((layer, POOL_HIST, seqs, POOL_WIDTH), lambda i: (0, 0, i, 0))] if layer else []
    prev_args = (kc_prev, vc_prev, sn_prev) if layer else ()
    kern = functools.partial(_mixer_short_kernel, seqs=seqs, layer=layer)
    return pl.pallas_call(
        kern,
        grid=(n_rows // rows,),
        in_specs=[
            pl.BlockSpec(memory_space=pltpu.SMEM),
            pl.BlockSpec((rows, D_MODEL), lambda i: (i, 0)),
            _const_spec((rows, 2 * LANES)),
            pl.BlockSpec((None, seqs, KV_WIDTH, WINDOW), lambda i: (layer, i, 0, 0)),
            pl.BlockSpec((None, seqs, KV_WIDTH, WINDOW), lambda i: (layer, i, 0, 0)),
            pl.BlockSpec((None, POOL_HIST, seqs, POOL_WIDTH), lambda i: (layer, 0, i, 0)),
        ] + _mixer_weight_specs(layer) + prev_specs,
        out_specs=[
            pl.BlockSpec((rows, D_MODEL), lambda i: (i, 0)),
            stacked_spec,
            stacked_spec,
            state_spec,
        ],
        out_shape=[
            jax.ShapeDtypeStruct((n_rows, D_MODEL), F32),
            jax.ShapeDtypeStruct((layer + 1, n_seq, KV_WIDTH, WINDOW), F32),
            jax.ShapeDtypeStruct((layer + 1, n_seq, KV_WIDTH, WINDOW), F32),
            jax.ShapeDtypeStruct((layer + 1, POOL_HIST, n_seq, POOL_WIDTH), F32),
        ],
        scratch_shapes=[
            pltpu.VMEM((rows, ATTN_WIDTH), F32),
            pltpu.VMEM((rows, ATTN_WIDTH), F32),
            pltpu.VMEM((len(POOL_WINDOWS), rows, POOL_GROUP), F32),
            pltpu.VMEM((len(POOL_WINDOWS), rows, POOL_GROUP), F32),
            pltpu.VMEM((rows // LANES, KV_WIDTH, LANES), F32),
            pltpu.VMEM((rows // LANES, KV_WIDTH, LANES), F32),
        ],
        compiler_params=pltpu.CompilerParams(
            dimension_semantics=("arbitrary",), vmem_limit_bytes=VMEM_LIMIT),
        name=name,
    )(sinks, h, rope, ck, cv, st, ln, win, pw, ps, wout, *prev_args)


def _mlp_rows(h, ln_ref, wup_ref, wdown_ref, lnf_ref, final_norm):
    xn = _rmsnorm(h, ln_ref[...]).astype(BF16)

    def hidden(c):
        a = jnp.maximum(jnp.dot(xn, wup_ref[:, c * MLP_CHUNK:(c + 1) * MLP_CHUNK],
                                preferred_element_type=F32), 0.0)
        return (a * a).astype(BF16)

    def down(a, c):
        return jnp.dot(a, wdown_ref[c * MLP_CHUNK:(c + 1) * MLP_CHUNK, :], preferred_element_type=F32)

    out = h
    a_prev = hidden(0)
    for c in range(1, D_FF // MLP_CHUNK):
        a_next = hidden(c)
        out = out + down(a_prev, c - 1)
        a_prev = a_next
    out = out + down(a_prev, D_FF // MLP_CHUNK - 1)
    if final_norm:
        out = _rmsnorm(out, lnf_ref[...])
    return out


def _mlp_kernel(ln_ref, wup_ref, wdown_ref, lnf_ref, *refs, final_norm, step_ranges):
    n_groups = len(step_ranges)
    i = pl.program_id(0)
    for g, (first, last) in enumerate(step_ranges):
        h_ref, o_ref = refs[g], refs[n_groups + g]

        @pl.when((i >= first) & (i <= last))
        def _(h_ref=h_ref, o_ref=o_ref):
            o_ref[...] = _mlp_rows(h_ref[...], ln_ref, wup_ref, wdown_ref, lnf_ref, final_norm)


def _mlp(groups, ln, wup, wdown, lnf, *, final_norm, layer, name):
    tiles = [min(MLP_TILE if g == 0 else MLP_SMALL_TILE, h.shape[0]) for g, h in enumerate(groups)]
    steps = [h.shape[0] // t for h, t in zip(groups, tiles)]
    firsts = [sum(steps[:g]) for g in range(len(groups))]
    step_ranges = tuple((f, f + n - 1) for f, n in zip(firsts, steps))

    def row_spec(tile, first, n):
        return pl.BlockSpec((tile, D_MODEL), lambda i: (jnp.clip(i - first, 0, n - 1), 0))

    row_specs = [row_spec(t, f, n) for t, f, n in zip(tiles, firsts, steps)]
    kern = functools.partial(_mlp_kernel, final_norm=final_norm, step_ranges=step_ranges)
    return pl.pallas_call(
        kern,
        grid=(sum(steps),),
        in_specs=[
            _layer_spec((1, D_MODEL), layer),
            _layer_spec((D_MODEL, D_FF), layer),
            _layer_spec((D_FF, D_MODEL), layer),
            _const_spec((1, D_MODEL)),
        ] + row_specs,
        out_specs=row_specs,
        out_shape=[jax.ShapeDtypeStruct(h.shape, F32) for h in groups],
        compiler_params=pltpu.CompilerParams(
            dimension_semantics=("arbitrary",), vmem_limit_bytes=VMEM_LIMIT),
        name=name,
    )(ln, wup, wdown, lnf, *groups)


def _rope_table(pos):
    n = pos.shape[0]
    inv_freq = ROPE_THETA ** (-jnp.arange(0, ROT_DIM, 2, dtype=F32) / ROT_DIM)
    ang = pos.astype(F32)[:, None] * inv_freq[None, :]
    cos, sin = jnp.cos(ang), jnp.sin(ang)
    rest = HEAD_DIM - ROT_DIM
    c = jnp.concatenate([cos, cos, jnp.ones((n, rest), F32)], axis=-1)
    s = jnp.concatenate([-sin, sin, jnp.zeros((n, rest), F32)], axis=-1)
    reps = LANES // HEAD_DIM
    return jnp.concatenate([jnp.tile(c, (1, reps)), jnp.tile(s, (1, reps))], axis=-1)


def _pair_heads(w, axis):
    shape = w.shape
    split = shape[:axis] + (N_HEADS // Q_BLOCKS, Q_BLOCKS, HEAD_DIM) + shape[axis + 1:]
    return jnp.swapaxes(w.reshape(split), axis, axis + 1).reshape(shape)


def kernel(x_prompt, x_sample, cache_k, cache_v, state_pool, meta_tokens, ln1, w_in, attn_sinks,
           pool_w, pool_scale, w_out, ln2, w_up, w_down, ln_f):
    batch, seq, _ = x_prompt.shape
    dec_batch, dec_seq, _ = x_sample.shape
    depth = w_in.shape[0]
    assert dec_seq == DEC_SEQ and seq % PROMPT_TILE == 0 and dec_batch % SAMPLE_SEQS == 0
    assert SAMPLE_SEQS % SAMPLE_GROUP == 0 and PAST_LEN >= WINDOW

    pad = WINDOW - N_META
    rope_meta = _rope_table(jnp.arange(WINDOW) - pad)
    rope_prompt = _rope_table(N_META + jnp.arange(seq))
    rope_sample = jnp.tile(_rope_table(PAST_LEN + jnp.arange(DEC_SEQ)), (SAMPLE_SEQS, 1))

    win = jnp.concatenate([_pair_heads(w_in[:, :, :ATTN_WIDTH], 2), w_in[:, :, ATTN_WIDTH:]], axis=2).astype(BF16)
    wout = jnp.concatenate([_pair_heads(w_out[:, :ATTN_WIDTH], 1), w_out[:, ATTN_WIDTH:]], axis=1).astype(BF16)
    wup = w_up.astype(BF16)
    wdown = w_down.astype(BF16)
    pw = pool_w.astype(BF16)
    ps = pool_scale.reshape(depth, 1, POOL_WIDTH)
    l1 = ln1.reshape(depth, 1, D_MODEL)
    l2 = ln2.reshape(depth, 1, D_MODEL)
    lnf = ln_f.reshape(1, D_MODEL)
    sinks = attn_sinks.astype(F32)
    mixer_w = (sinks, l1, win, pw, ps, wout)

    hm = jnp.concatenate([jnp.zeros((pad, D_MODEL), F32), meta_tokens.astype(F32)], axis=0)[None]
    hp = x_prompt
    hs = x_sample.reshape(dec_batch * dec_seq, D_MODEL)
    zero_kv = jnp.zeros((1, WINDOW, KV_WIDTH), F32)
    zero_u = jnp.zeros((1, HIST_ROWS, POOL_WIDTH), F32)
    st = jnp.transpose(state_pool, (0, 2, 1, 3))
    ck = jnp.transpose(cache_k, (0, 1, 3, 4, 2)).reshape(depth, dec_batch, KV_WIDTH, WINDOW)
    cv = jnp.transpose(cache_v, (0, 1, 3, 4, 2)).reshape(depth, dec_batch, KV_WIDTH, WINDOW)

    pk, pv, pu = [], [], []
    kc = vc = sn = None
    for l in range(depth):
        last = l == depth - 1
        hm_mid, km, vm, um = _mixer_long(hm, rope_meta, zero_kv, zero_kv, zero_u, *mixer_w, tile=WINDOW,
                                         base_pos=-pad, layer=l, name=f"mixer_meta_{l}")
        hp_mid, kt, vt, ut = _mixer_long(hp, rope_prompt, km, vm, um, *mixer_w, tile=PROMPT_TILE,
                                         base_pos=N_META, layer=l, name=f"mixer_prompt_{l}")
        hs_mid, kc, vc, sn = _mixer_short(hs, rope_sample, ck, cv, st, *mixer_w, kc, vc, sn, seqs=SAMPLE_SEQS,
                                          layer=l, name=f"mixer_sample_{l}")

        groups = [hp_mid.reshape(batch * seq, D_MODEL), hs_mid] + ([] if last else [hm_mid[0]])
        outs = _mlp(groups, l2, wup, wdown, lnf, final_norm=last, layer=l, name=f"mlp_{l}")
        hp = outs[0].reshape(batch, seq, D_MODEL)
        hs = outs[1]
        if not last:
            hm = outs[2][None]

        pk.append(kt.reshape(batch, WINDOW, N_KV_HEADS, HEAD_DIM))
        pv.append(vt.reshape(batch, WINDOW, N_KV_HEADS, HEAD_DIM))
        pu.append(ut[:, 1:, :])

    y_sample = hs.reshape(dec_batch, dec_seq, D_MODEL)
    def window_major(c):
        return jnp.transpose(c.reshape(depth, dec_batch, N_KV_HEADS, HEAD_DIM, WINDOW), (0, 1, 4, 2, 3))

    return (hp, y_sample, jnp.stack(pk), jnp.stack(pv), jnp.stack(pu),
            window_major(kc), window_major(vc), jnp.transpose(sn, (0, 2, 1, 3)))
```

```python
import functools

import jax
import jax.numpy as jnp
from jax import lax
from jax.experimental import pallas as pl
from jax.experimental.pallas import tpu as pltpu

D_MODEL = 1024
N_HEADS = 8
N_KV_HEADS = 2
HEAD_DIM = 64
ATTN_WIDTH = N_HEADS * HEAD_DIM
KV_WIDTH = N_KV_HEADS * HEAD_DIM
POOL_WINDOWS = (2, 4, 8, 16)
POOL_WIDTH = D_MODEL - ATTN_WIDTH
POOL_GROUP = POOL_WIDTH // len(POOL_WINDOWS)
POOL_HIST = max(POOL_WINDOWS) - 1
IN_WIDTH = ATTN_WIDTH + 2 * KV_WIDTH + POOL_WIDTH
WINDOW = 128
ROT_DIM = HEAD_DIM // 4
ROPE_THETA = 500000.0
D_FF = 4 * D_MODEL
N_META = 16
RMS_EPS = 1e-5
PAST_LEN = 16384
DEC_SEQ = 8

LANES = 128
SAMPLE_GROUP = LANES // DEC_SEQ
HIST_ROWS = POOL_HIST + 1
Q_BLOCKS = ATTN_WIDTH // LANES
LOG2E = 1.4426950408889634
Q_SCALE = HEAD_DIM ** -0.5 * LOG2E
NEG_INF = float("-inf")

PROMPT_TILE = 512
SAMPLE_SEQS = 32
MLP_TILE = 512
MLP_CHUNK = 1024
VMEM_LIMIT = 56 * 1024 * 1024
LAYER_VMEM_LIMIT = 60 * 1024 * 1024

F32 = jnp.float32
BF16 = jnp.bfloat16


def _rmsnorm(x, g):
    r = lax.rsqrt(jnp.mean(x * x, axis=-1, keepdims=True) + RMS_EPS)
    return x * r * g


def _rope(x, rope):
    c = rope[:, 0:LANES]
    s = rope[:, LANES:2 * LANES]
    half = ROT_DIM // 2
    lane = lax.broadcasted_iota(jnp.int32, (1, LANES), 1)
    on_x1 = jnp.bitwise_and(lane, HEAD_DIM - 1) < half
    partner = jnp.where(on_x1, pltpu.roll(x, LANES - half, 1), pltpu.roll(x, half, 1))
    return x * c + partner * s


def _in_proj(h, ln_ref, win_ref):
    xn = _rmsnorm(h, ln_ref[...]).astype(BF16)
    return jnp.dot(xn, win_ref[...], preferred_element_type=F32)


def _split_rope(p, rope):
    qb = [_rope(p[:, j * LANES:(j + 1) * LANES], rope) * Q_SCALE for j in range(Q_BLOCKS)]
    k = _rope(p[:, ATTN_WIDTH:ATTN_WIDTH + KV_WIDTH], rope)
    v = p[:, ATTN_WIDTH + KV_WIDTH:ATTN_WIDTH + 2 * KV_WIDTH]
    u = p[:, ATTN_WIDTH + 2 * KV_WIDTH:]
    return qb, k, v, u


def _stack_heads(qrows, lo):
    zero = jnp.zeros_like(qrows[0])
    parts = [jnp.where(lo, q, zero) for q in qrows] + [jnp.where(lo, zero, q) for q in qrows]
    return jnp.concatenate(parts, axis=0).astype(BF16)


def _softmax_sink(s, sink):
    m = jnp.maximum(jnp.max(s, axis=-1, keepdims=True), sink)
    e = jnp.exp2(s - m)
    l = jnp.sum(e, axis=-1, keepdims=True) + jnp.exp2(sink - m)
    return e.astype(BF16), 1.0 / l


def _window_sum(x, w, axis):
    span = 1
    while span < w:
        x = x + pltpu.roll(x, span, axis)
        span *= 2
    return x


def _pool_project(d_groups, pw_ref, ps_ref):
    outs = [jnp.dot(d.astype(BF16), pw_ref[g], preferred_element_type=F32)
            for g, d in enumerate(d_groups)]
    return jnp.concatenate(outs, axis=-1) * ps_ref[...]


def _project_finish(p, rope_ref, kin_ref, vin_ref, uin_ref, pw_ref, ps_ref,
                    kt_ref, vt_ref, ut_ref, ubuf, dst, src, *, tile, first, p0):
    dq, dk, dvt, dm, _, _ = dst
    _, sk, svt, _, _, _ = src
    qb, k, v, u = _split_rope(p, rope_ref[...])
    lo = lax.broadcasted_iota(jnp.int32, (1, LANES), 1) < HEAD_DIM
    for r in range(tile // WINDOW):
        rows = slice(r * WINDOW, (r + 1) * WINDOW)
        dq[r] = _stack_heads([qb[j][rows] for j in range(Q_BLOCKS)], lo)
    dk[0:WINDOW, :] = jnp.where(first, kin_ref[0].astype(BF16), sk[tile:tile + WINDOW, :])
    dvt[:, 0:WINDOW] = jnp.where(first, vin_ref[0].T.astype(BF16), svt[:, tile:tile + WINDOW])
    dk[WINDOW:WINDOW + tile, :] = k.astype(BF16)
    dvt[:, WINDOW:WINDOW + tile] = v.T.astype(BF16)
    kt_ref[0] = k[tile - WINDOW:, :]
    vt_ref[0] = v[tile - WINDOW:, :]
    ut_ref[0] = u[tile - HIST_ROWS:, :]

    ubuf[0:HIST_ROWS, :] = jnp.where(first, uin_ref[0], ubuf[tile:tile + HIST_ROWS, :])
    ubuf[HIST_ROWS:HIST_ROWS + tile, :] = u
    pos = p0 + lax.broadcasted_iota(jnp.int32, (tile, 1), 0)
    d_groups = []
    for g, w in enumerate(POOL_WINDOWS):
        cols = slice(g * POOL_GROUP, (g + 1) * POOL_GROUP)
        wsum = _window_sum(ubuf[:, cols], w, 0)[HIST_ROWS:]
        cnt = jnp.clip(pos + 1, 1, w).astype(F32)
        d_groups.append(wsum / cnt - u[:, cols])
    dm[...] = _pool_project(d_groups, pw_ref, ps_ref).astype(BF16)


def _attend_scores(src, *, tile):
    sq, sk, _, _, _, _ = src
    return [lax.dot_general(sk[r * WINDOW:(r + 2) * WINDOW, :], sq[r], (((1,), (1,)), ((), ())),
                            preferred_element_type=F32) for r in range(tile // WINDOW)]


def _softmax_sink_keys_major(s, sink):
    m = jnp.maximum(jnp.max(s, axis=0, keepdims=True), sink)
    e = jnp.exp2(s - m)
    l = jnp.sum(e, axis=0, keepdims=True) + jnp.exp2(sink - m)
    return e.astype(BF16), 1.0 / l


def _attend_values(scores, sink_ref, src, *, layer, p0):
    _, _, svt, _, _, _ = src
    first_kv = lax.broadcasted_iota(jnp.int32, (KV_WIDTH, 1), 0) < HEAD_DIM
    kc = lax.broadcasted_iota(jnp.int32, (2 * WINDOW, WINDOW), 0)
    qi = lax.broadcasted_iota(jnp.int32, (2 * WINDOW, WINDOW), 1)
    band = (kc >= qi) & (kc <= qi + WINDOW)
    attn_rows = []
    for r, s in enumerate(scores):
        kpos = p0 + (r - 1) * WINDOW + kc
        bias = jnp.where(band & (kpos >= 0), 0.0, NEG_INF)
        probs, inv = zip(*[_softmax_sink_keys_major(s[:, hh * WINDOW:(hh + 1) * WINDOW] + bias,
                                                    sink_ref[layer, hh] * LOG2E) for hh in range(N_HEADS)])
        o = jnp.dot(svt[:, r * WINDOW:(r + 2) * WINDOW], jnp.concatenate(probs, axis=1),
                    preferred_element_type=F32) * jnp.concatenate(inv, axis=1)
        attn_rows.append(jnp.concatenate(
            [jnp.where(first_kv, o[:, j * WINDOW:(j + 1) * WINDOW],
                       o[:, (j + Q_BLOCKS) * WINDOW:(j + Q_BLOCKS + 1) * WINDOW]).T
             for j in range(Q_BLOCKS)], axis=-1))
    return jnp.concatenate(attn_rows, axis=0).astype(BF16)


def _layer_long_kernel(sink_ref, hp_ref, hr_ref, rope_ref, kin_ref, vin_ref, uin_ref, ln_ref, win_ref,
                       pw_ref, ps_ref, wout_ref, ln2_ref, wup_ref, wdown_ref, lnf_ref,
                       ho_ref, kt_ref, vt_ref, ut_ref,
                       qa, ka, va, ma, xa, ga, qb, kb, vb, mb, xb, gb, ubuf,
                       *, tile, base_pos, layer, tiles_per_seq, n_tiles, final_norm):
    i = pl.program_id(0)
    tp = lax.rem(jnp.minimum(i, n_tiles - 1), tiles_per_seq)
    ts = lax.rem(jnp.clip(i - 1, 0, n_tiles - 1), tiles_per_seq)
    set_a, set_b = (qa, ka, va, ma, xa, ga), (qb, kb, vb, mb, xb, gb)

    @pl.when(i == 0)
    def _():
        for ref in (kb, vb, ubuf):
            ref[...] = jnp.zeros(ref.shape, ref.dtype)

    def step(dst, src, project, attend, output, mlp):
        if attend:
            scores = _attend_scores(src, tile=tile)
        if project:
            p = _in_proj(hp_ref[0], ln_ref, win_ref)
        if mlp:
            ho_ref[0] = _mlp_rows(src[5][...], ln2_ref, wup_ref, wdown_ref, lnf_ref, final_norm)
        if output:
            dst[5][...] = hr_ref[0] + jnp.dot(dst[4][...], wout_ref[...], preferred_element_type=F32)
        if attend:
            attn = _attend_values(scores, sink_ref, src, layer=layer, p0=base_pos + ts * tile)
            src[4][...] = jnp.concatenate([attn, src[3][...]], axis=-1)
        if project:
            _project_finish(p, rope_ref, kin_ref, vin_ref, uin_ref, pw_ref, ps_ref, kt_ref, vt_ref, ut_ref,
                            ubuf, dst, src, tile=tile, first=tp == 0, p0=base_pos + tp * tile)

    variants = {}
    for idx in range(n_tiles + 3):
        key = (idx < n_tiles, 1 <= idx <= n_tiles, 2 <= idx <= n_tiles + 1, idx >= 3, idx % 2)
        variants.setdefault(key, []).append(idx)
    parity = lax.rem(i, 2)
    for (project, attend, output, mlp, par), steps in variants.items():
        dst, src = (set_a, set_b) if par == 0 else (set_b, set_a)
        cond = (i >= steps[0]) & (i <= steps[-1]) & (parity == par)
        pl.when(cond)(functools.partial(step, dst, src, project, attend, output, mlp))


def _const_spec(shape):
    nd = len(shape)
    return pl.BlockSpec(shape, lambda *_: (0,) * nd, pipeline_mode=pl.Buffered(1))


def _layer_spec(shape, layer):
    nd = len(shape)
    return pl.BlockSpec((None,) + shape, lambda *_: (layer,) + (0,) * nd, pipeline_mode=pl.Buffered(1))


def _mixer_weight_specs(layer):
    return [
        _layer_spec((1, D_MODEL), layer),
        _layer_spec((D_MODEL, IN_WIDTH), layer),
        _layer_spec((len(POOL_WINDOWS), POOL_GROUP, POOL_GROUP), layer),
        _layer_spec((1, POOL_WIDTH), layer),
        _layer_spec((D_MODEL, D_MODEL), layer),
    ]


def _layer_long(h, rope, kin, vin, uin, sinks, ln, win, pw, ps, wout, ln2, wup, wdown, lnf,
                *, tile, base_pos, layer, final_norm, name):
    n_seq, seq_len, _ = h.shape
    tps = seq_len // tile
    n_tiles = n_seq * tps
    shared_init = kin.shape[0] == 1

    def proj_tile(i):
        return jnp.minimum(i, n_tiles - 1)

    def res_tile(i):
        return jnp.clip(i - 2, 0, n_tiles - 1)

    def out_tile(i):
        return jnp.maximum(i - 3, 0)

    def init_idx(i):
        return (0 if shared_init else proj_tile(i) // tps, 0, 0)

    kern = functools.partial(_layer_long_kernel, tile=tile, base_pos=base_pos, layer=layer,
                             tiles_per_seq=tps, n_tiles=n_tiles, final_norm=final_norm)
    buffer_set = [
        pltpu.VMEM((tile // WINDOW, N_HEADS * WINDOW, LANES), BF16),
        pltpu.VMEM((WINDOW + tile, KV_WIDTH), BF16),
        pltpu.VMEM((KV_WIDTH, WINDOW + tile), BF16),
        pltpu.VMEM((tile, POOL_WIDTH), BF16),
        pltpu.VMEM((tile, D_MODEL), BF16),
        pltpu.VMEM((tile, D_MODEL), F32),
    ]
    return pl.pallas_call(
        kern,
        grid=(n_tiles + 3,),
        in_specs=[
            pl.BlockSpec(memory_space=pltpu.SMEM),
            pl.BlockSpec((1, tile, D_MODEL), lambda i: (proj_tile(i) // tps, proj_tile(i) % tps, 0)),
            pl.BlockSpec((1, tile, D_MODEL), lambda i: (res_tile(i) // tps, res_tile(i) % tps, 0)),
            pl.BlockSpec((tile, 2 * LANES), lambda i: (proj_tile(i) % tps, 0)),
            pl.BlockSpec((1, WINDOW, KV_WIDTH), init_idx),
            pl.BlockSpec((1, WINDOW, KV_WIDTH), init_idx),
            pl.BlockSpec((1, HIST_ROWS, POOL_WIDTH), init_idx),
        ] + _mixer_weight_specs(layer) + [
            _layer_spec((1, D_MODEL), layer),
            _layer_spec((D_MODEL, D_FF), layer),
            _layer_spec((D_FF, D_MODEL), layer),
            _const_spec((1, D_MODEL)),
        ],
        out_specs=[
            pl.BlockSpec((1, tile, D_MODEL), lambda i: (out_tile(i) // tps, out_tile(i) % tps, 0)),
            pl.BlockSpec((1, WINDOW, KV_WIDTH), lambda i: (proj_tile(i) // tps, 0, 0)),
            pl.BlockSpec((1, WINDOW, KV_WIDTH), lambda i: (proj_tile(i) // tps, 0, 0)),
            pl.BlockSpec((1, HIST_ROWS, POOL_WIDTH), lambda i: (proj_tile(i) // tps, 0, 0)),
        ],
        out_shape=[
            jax.ShapeDtypeStruct((n_seq, seq_len, D_MODEL), F32),
            jax.ShapeDtypeStruct((n_seq, WINDOW, KV_WIDTH), F32),
            jax.ShapeDtypeStruct((n_seq, WINDOW, KV_WIDTH), F32),
            jax.ShapeDtypeStruct((n_seq, HIST_ROWS, POOL_WIDTH), F32),
        ],
        scratch_shapes=buffer_set + buffer_set + [pltpu.VMEM((HIST_ROWS + tile, POOL_WIDTH), F32)],
        compiler_params=pltpu.CompilerParams(
            dimension_semantics=("arbitrary",), vmem_limit_bytes=LAYER_VMEM_LIMIT),
        name=name,
    )(sinks, h, h, rope, kin, vin, uin, ln, win, pw, ps, wout, ln2, wup, wdown, lnf)


def _mixer_short_kernel(sink_ref, h_ref, rope_ref, ck_ref, cv_ref, st_ref, ln_ref, win_ref,
                        pw_ref, ps_ref, wout_ref, *rest, seqs, layer):
    if layer:
        kprev_ref, vprev_ref, sprev_ref = rest[:3]
        rest = rest[3:]
    ho_ref, kc_ref, vc_ref, sn_ref, qbuf, abuf, ubuf, dbuf, ktbuf, vtbuf = rest
    if layer:
        kc_ref[0:layer] = kprev_ref[...]
        vc_ref[0:layer] = vprev_ref[...]
        sn_ref[0:layer] = sprev_ref[...]
    rows = seqs * DEC_SEQ
    h = h_ref[...]
    qb, k, v, u = _split_rope(_in_proj(h, ln_ref, win_ref), rope_ref[...])
    for j in range(Q_BLOCKS):
        qbuf[:, j * LANES:(j + 1) * LANES] = qb[j]
    kt, vt = k.T, v.T
    for b in range(rows // LANES):
        ktbuf[b] = kt[:, b * LANES:(b + 1) * LANES]
        vtbuf[b] = vt[:, b * LANES:(b + 1) * LANES]

    groups = len(POOL_WINDOWS)
    for g in range(groups):
        ubuf[g] = u[:, g * POOL_GROUP:(g + 1) * POOL_GROUP]
    new_slabs = [[ubuf[g, pl.ds(t, seqs, stride=DEC_SEQ), :] for g in range(groups)] for t in range(DEC_SEQ)]
    for r in range(POOL_HIST):
        src_row = r + DEC_SEQ
        sn_ref[layer, r] = (st_ref[src_row] if src_row < POOL_HIST
                            else jnp.concatenate(new_slabs[src_row - POOL_HIST], axis=-1))
    for g, w in enumerate(POOL_WINDOWS):
        cols = slice(g * POOL_GROUP, (g + 1) * POOL_GROUP)
        slabs = [st_ref[r, :, cols] for r in range(POOL_HIST)] + [new_slabs[t][g] for t in range(DEC_SEQ)]
        cnt = float(min(PAST_LEN + 1, w))
        for t in range(DEC_SEQ):
            wsum = slabs[POOL_HIST + t]
            for back in range(1, w):
                wsum = wsum + slabs[POOL_HIST + t - back]
            dbuf[g, pl.ds(t, seqs, stride=DEC_SEQ), :] = wsum / cnt - slabs[POOL_HIST + t]

    stacked = N_HEADS * DEC_SEQ
    kept = WINDOW - DEC_SEQ
    lane = lax.broadcasted_iota(jnp.int32, (1, LANES), 1)
    lo = lane < HEAD_DIM
    old = lane < kept
    srow = lax.broadcasted_iota(jnp.int32, (stacked, 2 * LANES), 0)
    kc = lax.broadcasted_iota(jnp.int32, (stacked, 2 * LANES), 1)
    qi = jnp.bitwise_and(srow, DEC_SEQ - 1)
    visible = ((kc < LANES) & (kc >= qi)) | ((kc >= LANES + kept) & (kc - (LANES + kept) <= qi))
    bias = jnp.where(visible, 0.0, NEG_INF)
    hrow = jnp.right_shift(lax.broadcasted_iota(jnp.int32, (stacked, 1), 0), DEC_SEQ.bit_length() - 1)
    sink = jnp.zeros((stacked, 1), F32)
    for hh in range(N_HEADS):
        sink = jnp.where(hrow == hh, sink_ref[layer, hh] * LOG2E, sink)

    def shifted(cache_ref, new_ref, gi, i, n):
        old_t = cache_ref[i]
        new_t = pltpu.roll(new_ref[gi], (kept - DEC_SEQ * n) % LANES, 1)
        return old_t, new_t, jnp.where(old, pltpu.roll(old_t, kept, 1), new_t)

    def group_body(gi, carry):
        ids = [gi * SAMPLE_GROUP + n for n in range(SAMPLE_GROUP)]
        starts = [pl.multiple_of(i * DEC_SEQ, DEC_SEQ) for i in ids]
        scores = []
        for n, (i, r0) in enumerate(zip(ids, starts)):
            qrows = qbuf[pl.ds(r0, DEC_SEQ), :]
            q = _stack_heads([qrows[:, j * LANES:(j + 1) * LANES] for j in range(Q_BLOCKS)], lo)
            old_t, new_t, kc_ref[layer, i] = shifted(ck_ref, ktbuf, gi, i, n)
            kall = jnp.concatenate([old_t, new_t], axis=1).astype(BF16)
            scores.append(jnp.dot(q, kall, preferred_element_type=F32))
        probs = [_softmax_sink(s + bias, sink) for s in scores]
        for n, (i, r0, (prob, inv)) in enumerate(zip(ids, starts, probs)):
            old_t, new_t, vc_ref[layer, i] = shifted(cv_ref, vtbuf, gi, i, n)
            vall = jnp.concatenate([old_t, new_t], axis=1).astype(BF16)
            o = lax.dot_general(prob, vall, (((1,), (1,)), ((), ())), preferred_element_type=F32) * inv
            abuf[pl.ds(r0, DEC_SEQ), :] = jnp.concatenate(
                [jnp.where(lo, o[j * DEC_SEQ:(j + 1) * DEC_SEQ],
                           o[(j + Q_BLOCKS) * DEC_SEQ:(j + Q_BLOCKS + 1) * DEC_SEQ])
                 for j in range(Q_BLOCKS)], axis=-1)
        return carry

    lax.fori_loop(0, seqs // SAMPLE_GROUP, group_body, 0)

    pm = _pool_project([dbuf[g] for g in range(groups)], pw_ref, ps_ref)

    mix = jnp.concatenate([abuf[...], pm], axis=-1).astype(BF16)
    ho_ref[...] = h + jnp.dot(mix, wout_ref[...], preferred_element_type=F32)


def _mixer_short(h, rope, ck, cv, st, sinks, ln, win, pw, ps, wout, kc_prev, vc_prev, sn_prev,
                 *, seqs, layer, name):
    n_rows = h.shape[0]
    rows = seqs * DEC_SEQ
    n_seq = n_rows // DEC_SEQ
    stacked_spec = pl.BlockSpec((layer + 1, seqs, KV_WIDTH, WINDOW), lambda i: (0, i, 0, 0))
    state_spec = pl.BlockSpec((layer + 1, POOL_HIST, seqs, POOL_WIDTH), lambda i: (0, 0, i, 0))
    prev_specs = [pl.BlockSpec((layer, seqs, KV_WIDTH, WINDOW), lambda i: (0, i, 0, 0))] * 2 + [
        pl.BlockSpec((layer, POOL_HIST, seqs, POOL_WIDTH), lambda i: (0, 0, i, 0))] if layer else []
    prev_args = (kc_prev, vc_prev, sn_prev) if layer else ()
    kern = functools.partial(_mixer_short_kernel, seqs=seqs, layer=layer)
    return pl.pallas_call(
        kern,
        grid=(n_rows // rows,),
        in_specs=[
            pl.BlockSpec(memory_space=pltpu.SMEM),
            pl.BlockSpec((rows, D_MODEL), lambda i: (i, 0)),
            _const_spec((rows, 2 * LANES)),
            pl.BlockSpec((None, seqs, KV_WIDTH, WINDOW), lambda i: (layer, i, 0, 0)),
            pl.BlockSpec((None, seqs, KV_WIDTH, WINDOW), lambda i: (layer, i, 0, 0)),
            pl.BlockSpec((None, POOL_HIST, seqs, POOL_WIDTH), lambda i: (layer, 0, i, 0)),
        ] + _mixer_weight_specs(layer) + prev_specs,
        out_specs=[
            pl.BlockSpec((rows, D_MODEL), lambda i: (i, 0)),
            stacked_spec,
            stacked_spec,
            state_spec,
        ],
        out_shape=[
            jax.ShapeDtypeStruct((n_rows, D_MODEL), F32),
            jax.ShapeDtypeStruct((layer + 1, n_seq, KV_WIDTH, WINDOW), F32),
            jax.ShapeDtypeStruct((layer + 1, n_seq, KV_WIDTH, WINDOW), F32),
            jax.ShapeDtypeStruct((layer + 1, POOL_HIST, n_seq, POOL_WIDTH), F32),
        ],
        scratch_shapes=[
            pltpu.VMEM((rows, ATTN_WIDTH), F32),
            pltpu.VMEM((rows, ATTN_WIDTH), F32),
            pltpu.VMEM((len(POOL_WINDOWS), rows, POOL_GROUP), F32),
            pltpu.VMEM((len(POOL_WINDOWS), rows, POOL_GROUP), F32),
            pltpu.VMEM((rows // LANES, KV_WIDTH, LANES), F32),
            pltpu.VMEM((rows // LANES, KV_WIDTH, LANES), F32),
        ],
        compiler_params=pltpu.CompilerParams(
            dimension_semantics=("arbitrary",), vmem_limit_bytes=VMEM_LIMIT),
        name=name,
    )(sinks, h, rope, ck, cv, st, ln, win, pw, ps, wout, *prev_args)


def _mlp_rows(h, ln_ref, wup_ref, wdown_ref, lnf_ref, final_norm):
    xn = _rmsnorm(h, ln_ref[...]).astype(BF16)

    def hidden(c):
        a = jnp.maximum(jnp.dot(xn, wup_ref[:, c * MLP_CHUNK:(c + 1) * MLP_CHUNK],
                                preferred_element_type=F32), 0.0)
        return (a * a).astype(BF16)

    def down(a, c):
        return jnp.dot(a, wdown_ref[c * MLP_CHUNK:(c + 1) * MLP_CHUNK, :], preferred_element_type=F32)

    out = h
    a_prev = hidden(0)
    for c in range(1, D_FF // MLP_CHUNK):
        a_next = hidden(c)
        out = out + down(a_prev, c - 1)
        a_prev = a_next
    out = out + down(a_prev, D_FF // MLP_CHUNK - 1)
    if final_norm:
        out = _rmsnorm(out, lnf_ref[...])
    return out


def _mlp_kernel(h_ref, ln_ref, wup_ref, wdown_ref, lnf_ref, o_ref, *, final_norm):
    o_ref[...] = _mlp_rows(h_ref[...], ln_ref, wup_ref, wdown_ref, lnf_ref, final_norm)


def _mlp(h, ln, wup, wdown, lnf, *, final_norm, layer, name):
    n_rows = h.shape[0]
    tile = min(MLP_TILE, n_rows)
    row_spec = pl.BlockSpec((tile, D_MODEL), lambda i: (i, 0))
    return pl.pallas_call(
        functools.partial(_mlp_kernel, final_norm=final_norm),
        grid=(n_rows // tile,),
        in_specs=[
            row_spec,
            _layer_spec((1, D_MODEL), layer),
            _layer_spec((D_MODEL, D_FF), layer),
            _layer_spec((D_FF, D_MODEL), layer),
            _const_spec((1, D_MODEL)),
        ],
        out_specs=row_spec,
        out_shape=jax.ShapeDtypeStruct(h.shape, F32),
        compiler_params=pltpu.CompilerParams(
            dimension_semantics=("arbitrary",), vmem_limit_bytes=VMEM_LIMIT),
        name=name,
    )(h, ln, wup, wdown, lnf)


def _rope_table(pos):
    n = pos.shape[0]
    inv_freq = ROPE_THETA ** (-jnp.arange(0, ROT_DIM, 2, dtype=F32) / ROT_DIM)
    ang = pos.astype(F32)[:, None] * inv_freq[None, :]
    cos, sin = jnp.cos(ang), jnp.sin(ang)
    rest = HEAD_DIM - ROT_DIM
    c = jnp.concatenate([cos, cos, jnp.ones((n, rest), F32)], axis=-1)
    s = jnp.concatenate([-sin, sin, jnp.zeros((n, rest), F32)], axis=-1)
    reps = LANES // HEAD_DIM
    return jnp.concatenate([jnp.tile(c, (1, reps)), jnp.tile(s, (1, reps))], axis=-1)


def _pair_heads(w, axis):
    shape = w.shape
    split = shape[:axis] + (N_HEADS // Q_BLOCKS, Q_BLOCKS, HEAD_DIM) + shape[axis + 1:]
    return jnp.swapaxes(w.reshape(split), axis, axis + 1).reshape(shape)


def kernel(x_prompt, x_sample, cache_k, cache_v, state_pool, meta_tokens, ln1, w_in, attn_sinks,
           pool_w, pool_scale, w_out, ln2, w_up, w_down, ln_f):
    batch, seq, _ = x_prompt.shape
    dec_batch, dec_seq, _ = x_sample.shape
    depth = w_in.shape[0]
    assert dec_seq == DEC_SEQ and seq % PROMPT_TILE == 0 and dec_batch % SAMPLE_SEQS == 0
    assert SAMPLE_SEQS % SAMPLE_GROUP == 0 and PAST_LEN >= WINDOW

    pad = WINDOW - N_META
    rope_meta = _rope_table(jnp.arange(WINDOW) - pad)
    rope_prompt = _rope_table(N_META + jnp.arange(seq))
    rope_sample = jnp.tile(_rope_table(PAST_LEN + jnp.arange(DEC_SEQ)), (SAMPLE_SEQS, 1))

    win = jnp.concatenate([_pair_heads(w_in[:, :, :ATTN_WIDTH], 2), w_in[:, :, ATTN_WIDTH:]], axis=2).astype(BF16)
    wout = jnp.concatenate([_pair_heads(w_out[:, :ATTN_WIDTH], 1), w_out[:, ATTN_WIDTH:]], axis=1).astype(BF16)
    wup = w_up.astype(BF16)
    wdown = w_down.astype(BF16)
    pw = pool_w.astype(BF16)
    ps = pool_scale.reshape(depth, 1, POOL_WIDTH)
    l1 = ln1.reshape(depth, 1, D_MODEL)
    l2 = ln2.reshape(depth, 1, D_MODEL)
    lnf = ln_f.reshape(1, D_MODEL)
    sinks = attn_sinks.astype(F32)
    mixer_w = (sinks, l1, win, pw, ps, wout)

    hm = jnp.concatenate([jnp.zeros((pad, D_MODEL), F32), meta_tokens.astype(F32)], axis=0)[None]
    hp = x_prompt
    hs = x_sample.reshape(dec_batch * dec_seq, D_MODEL)
    zero_kv = jnp.zeros((1, WINDOW, KV_WIDTH), F32)
    zero_u = jnp.zeros((1, HIST_ROWS, POOL_WIDTH), F32)
    st = jnp.transpose(state_pool, (0, 2, 1, 3))
    ck = jnp.transpose(cache_k, (0, 1, 3, 4, 2)).reshape(depth, dec_batch, KV_WIDTH, WINDOW)
    cv = jnp.transpose(cache_v, (0, 1, 3, 4, 2)).reshape(depth, dec_batch, KV_WIDTH, WINDOW)

    pk, pv, pu = [], [], []
    kc = vc = sn = None
    for l in range(depth):
        last = l == depth - 1
        mlp_w = (l2, wup, wdown, lnf)
        hm, km, vm, um = _layer_long(hm, rope_meta, zero_kv, zero_kv, zero_u, *mixer_w, *mlp_w, tile=WINDOW,
                                     base_pos=-pad, layer=l, final_norm=False, name=f"layer_meta_{l}")
        hp, kt, vt, ut = _layer_long(hp, rope_prompt, km, vm, um, *mixer_w, *mlp_w, tile=PROMPT_TILE,
                                     base_pos=N_META, layer=l, final_norm=last, name=f"layer_prompt_{l}")
        hs_mid, kc, vc, sn = _mixer_short(hs, rope_sample, ck, cv, st, *mixer_w, kc, vc, sn, seqs=SAMPLE_SEQS,
                                          layer=l, name=f"mixer_sample_{l}")

        hs = _mlp(hs_mid, *mlp_w, final_norm=last, layer=l, name=f"mlp_sample_{l}")

        pk.append(kt.reshape(batch, WINDOW, N_KV_HEADS, HEAD_DIM))
        pv.append(vt.reshape(batch, WINDOW, N_KV_HEADS, HEAD_DIM))
        pu.append(ut[:, 1:, :])

    y_sample = hs.reshape(dec_batch, dec_seq, D_MODEL)

    def window_major(c):
        return jnp.transpose(c.reshape(depth, dec_batch, N_KV_HEADS, HEAD_DIM, WINDOW), (0, 1, 4, 2, 3))

    return (hp, y_sample, jnp.stack(pk), jnp.stack(pv), jnp.stack(pu),
            window_major(kc), window_major(vc), jnp.transpose(sn, (0, 2, 1, 3)))
```

```python
import functools

import jax
import jax.numpy as jnp
from jax import lax
from jax.experimental import pallas as pl
from jax.experimental.pallas import tpu as pltpu

D_MODEL = 1024
N_HEADS = 8
N_KV_HEADS = 2
HEAD_DIM = 64
ATTN_WIDTH = N_HEADS * HEAD_DIM
KV_WIDTH = N_KV_HEADS * HEAD_DIM
POOL_WINDOWS = (2, 4, 8, 16)
POOL_WIDTH = D_MODEL - ATTN_WIDTH
POOL_GROUP = POOL_WIDTH // len(POOL_WINDOWS)
POOL_HIST = max(POOL_WINDOWS) - 1
IN_WIDTH = ATTN_WIDTH + 2 * KV_WIDTH + POOL_WIDTH
WINDOW = 128
ROT_DIM = HEAD_DIM // 4
ROPE_THETA = 500000.0
D_FF = 4 * D_MODEL
N_META = 16
RMS_EPS = 1e-5
PAST_LEN = 16384
DEC_SEQ = 8

LANES = 128
SAMPLE_GROUP = LANES // DEC_SEQ
HIST_ROWS = POOL_HIST + 1
Q_BLOCKS = ATTN_WIDTH // LANES
LOG2E = 1.4426950408889634
Q_SCALE = HEAD_DIM ** -0.5 * LOG2E
NEG_INF = float("-inf")

PROMPT_TILE = 512
SAMPLE_SEQS = 32
MLP_TILE = 512
MLP_CHUNK = 1024
VMEM_LIMIT = 56 * 1024 * 1024
LAYER_VMEM_LIMIT = 60 * 1024 * 1024

F32 = jnp.float32
BF16 = jnp.bfloat16


def _rmsnorm(x, g):
    r = lax.rsqrt(jnp.mean(x * x, axis=-1, keepdims=True) + RMS_EPS)
    return x * r * g


def _rope(x, rope):
    c = rope[:, 0:LANES]
    s = rope[:, LANES:2 * LANES]
    half = ROT_DIM // 2
    lane = lax.broadcasted_iota(jnp.int32, (1, LANES), 1)
    on_x1 = jnp.bitwise_and(lane, HEAD_DIM - 1) < half
    partner = jnp.where(on_x1, pltpu.roll(x, LANES - half, 1), pltpu.roll(x, half, 1))
    return x * c + partner * s


def _in_proj(h, ln_ref, win_ref):
    xn = _rmsnorm(h, ln_ref[...]).astype(BF16)
    return jnp.dot(xn, win_ref[...], preferred_element_type=F32)


def _split_rope(p, rope):
    qb = [_rope(p[:, j * LANES:(j + 1) * LANES], rope) * Q_SCALE for j in range(Q_BLOCKS)]
    k = _rope(p[:, ATTN_WIDTH:ATTN_WIDTH + KV_WIDTH], rope)
    v = p[:, ATTN_WIDTH + KV_WIDTH:ATTN_WIDTH + 2 * KV_WIDTH]
    u = p[:, ATTN_WIDTH + 2 * KV_WIDTH:]
    return qb, k, v, u


def _stack_heads(qrows, lo):
    zero = jnp.zeros_like(qrows[0])
    parts = [jnp.where(lo, q, zero) for q in qrows] + [jnp.where(lo, zero, q) for q in qrows]
    return jnp.concatenate(parts, axis=0).astype(BF16)


def _softmax_sink(s, sink):
    m = jnp.maximum(jnp.max(s, axis=-1, keepdims=True), sink)
    e = jnp.exp2(s - m)
    l = jnp.sum(e, axis=-1, keepdims=True) + jnp.exp2(sink - m)
    return e.astype(BF16), 1.0 / l


def _window_sum(x, w, axis):
    span = 1
    while span < w:
        x = x + pltpu.roll(x, span, axis)
        span *= 2
    return x


def _pool_project(d_groups, pw_ref, ps_ref):
    outs = [jnp.dot(d.astype(BF16), pw_ref[g], preferred_element_type=F32)
            for g, d in enumerate(d_groups)]
    return jnp.concatenate(outs, axis=-1) * ps_ref[...]


def _project_finish(p, rope_ref, kin_ref, vin_ref, uin_ref, pw_ref, ps_ref,
                    kt_ref, vt_ref, ut_ref, ubuf, dst, src, *, tile, first, p0):
    dq, dk, dvt, dm, _ = dst
    _, sk, svt, _, _ = src
    qb, k, v, u = _split_rope(p, rope_ref[...])
    lo = lax.broadcasted_iota(jnp.int32, (1, LANES), 1) < HEAD_DIM
    for r in range(tile // WINDOW):
        rows = slice(r * WINDOW, (r + 1) * WINDOW)
        dq[r] = _stack_heads([qb[j][rows] for j in range(Q_BLOCKS)], lo)
    dk[0:WINDOW, :] = jnp.where(first, kin_ref[0].astype(BF16), sk[tile:tile + WINDOW, :])
    dvt[:, 0:WINDOW] = jnp.where(first, vin_ref[0].T.astype(BF16), svt[:, tile:tile + WINDOW])
    dk[WINDOW:WINDOW + tile, :] = k.astype(BF16)
    dvt[:, WINDOW:WINDOW + tile] = v.T.astype(BF16)
    kt_ref[0] = k[tile - WINDOW:, :]
    vt_ref[0] = v[tile - WINDOW:, :]
    ut_ref[0] = u[tile - HIST_ROWS:, :]

    ubuf[0:HIST_ROWS, :] = jnp.where(first, uin_ref[0], ubuf[tile:tile + HIST_ROWS, :])
    ubuf[HIST_ROWS:HIST_ROWS + tile, :] = u
    pos = p0 + lax.broadcasted_iota(jnp.int32, (tile, 1), 0)
    d_groups = []
    for g, w in enumerate(POOL_WINDOWS):
        cols = slice(g * POOL_GROUP, (g + 1) * POOL_GROUP)
        wsum = _window_sum(ubuf[:, cols], w, 0)[HIST_ROWS:]
        cnt = jnp.clip(pos + 1, 1, w).astype(F32)
        d_groups.append(wsum / cnt - u[:, cols])
    dm[...] = _pool_project(d_groups, pw_ref, ps_ref).astype(BF16)


def _attend_scores(src, *, tile):
    sq, sk, _, _, _ = src
    return [lax.dot_general(sk[r * WINDOW:(r + 2) * WINDOW, :], sq[r], (((1,), (1,)), ((), ())),
                            preferred_element_type=F32) for r in range(tile // WINDOW)]


def _softmax_sink_keys_major(s, sink):
    m = jnp.maximum(jnp.max(s, axis=0, keepdims=True), sink)
    e = jnp.exp2(s - m)
    l = jnp.sum(e, axis=0, keepdims=True) + jnp.exp2(sink - m)
    return e.astype(BF16), 1.0 / l


def _attend_values(scores, sink_ref, src, *, layer, p0):
    _, _, svt, _, _ = src
    first_kv = lax.broadcasted_iota(jnp.int32, (KV_WIDTH, 1), 0) < HEAD_DIM
    kc = lax.broadcasted_iota(jnp.int32, (2 * WINDOW, WINDOW), 0)
    qi = lax.broadcasted_iota(jnp.int32, (2 * WINDOW, WINDOW), 1)
    band = (kc >= qi) & (kc <= qi + WINDOW)
    attn_rows = []
    for r, s in enumerate(scores):
        kpos = p0 + (r - 1) * WINDOW + kc
        bias = jnp.where(band & (kpos >= 0), 0.0, NEG_INF)
        probs, inv = zip(*[_softmax_sink_keys_major(s[:, hh * WINDOW:(hh + 1) * WINDOW] + bias,
                                                    sink_ref[layer, hh] * LOG2E) for hh in range(N_HEADS)])
        o = jnp.dot(svt[:, r * WINDOW:(r + 2) * WINDOW], jnp.concatenate(probs, axis=1),
                    preferred_element_type=F32) * jnp.concatenate(inv, axis=1)
        attn_rows.append(jnp.concatenate(
            [jnp.where(first_kv, o[:, j * WINDOW:(j + 1) * WINDOW],
                       o[:, (j + Q_BLOCKS) * WINDOW:(j + Q_BLOCKS + 1) * WINDOW]).T
             for j in range(Q_BLOCKS)], axis=-1))
    return jnp.concatenate(attn_rows, axis=0).astype(BF16)


def _layer_long_kernel(sink_ref, hp_ref, hr_ref, rope_ref, kin_ref, vin_ref, uin_ref, ln_ref, win_ref,
                       pw_ref, ps_ref, wout_ref, ln2_ref, wup_ref, wdown_ref, lnf_ref,
                       ho_ref, kt_ref, vt_ref, ut_ref,
                       qa, ka, va, ma, xa, qb, kb, vb, mb, xb, ubuf, gbuf,
                       *, tile, base_pos, layer, tiles_per_seq, n_tiles, final_norm):
    i = pl.program_id(0)
    tp = lax.rem(jnp.minimum(i, n_tiles - 1), tiles_per_seq)
    ts = lax.rem(jnp.clip(i - 1, 0, n_tiles - 1), tiles_per_seq)
    set_a, set_b = (qa, ka, va, ma, xa), (qb, kb, vb, mb, xb)

    @pl.when(i == 0)
    def _():
        for ref in (kb, vb, ubuf):
            ref[...] = jnp.zeros(ref.shape, ref.dtype)

    def step(dst, src, par, project, attend, output):
        if attend:
            scores = _attend_scores(src, tile=tile)
        if project:
            p = _in_proj(hp_ref[0], ln_ref, win_ref)
        if output:
            gbuf[par] = hr_ref[0] + jnp.dot(dst[4][...], wout_ref[...], preferred_element_type=F32)
        if attend:
            attn = _attend_values(scores, sink_ref, src, layer=layer, p0=base_pos + ts * tile)
            src[4][...] = jnp.concatenate([attn, src[3][...]], axis=-1)
        if project:
            _project_finish(p, rope_ref, kin_ref, vin_ref, uin_ref, pw_ref, ps_ref, kt_ref, vt_ref, ut_ref,
                            ubuf, dst, src, tile=tile, first=tp == 0, p0=base_pos + tp * tile)

    variants = {}
    for idx in range(n_tiles + 2):
        key = (idx < n_tiles, 1 <= idx <= n_tiles, idx >= 2, idx % 2)
        variants.setdefault(key, []).append(idx)
    parity = lax.rem(i, 2)
    for (project, attend, output, par), steps in variants.items():
        dst, src = (set_a, set_b) if par == 0 else (set_b, set_a)
        cond = (i >= steps[0]) & (i <= steps[-1]) & (parity == par)
        pl.when(cond)(functools.partial(step, dst, src, par, project, attend, output))

    @pl.when(i >= 3)
    def _():
        ho_ref[0] = _mlp_rows(gbuf[1 - parity], ln2_ref, wup_ref, wdown_ref, lnf_ref, final_norm)


def _const_spec(shape):
    nd = len(shape)
    return pl.BlockSpec(shape, lambda *_: (0,) * nd, pipeline_mode=pl.Buffered(1))


def _layer_spec(shape, layer):
    nd = len(shape)
    return pl.BlockSpec((None,) + shape, lambda *_: (layer,) + (0,) * nd, pipeline_mode=pl.Buffered(1))


def _mixer_weight_specs(layer):
    return [
        _layer_spec((1, D_MODEL), layer),
        _layer_spec((D_MODEL, IN_WIDTH), layer),
        _layer_spec((len(POOL_WINDOWS), POOL_GROUP, POOL_GROUP), layer),
        _layer_spec((1, POOL_WIDTH), layer),
        _layer_spec((D_MODEL, D_MODEL), layer),
    ]


def _layer_long(h, rope, kin, vin, uin, sinks, ln, win, pw, ps, wout, ln2, wup, wdown, lnf,
                *, tile, base_pos, layer, final_norm, name):
    n_seq, seq_len, _ = h.shape
    tps = seq_len // tile
    n_tiles = n_seq * tps
    shared_init = kin.shape[0] == 1

    def proj_tile(i):
        return jnp.minimum(i, n_tiles - 1)

    def res_tile(i):
        return jnp.clip(i - 2, 0, n_tiles - 1)

    def out_tile(i):
        return jnp.maximum(i - 3, 0)

    def init_idx(i):
        return (0 if shared_init else proj_tile(i) // tps, 0, 0)

    kern = functools.partial(_layer_long_kernel, tile=tile, base_pos=base_pos, layer=layer,
                             tiles_per_seq=tps, n_tiles=n_tiles, final_norm=final_norm)
    buffer_set = [
        pltpu.VMEM((tile // WINDOW, N_HEADS * WINDOW, LANES), BF16),
        pltpu.VMEM((WINDOW + tile, KV_WIDTH), BF16),
        pltpu.VMEM((KV_WIDTH, WINDOW + tile), BF16),
        pltpu.VMEM((tile, POOL_WIDTH), BF16),
        pltpu.VMEM((tile, D_MODEL), BF16),
    ]
    return pl.pallas_call(
        kern,
        grid=(n_tiles + 3,),
        in_specs=[
            pl.BlockSpec(memory_space=pltpu.SMEM),
            pl.BlockSpec((1, tile, D_MODEL), lambda i: (proj_tile(i) // tps, proj_tile(i) % tps, 0)),
            pl.BlockSpec((1, tile, D_MODEL), lambda i: (res_tile(i) // tps, res_tile(i) % tps, 0)),
            pl.BlockSpec((tile, 2 * LANES), lambda i: (proj_tile(i) % tps, 0)),
            pl.BlockSpec((1, WINDOW, KV_WIDTH), init_idx),
            pl.BlockSpec((1, WINDOW, KV_WIDTH), init_idx),
            pl.BlockSpec((1, HIST_ROWS, POOL_WIDTH), init_idx),
        ] + _mixer_weight_specs(layer) + [
            _layer_spec((1, D_MODEL), layer),
            _layer_spec((D_MODEL, D_FF), layer),
            _layer_spec((D_FF, D_MODEL), layer),
            _const_spec((1, D_MODEL)),
        ],
        out_specs=[
            pl.BlockSpec((1, tile, D_MODEL), lambda i: (out_tile(i) // tps, out_tile(i) % tps, 0)),
            pl.BlockSpec((1, WINDOW, KV_WIDTH), lambda i: (proj_tile(i) // tps, 0, 0)),
            pl.BlockSpec((1, WINDOW, KV_WIDTH), lambda i: (proj_tile(i) // tps, 0, 0)),
            pl.BlockSpec((1, HIST_ROWS, POOL_WIDTH), lambda i: (proj_tile(i) // tps, 0, 0)),
        ],
        out_shape=[
            jax.ShapeDtypeStruct((n_seq, seq_len, D_MODEL), F32),
            jax.ShapeDtypeStruct((n_seq, WINDOW, KV_WIDTH), F32),
            jax.ShapeDtypeStruct((n_seq, WINDOW, KV_WIDTH), F32),
            jax.ShapeDtypeStruct((n_seq, HIST_ROWS, POOL_WIDTH), F32),
        ],
        scratch_shapes=buffer_set + buffer_set + [pltpu.VMEM((HIST_ROWS + tile, POOL_WIDTH), F32),
                                                  pltpu.VMEM((2, tile, D_MODEL), F32)],
        compiler_params=pltpu.CompilerParams(
            dimension_semantics=("arbitrary",), vmem_limit_bytes=LAYER_VMEM_LIMIT),
        name=name,
    )(sinks, h, h, rope, kin, vin, uin, ln, win, pw, ps, wout, ln2, wup, wdown, lnf)


def _mixer_short_kernel(sink_ref, h_ref, rope_ref, ck_ref, cv_ref, st_ref, ln_ref, win_ref,
                        pw_ref, ps_ref, wout_ref, *rest, seqs, layer):
    if layer:
        kprev_ref, vprev_ref, sprev_ref = rest[:3]
        rest = rest[3:]
    ho_ref, kc_ref, vc_ref, sn_ref, qbuf, abuf, ubuf, dbuf, ktbuf, vtbuf = rest
    if layer:
        kc_ref[0:layer] = kprev_ref[...]
        vc_ref[0:layer] = vprev_ref[...]
        sn_ref[0:layer] = sprev_ref[...]
    rows = seqs * DEC_SEQ
    h = h_ref[...]
    qb, k, v, u = _split_rope(_in_proj(h, ln_ref, win_ref), rope_ref[...])
    for j in range(Q_BLOCKS):
        qbuf[:, j * LANES:(j + 1) * LANES] = qb[j]
    kt, vt = k.T, v.T
    for b in range(rows // LANES):
        ktbuf[b] = kt[:, b * LANES:(b + 1) * LANES]
        vtbuf[b] = vt[:, b * LANES:(b + 1) * LANES]

    groups = len(POOL_WINDOWS)
    for g in range(groups):
        ubuf[g] = u[:, g * POOL_GROUP:(g + 1) * POOL_GROUP]
    new_slabs = [[ubuf[g, pl.ds(t, seqs, stride=DEC_SEQ), :] for g in range(groups)] for t in range(DEC_SEQ)]
    for r in range(POOL_HIST):
        src_row = r + DEC_SEQ
        sn_ref[layer, r] = (st_ref[src_row] if src_row < POOL_HIST
                            else jnp.concatenate(new_slabs[src_row - POOL_HIST], axis=-1))
    for g, w in enumerate(POOL_WINDOWS):
        cols = slice(g * POOL_GROUP, (g + 1) * POOL_GROUP)
        slabs = [st_ref[r, :, cols] for r in range(POOL_HIST)] + [new_slabs[t][g] for t in range(DEC_SEQ)]
        cnt = float(min(PAST_LEN + 1, w))
        for t in range(DEC_SEQ):
            wsum = slabs[POOL_HIST + t]
            for back in range(1, w):
                wsum = wsum + slabs[POOL_HIST + t - back]
            dbuf[g, pl.ds(t, seqs, stride=DEC_SEQ), :] = wsum / cnt - slabs[POOL_HIST + t]

    stacked = N_HEADS * DEC_SEQ
    kept = WINDOW - DEC_SEQ
    lane = lax.broadcasted_iota(jnp.int32, (1, LANES), 1)
    lo = lane < HEAD_DIM
    old = lane < kept
    srow = lax.broadcasted_iota(jnp.int32, (stacked, 2 * LANES), 0)
    kc = lax.broadcasted_iota(jnp.int32, (stacked, 2 * LANES), 1)
    qi = jnp.bitwise_and(srow, DEC_SEQ - 1)
    visible = ((kc < LANES) & (kc >= qi)) | ((kc >= LANES + kept) & (kc - (LANES + kept) <= qi))
    bias = jnp.where(visible, 0.0, NEG_INF)
    hrow = jnp.right_shift(lax.broadcasted_iota(jnp.int32, (stacked, 1), 0), DEC_SEQ.bit_length() - 1)
    sink = jnp.zeros((stacked, 1), F32)
    for hh in range(N_HEADS):
        sink = jnp.where(hrow == hh, sink_ref[layer, hh] * LOG2E, sink)

    def shifted(cache_ref, new_ref, gi, i, n):
        old_t = cache_ref[i]
        new_t = pltpu.roll(new_ref[gi], (kept - DEC_SEQ * n) % LANES, 1)
        return old_t, new_t, jnp.where(old, pltpu.roll(old_t, kept, 1), new_t)

    def group_body(gi, carry):
        ids = [gi * SAMPLE_GROUP + n for n in range(SAMPLE_GROUP)]
        starts = [pl.multiple_of(i * DEC_SEQ, DEC_SEQ) for i in ids]
        scores = []
        for n, (i, r0) in enumerate(zip(ids, starts)):
            qrows = qbuf[pl.ds(r0, DEC_SEQ), :]
            q = _stack_heads([qrows[:, j * LANES:(j + 1) * LANES] for j in range(Q_BLOCKS)], lo)
            old_t, new_t, kc_ref[layer, i] = shifted(ck_ref, ktbuf, gi, i, n)
            kall = jnp.concatenate([old_t, new_t], axis=1).astype(BF16)
            scores.append(jnp.dot(q, kall, preferred_element_type=F32))
        probs = [_softmax_sink(s + bias, sink) for s in scores]
        for n, (i, r0, (prob, inv)) in enumerate(zip(ids, starts, probs)):
            old_t, new_t, vc_ref[layer, i] = shifted(cv_ref, vtbuf, gi, i, n)
            vall = jnp.concatenate([old_t, new_t], axis=1).astype(BF16)
            o = lax.dot_general(prob, vall, (((1,), (1,)), ((), ())), preferred_element_type=F32) * inv
            abuf[pl.ds(r0, DEC_SEQ), :] = jnp.concatenate(
                [jnp.where(lo, o[j * DEC_SEQ:(j + 1) * DEC_SEQ],
                           o[(j + Q_BLOCKS) * DEC_SEQ:(j + Q_BLOCKS + 1) * DEC_SEQ])
                 for j in range(Q_BLOCKS)], axis=-1)
        return carry

    lax.fori_loop(0, seqs // SAMPLE_GROUP, group_body, 0)

    pm = _pool_project([dbuf[g] for g in range(groups)], pw_ref, ps_ref)

    mix = jnp.concatenate([abuf[...], pm], axis=-1).astype(BF16)
    ho_ref[...] = h + jnp.dot(mix, wout_ref[...], preferred_element_type=F32)


def _mixer_short(h, rope, ck, cv, st, sinks, ln, win, pw, ps, wout, kc_prev, vc_prev, sn_prev,
                 *, seqs, layer, name):
    n_rows = h.shape[0]
    rows = seqs * DEC_SEQ
    n_seq = n_rows // DEC_SEQ
    stacked_spec = pl.BlockSpec((layer + 1, seqs, KV_WIDTH, WINDOW), lambda i: (0, i, 0, 0))
    state_spec = pl.BlockSpec((layer + 1, POOL_HIST, seqs, POOL_WIDTH), lambda i: (0, 0, i, 0))
    prev_specs = [pl.BlockSpec((layer, seqs, KV_WIDTH, WINDOW), lambda i: (0, i, 0, 0))] * 2 + [
        pl.BlockSpec((layer, POOL_HIST, seqs, POOL_WIDTH), lambda i: (0, 0, i, 0))] if layer else []
    prev_args = (kc_prev, vc_prev, sn_prev) if layer else ()
    kern = functools.partial(_mixer_short_kernel, seqs=seqs, layer=layer)
    return pl.pallas_call(
        kern,
        grid=(n_rows // rows,),
        in_specs=[
            pl.BlockSpec(memory_space=pltpu.SMEM),
            pl.BlockSpec((rows, D_MODEL), lambda i: (i, 0)),
            _const_spec((rows, 2 * LANES)),
            pl.BlockSpec((None, seqs, KV_WIDTH, WINDOW), lambda i: (layer, i, 0, 0)),
            pl.BlockSpec((None, seqs, KV_WIDTH, WINDOW), lambda i: (layer, i, 0, 0)),
            pl.BlockSpec((None, POOL_HIST, seqs, POOL_WIDTH), lambda i: (layer, 0, i, 0)),
        ] + _mixer_weight_specs(layer) + prev_specs,
        out_specs=[
            pl.BlockSpec((rows, D_MODEL), lambda i: (i, 0)),
            stacked_spec,
            stacked_spec,
            state_spec,
        ],
        out_shape=[
            jax.ShapeDtypeStruct((n_rows, D_MODEL), F32),
            jax.ShapeDtypeStruct((layer + 1, n_seq, KV_WIDTH, WINDOW), F32),
            jax.ShapeDtypeStruct((layer + 1, n_seq, KV_WIDTH, WINDOW), F32),
            jax.ShapeDtypeStruct((layer + 1, POOL_HIST, n_seq, POOL_WIDTH), F32),
        ],
        scratch_shapes=[
            pltpu.VMEM((rows, ATTN_WIDTH), F32),
            pltpu.VMEM((rows, ATTN_WIDTH), F32),
            pltpu.VMEM((len(POOL_WINDOWS), rows, POOL_GROUP), F32),
            pltpu.VMEM((len(POOL_WINDOWS), rows, POOL_GROUP), F32),
            pltpu.VMEM((rows // LANES, KV_WIDTH, LANES), F32),
            pltpu.VMEM((rows // LANES, KV_WIDTH, LANES), F32),
        ],
        compiler_params=pltpu.CompilerParams(
            dimension_semantics=("arbitrary",), vmem_limit_bytes=VMEM_LIMIT),
        name=name,
    )(sinks, h, rope, ck, cv, st, ln, win, pw, ps, wout, *prev_args)


def _mlp_rows(h, ln_ref, wup_ref, wdown_ref, lnf_ref, final_norm):
    xn = _rmsnorm(h, ln_ref[...]).astype(BF16)

    def hidden(c):
        a = jnp.maximum(jnp.dot(xn, wup_ref[:, c * MLP_CHUNK:(c + 1) * MLP_CHUNK],
                                preferred_element_type=F32), 0.0)
        return (a * a).astype(BF16)

    def down(a, c):
        return jnp.dot(a, wdown_ref[c * MLP_CHUNK:(c + 1) * MLP_CHUNK, :], preferred_element_type=F32)

    out = h
    a_prev = hidden(0)
    for c in range(1, D_FF // MLP_CHUNK):
        a_next = hidden(c)
        out = out + down(a_prev, c - 1)
        a_prev = a_next
    out = out + down(a_prev, D_FF // MLP_CHUNK - 1)
    if final_norm:
        out = _rmsnorm(out, lnf_ref[...])
    return out


def _mlp_kernel(h_ref, ln_ref, wup_ref, wdown_ref, lnf_ref, o_ref, *, final_norm):
    o_ref[...] = _mlp_rows(h_ref[...], ln_ref, wup_ref, wdown_ref, lnf_ref, final_norm)


def _mlp(h, ln, wup, wdown, lnf, *, final_norm, layer, name):
    n_rows = h.shape[0]
    tile = min(MLP_TILE, n_rows)
    row_spec = pl.BlockSpec((tile, D_MODEL), lambda i: (i, 0))
    return pl.pallas_call(
        functools.partial(_mlp_kernel, final_norm=final_norm),
        grid=(n_rows // tile,),
        in_specs=[
            row_spec,
            _layer_spec((1, D_MODEL), layer),
            _layer_spec((D_MODEL, D_FF), layer),
            _layer_spec((D_FF, D_MODEL), layer),
            _const_spec((1, D_MODEL)),
        ],
        out_specs=row_spec,
        out_shape=jax.ShapeDtypeStruct(h.shape, F32),
        compiler_params=pltpu.CompilerParams(
            dimension_semantics=("arbitrary",), vmem_limit_bytes=VMEM_LIMIT),
        name=name,
    )(h, ln, wup, wdown, lnf)


def _rope_table(pos):
    n = pos.shape[0]
    inv_freq = ROPE_THETA ** (-jnp.arange(0, ROT_DIM, 2, dtype=F32) / ROT_DIM)
    ang = pos.astype(F32)[:, None] * inv_freq[None, :]
    cos, sin = jnp.cos(ang), jnp.sin(ang)
    rest = HEAD_DIM - ROT_DIM
    c = jnp.concatenate([cos, cos, jnp.ones((n, rest), F32)], axis=-1)
    s = jnp.concatenate([-sin, sin, jnp.zeros((n, rest), F32)], axis=-1)
    reps = LANES // HEAD_DIM
    return jnp.concatenate([jnp.tile(c, (1, reps)), jnp.tile(s, (1, reps))], axis=-1)


def _pair_heads(w, axis):
    shape = w.shape
    split = shape[:axis] + (N_HEADS // Q_BLOCKS, Q_BLOCKS, HEAD_DIM) + shape[axis + 1:]
    return jnp.swapaxes(w.reshape(split), axis, axis + 1).reshape(shape)


def kernel(x_prompt, x_sample, cache_k, cache_v, state_pool, meta_tokens, ln1, w_in, attn_sinks,
           pool_w, pool_scale, w_out, ln2, w_up, w_down, ln_f):
    batch, seq, _ = x_prompt.shape
    dec_batch, dec_seq, _ = x_sample.shape
    depth = w_in.shape[0]
    assert dec_seq == DEC_SEQ and seq % PROMPT_TILE == 0 and dec_batch % SAMPLE_SEQS == 0
    assert SAMPLE_SEQS % SAMPLE_GROUP == 0 and PAST_LEN >= WINDOW

    pad = WINDOW - N_META
    rope_meta = _rope_table(jnp.arange(WINDOW) - pad)
    rope_prompt = _rope_table(N_META + jnp.arange(seq))
    rope_sample = jnp.tile(_rope_table(PAST_LEN + jnp.arange(DEC_SEQ)), (SAMPLE_SEQS, 1))

    win = jnp.concatenate([_pair_heads(w_in[:, :, :ATTN_WIDTH], 2), w_in[:, :, ATTN_WIDTH:]], axis=2).astype(BF16)
    wout = jnp.concatenate([_pair_heads(w_out[:, :ATTN_WIDTH], 1), w_out[:, ATTN_WIDTH:]], axis=1).astype(BF16)
    wup = w_up.astype(BF16)
    wdown = w_down.astype(BF16)
    pw = pool_w.astype(BF16)
    ps = pool_scale.reshape(depth, 1, POOL_WIDTH)
    l1 = ln1.reshape(depth, 1, D_MODEL)
    l2 = ln2.reshape(depth, 1, D_MODEL)
    lnf = ln_f.reshape(1, D_MODEL)
    sinks = attn_sinks.astype(F32)
    mixer_w = (sinks, l1, win, pw, ps, wout)

    hm = jnp.concatenate([jnp.zeros((pad, D_MODEL), F32), meta_tokens.astype(F32)], axis=0)[None]
    hp = x_prompt
    hs = x_sample.reshape(dec_batch * dec_seq, D_MODEL)
    zero_kv = jnp.zeros((1, WINDOW, KV_WIDTH), F32)
    zero_u = jnp.zeros((1, HIST_ROWS, POOL_WIDTH), F32)
    st = jnp.transpose(state_pool, (0, 2, 1, 3))
    ck = jnp.transpose(cache_k, (0, 1, 3, 4, 2)).reshape(depth, dec_batch, KV_WIDTH, WINDOW)
    cv = jnp.transpose(cache_v, (0, 1, 3, 4, 2)).reshape(depth, dec_batch, KV_WIDTH, WINDOW)

    pk, pv, pu = [], [], []
    kc = vc = sn = None
    for l in range(depth):
        last = l == depth - 1
        mlp_w = (l2, wup, wdown, lnf)
        hm, km, vm, um = _layer_long(hm, rope_meta, zero_kv, zero_kv, zero_u, *mixer_w, *mlp_w, tile=WINDOW,
                                     base_pos=-pad, layer=l, final_norm=False, name=f"layer_meta_{l}")
        hp, kt, vt, ut = _layer_long(hp, rope_prompt, km, vm, um, *mixer_w, *mlp_w, tile=PROMPT_TILE,
                                     base_pos=N_META, layer=l, final_norm=last, name=f"layer_prompt_{l}")
        hs_mid, kc, vc, sn = _mixer_short(hs, rope_sample, ck, cv, st, *mixer_w, kc, vc, sn, seqs=SAMPLE_SEQS,
                                          layer=l, name=f"mixer_sample_{l}")

        hs = _mlp(hs_mid, *mlp_w, final_norm=last, layer=l, name=f"mlp_sample_{l}")

        pk.append(kt.reshape(batch, WINDOW, N_KV_HEADS, HEAD_DIM))
        pv.append(vt.reshape(batch, WINDOW, N_KV_HEADS, HEAD_DIM))
        pu.append(ut[:, 1:, :])

    y_sample = hs.reshape(dec_batch, dec_seq, D_MODEL)

    def window_major(c):
        return jnp.transpose(c.reshape(depth, dec_batch, N_KV_HEADS, HEAD_DIM, WINDOW), (0, 1, 4, 2, 3))

    return (hp, y_sample, jnp.stack(pk), jnp.stack(pv), jnp.stack(pu),
            window_major(kc), window_major(vc), jnp.transpose(sn, (0, 2, 1, 3)))
```

```python
import functools

import jax
import jax.numpy as jnp
from jax import lax
from jax.experimental import pallas as pl
from jax.experimental.pallas import tpu as pltpu

D_MODEL = 1024
N_HEADS = 8
N_KV_HEADS = 2
HEAD_DIM = 64
ATTN_WIDTH = N_HEADS * HEAD_DIM
KV_WIDTH = N_KV_HEADS * HEAD_DIM
POOL_WINDOWS = (2, 4, 8, 16)
POOL_WIDTH = D_MODEL - ATTN_WIDTH
POOL_GROUP = POOL_WIDTH // len(POOL_WINDOWS)
POOL_HIST = max(POOL_WINDOWS) - 1
IN_WIDTH = ATTN_WIDTH + 2 * KV_WIDTH + POOL_WIDTH
WINDOW = 128
ROT_DIM = HEAD_DIM // 4
ROPE_THETA = 500000.0
D_FF = 4 * D_MODEL
N_META = 16
RMS_EPS = 1e-5
PAST_LEN = 16384
DEC_SEQ = 8

LANES = 128
SAMPLE_GROUP = LANES // DEC_SEQ
HIST_ROWS = POOL_HIST + 1
Q_BLOCKS = ATTN_WIDTH // LANES
LOG2E = 1.4426950408889634
Q_SCALE = HEAD_DIM ** -0.5 * LOG2E
NEG_INF = float("-inf")

PROMPT_TILE = 512
SAMPLE_SEQS = 32
MLP_TILE = 1024
MLP_SMALL_TILE = 256
MLP_CHUNK = 1024
MLP_HALF_MIN = 256
VMEM_LIMIT = 56 * 1024 * 1024

F32 = jnp.float32
BF16 = jnp.bfloat16


def _rmsnorm(x, g):
    r = lax.rsqrt(jnp.mean(x * x, axis=-1, keepdims=True) + RMS_EPS)
    return x * r * g


def _rope(x, rope):
    c = rope[:, 0:LANES]
    s = rope[:, LANES:2 * LANES]
    half = ROT_DIM // 2
    lane = lax.broadcasted_iota(jnp.int32, (1, LANES), 1)
    on_x1 = jnp.bitwise_and(lane, HEAD_DIM - 1) < half
    partner = jnp.where(on_x1, pltpu.roll(x, LANES - half, 1), pltpu.roll(x, half, 1))
    return x * c + partner * s


def _in_proj(h, ln_ref, win_ref):
    xn = _rmsnorm(h, ln_ref[...]).astype(BF16)
    return jnp.dot(xn, win_ref[...], preferred_element_type=F32)


def _split_rope(p, rope):
    qb = [_rope(p[:, j * LANES:(j + 1) * LANES], rope) * Q_SCALE for j in range(Q_BLOCKS)]
    k = _rope(p[:, ATTN_WIDTH:ATTN_WIDTH + KV_WIDTH], rope)
    v = p[:, ATTN_WIDTH + KV_WIDTH:ATTN_WIDTH + 2 * KV_WIDTH]
    u = p[:, ATTN_WIDTH + 2 * KV_WIDTH:]
    return qb, k, v, u


def _stack_heads(qrows, lo):
    zero = jnp.zeros_like(qrows[0])
    parts = [jnp.where(lo, q, zero) for q in qrows] + [jnp.where(lo, zero, q) for q in qrows]
    return jnp.concatenate(parts, axis=0).astype(BF16)


def _softmax_sink(s, sink):
    m = jnp.maximum(jnp.max(s, axis=-1, keepdims=True), sink)
    e = jnp.exp2(s - m)
    l = jnp.sum(e, axis=-1, keepdims=True) + jnp.exp2(sink - m)
    return e.astype(BF16), 1.0 / l


def _window_sum(x, w, axis):
    span = 1
    while span < w:
        x = x + pltpu.roll(x, span, axis)
        span *= 2
    return x


def _pool_project(d_groups, pw_ref, ps_ref):
    outs = [jnp.dot(d.astype(BF16), pw_ref[g], preferred_element_type=F32)
            for g, d in enumerate(d_groups)]
    return jnp.concatenate(outs, axis=-1) * ps_ref[...]


def _project_finish(p, rope_ref, kin_ref, vin_ref, uin_ref, pw_ref, ps_ref,
                    kt_ref, vt_ref, ut_ref, ubuf, dst, src, *, tile, first, p0):
    dq, dk, dvt, dm, _ = dst
    _, sk, svt, _, _ = src
    qb, k, v, u = _split_rope(p, rope_ref[...])
    lo = lax.broadcasted_iota(jnp.int32, (1, LANES), 1) < HEAD_DIM
    for r in range(tile // WINDOW):
        rows = slice(r * WINDOW, (r + 1) * WINDOW)
        dq[r] = _stack_heads([qb[j][rows] for j in range(Q_BLOCKS)], lo)
    dk[0:WINDOW, :] = jnp.where(first, kin_ref[0].astype(BF16), sk[tile:tile + WINDOW, :])
    dvt[:, 0:WINDOW] = jnp.where(first, vin_ref[0].T.astype(BF16), svt[:, tile:tile + WINDOW])
    dk[WINDOW:WINDOW + tile, :] = k.astype(BF16)
    dvt[:, WINDOW:WINDOW + tile] = v.T.astype(BF16)
    kt_ref[0] = k[tile - WINDOW:, :]
    vt_ref[0] = v[tile - WINDOW:, :]
    ut_ref[0] = u[tile - HIST_ROWS:, :]

    ubuf[0:HIST_ROWS, :] = jnp.where(first, uin_ref[0], ubuf[tile:tile + HIST_ROWS, :])
    ubuf[HIST_ROWS:HIST_ROWS + tile, :] = u
    pos = p0 + lax.broadcasted_iota(jnp.int32, (tile, 1), 0)
    d_groups = []
    for g, w in enumerate(POOL_WINDOWS):
        cols = slice(g * POOL_GROUP, (g + 1) * POOL_GROUP)
        wsum = _window_sum(ubuf[:, cols], w, 0)[HIST_ROWS:]
        cnt = jnp.clip(pos + 1, 1, w).astype(F32)
        d_groups.append(wsum / cnt - u[:, cols])
    dm[...] = _pool_project(d_groups, pw_ref, ps_ref).astype(BF16)


def _attend_scores(src, *, tile):
    sq, sk, _, _, _ = src
    return [lax.dot_general(sk[r * WINDOW:(r + 2) * WINDOW, :], sq[r], (((1,), (1,)), ((), ())),
                            preferred_element_type=F32) for r in range(tile // WINDOW)]


def _softmax_sink_keys_major(s, sink):
    m = jnp.maximum(jnp.max(s, axis=0, keepdims=True), sink)
    e = jnp.exp2(s - m)
    l = jnp.sum(e, axis=0, keepdims=True) + jnp.exp2(sink - m)
    return e.astype(BF16), 1.0 / l


def _attend_values(scores, sink_ref, src, *, layer, p0):
    _, _, svt, _, _ = src
    first_kv = lax.broadcasted_iota(jnp.int32, (KV_WIDTH, 1), 0) < HEAD_DIM
    kc = lax.broadcasted_iota(jnp.int32, (2 * WINDOW, WINDOW), 0)
    qi = lax.broadcasted_iota(jnp.int32, (2 * WINDOW, WINDOW), 1)
    band = (kc >= qi) & (kc <= qi + WINDOW)
    attn_rows = []
    for r, s in enumerate(scores):
        kpos = p0 + (r - 1) * WINDOW + kc
        bias = jnp.where(band & (kpos >= 0), 0.0, NEG_INF)
        probs, inv = zip(*[_softmax_sink_keys_major(s[:, hh * WINDOW:(hh + 1) * WINDOW] + bias,
                                                    sink_ref[layer, hh] * LOG2E) for hh in range(N_HEADS)])
        o = jnp.dot(svt[:, r * WINDOW:(r + 2) * WINDOW], jnp.concatenate(probs, axis=1),
                    preferred_element_type=F32) * jnp.concatenate(inv, axis=1)
        attn_rows.append(jnp.concatenate(
            [jnp.where(first_kv, o[:, j * WINDOW:(j + 1) * WINDOW],
                       o[:, (j + Q_BLOCKS) * WINDOW:(j + Q_BLOCKS + 1) * WINDOW]).T
             for j in range(Q_BLOCKS)], axis=-1))
    return jnp.concatenate(attn_rows, axis=0).astype(BF16)


def _mixer_long_kernel(sink_ref, hp_ref, hr_ref, rope_ref, kin_ref, vin_ref, uin_ref, ln_ref, win_ref,
                       pw_ref, ps_ref, wout_ref, ho_ref, kt_ref, vt_ref, ut_ref,
                       qs, ks, vs, ms, xs, ubuf,
                       *, tile, base_pos, layer, tiles_per_seq, n_tiles):
    i = pl.program_id(0)
    tp = lax.rem(jnp.minimum(i, n_tiles - 1), tiles_per_seq)
    ts = lax.rem(jnp.clip(i - 1, 0, n_tiles - 1), tiles_per_seq)
    parity = lax.rem(i, 2)
    dst = tuple(ref.at[parity] for ref in (qs, ks, vs, ms, xs))
    src = tuple(ref.at[1 - parity] for ref in (qs, ks, vs, ms, xs))

    @pl.when(i == 0)
    def _():
        ks[1] = jnp.zeros(ks.shape[1:], ks.dtype)
        vs[1] = jnp.zeros(vs.shape[1:], vs.dtype)
        ubuf[...] = jnp.zeros(ubuf.shape, ubuf.dtype)

    def step(project, attend, output):
        if attend:
            scores = _attend_scores(src, tile=tile)
        if project:
            p = _in_proj(hp_ref[0], ln_ref, win_ref)
        if output:
            ho_ref[0] = hr_ref[0] + jnp.dot(dst[4][...], wout_ref[...], preferred_element_type=F32)
        if attend:
            attn = _attend_values(scores, sink_ref, src, layer=layer, p0=base_pos + ts * tile)
            src[4][...] = jnp.concatenate([attn, src[3][...]], axis=-1)
        if project:
            _project_finish(p, rope_ref, kin_ref, vin_ref, uin_ref, pw_ref, ps_ref, kt_ref, vt_ref, ut_ref,
                            ubuf, dst, src, tile=tile, first=tp == 0, p0=base_pos + tp * tile)

    variants = {}
    for idx in range(n_tiles + 2):
        key = (idx < n_tiles, 1 <= idx <= n_tiles, idx >= 2)
        variants.setdefault(key, []).append(idx)
    for (project, attend, output), steps in variants.items():
        cond = (i >= steps[0]) & (i <= steps[-1])
        pl.when(cond)(functools.partial(step, project, attend, output))


def _const_spec(shape):
    nd = len(shape)
    return pl.BlockSpec(shape, lambda *_: (0,) * nd, pipeline_mode=pl.Buffered(1))


def _layer_spec(shape, layer):
    nd = len(shape)
    return pl.BlockSpec((None,) + shape, lambda *_: (layer,) + (0,) * nd, pipeline_mode=pl.Buffered(1))


def _mixer_weight_specs(layer):
    return [
        _layer_spec((1, D_MODEL), layer),
        _layer_spec((D_MODEL, IN_WIDTH), layer),
        _layer_spec((len(POOL_WINDOWS), POOL_GROUP, POOL_GROUP), layer),
        _layer_spec((1, POOL_WIDTH), layer),
        _layer_spec((D_MODEL, D_MODEL), layer),
    ]


def _mixer_long(h, rope, kin, vin, uin, sinks, ln, win, pw, ps, wout, *, tile, base_pos, layer, name):
    n_seq, seq_len, _ = h.shape
    tps = seq_len // tile
    n_tiles = n_seq * tps
    shared_init = kin.shape[0] == 1

    def proj_tile(i):
        return jnp.minimum(i, n_tiles - 1)

    def out_tile(i):
        return jnp.maximum(i - 2, 0)

    def init_idx(i):
        return (0 if shared_init else proj_tile(i) // tps, 0, 0)

    kern = functools.partial(_mixer_long_kernel, tile=tile, base_pos=base_pos, layer=layer,
                             tiles_per_seq=tps, n_tiles=n_tiles)
    buffer_set = [
        pltpu.VMEM((2, tile // WINDOW, N_HEADS * WINDOW, LANES), BF16),
        pltpu.VMEM((2, WINDOW + tile, KV_WIDTH), BF16),
        pltpu.VMEM((2, KV_WIDTH, WINDOW + tile), BF16),
        pltpu.VMEM((2, tile, POOL_WIDTH), BF16),
        pltpu.VMEM((2, tile, D_MODEL), BF16),
    ]
    return pl.pallas_call(
        kern,
        grid=(n_tiles + 2,),
        in_specs=[
            pl.BlockSpec(memory_space=pltpu.SMEM),
            pl.BlockSpec((1, tile, D_MODEL), lambda i: (proj_tile(i) // tps, proj_tile(i) % tps, 0)),
            pl.BlockSpec((1, tile, D_MODEL), lambda i: (out_tile(i) // tps, out_tile(i) % tps, 0)),
            pl.BlockSpec((tile, 2 * LANES), lambda i: (proj_tile(i) % tps, 0)),
            pl.BlockSpec((1, WINDOW, KV_WIDTH), init_idx),
            pl.BlockSpec((1, WINDOW, KV_WIDTH), init_idx),
            pl.BlockSpec((1, HIST_ROWS, POOL_WIDTH), init_idx),
        ] + _mixer_weight_specs(layer),
        out_specs=[
            pl.BlockSpec((1, tile, D_MODEL), lambda i: (out_tile(i) // tps, out_tile(i) % tps, 0)),
            pl.BlockSpec((1, WINDOW, KV_WIDTH), lambda i: (proj_tile(i) // tps, 0, 0)),
            pl.BlockSpec((1, WINDOW, KV_WIDTH), lambda i: (proj_tile(i) // tps, 0, 0)),
            pl.BlockSpec((1, HIST_ROWS, POOL_WIDTH), lambda i: (proj_tile(i) // tps, 0, 0)),
        ],
        out_shape=[
            jax.ShapeDtypeStruct((n_seq, seq_len, D_MODEL), F32),
            jax.ShapeDtypeStruct((n_seq, WINDOW, KV_WIDTH), F32),
            jax.ShapeDtypeStruct((n_seq, WINDOW, KV_WIDTH), F32),
            jax.ShapeDtypeStruct((n_seq, HIST_ROWS, POOL_WIDTH), F32),
        ],
        scratch_shapes=buffer_set + [pltpu.VMEM((HIST_ROWS + tile, POOL_WIDTH), F32)],
        compiler_params=pltpu.CompilerParams(
            dimension_semantics=("arbitrary",), vmem_limit_bytes=VMEM_LIMIT),
        name=name,
    )(sinks, h, h, rope, kin, vin, uin, ln, win, pw, ps, wout)


def _mixer_short_kernel(sink_ref, h_ref, rope_ref, ck_ref, cv_ref, st_ref, ln_ref, win_ref,
                        pw_ref, ps_ref, wout_ref, *rest, seqs, layer):
    if layer:
        kprev_ref, vprev_ref, sprev_ref = rest[:3]
        rest = rest[3:]
    ho_ref, kc_ref, vc_ref, sn_ref, qbuf, abuf, ubuf, dbuf, ktbuf, vtbuf = rest
    if layer:
        kc_ref[0:layer] = kprev_ref[...]
        vc_ref[0:layer] = vprev_ref[...]
        sn_ref[0:layer] = sprev_ref[...]
    rows = seqs * DEC_SEQ
    h = h_ref[...]
    qb, k, v, u = _split_rope(_in_proj(h, ln_ref, win_ref), rope_ref[...])
    for j in range(Q_BLOCKS):
        qbuf[:, j * LANES:(j + 1) * LANES] = qb[j]
    kt, vt = k.T, v.T
    for b in range(rows // LANES):
        ktbuf[b] = kt[:, b * LANES:(b + 1) * LANES]
        vtbuf[b] = vt[:, b * LANES:(b + 1) * LANES]

    groups = len(POOL_WINDOWS)
    for g in range(groups):
        ubuf[g] = u[:, g * POOL_GROUP:(g + 1) * POOL_GROUP]
    new_slabs = [[ubuf[g, pl.ds(t, seqs, stride=DEC_SEQ), :] for g in range(groups)] for t in range(DEC_SEQ)]
    for r in range(POOL_HIST):
        src_row = r + DEC_SEQ
        sn_ref[layer, r] = (st_ref[src_row] if src_row < POOL_HIST
                            else jnp.concatenate(new_slabs[src_row - POOL_HIST], axis=-1))
    for g, w in enumerate(POOL_WINDOWS):
        cols = slice(g * POOL_GROUP, (g + 1) * POOL_GROUP)
        slabs = [st_ref[r, :, cols] for r in range(POOL_HIST)] + [new_slabs[t][g] for t in range(DEC_SEQ)]
        cnt = float(min(PAST_LEN + 1, w))
        for t in range(DEC_SEQ):
            wsum = slabs[POOL_HIST + t]
            for back in range(1, w):
                wsum = wsum + slabs[POOL_HIST + t - back]
            dbuf[g, pl.ds(t, seqs, stride=DEC_SEQ), :] = wsum / cnt - slabs[POOL_HIST + t]

    stacked = N_HEADS * DEC_SEQ
    kept = WINDOW - DEC_SEQ
    lane = lax.broadcasted_iota(jnp.int32, (1, LANES), 1)
    lo = lane < HEAD_DIM
    old = lane < kept
    srow = lax.broadcasted_iota(jnp.int32, (stacked, 2 * LANES), 0)
    kc = lax.broadcasted_iota(jnp.int32, (stacked, 2 * LANES), 1)
    qi = jnp.bitwise_and(srow, DEC_SEQ - 1)
    visible = ((kc < LANES) & (kc >= qi)) | ((kc >= LANES + kept) & (kc - (LANES + kept) <= qi))
    bias = jnp.where(visible, 0.0, NEG_INF)
    hrow = jnp.right_shift(lax.broadcasted_iota(jnp.int32, (stacked, 1), 0), DEC_SEQ.bit_length() - 1)
    sink = jnp.zeros((stacked, 1), F32)
    for hh in range(N_HEADS):
        sink = jnp.where(hrow == hh, sink_ref[layer, hh] * LOG2E, sink)

    def shifted(cache_ref, new_ref, gi, i, n):
        old_t = cache_ref[i]
        new_t = pltpu.roll(new_ref[gi], (kept - DEC_SEQ * n) % LANES, 1)
        return old_t, new_t, jnp.where(old, pltpu.roll(old_t, kept, 1), new_t)

    def group_body(gi, carry):
        ids = [gi * SAMPLE_GROUP + n for n in range(SAMPLE_GROUP)]
        starts = [pl.multiple_of(i * DEC_SEQ, DEC_SEQ) for i in ids]
        scores = []
        for n, (i, r0) in enumerate(zip(ids, starts)):
            qrows = qbuf[pl.ds(r0, DEC_SEQ), :]
            q = _stack_heads([qrows[:, j * LANES:(j + 1) * LANES] for j in range(Q_BLOCKS)], lo)
            old_t, new_t, kc_ref[layer, i] = shifted(ck_ref, ktbuf, gi, i, n)
            kall = jnp.concatenate([old_t, new_t], axis=1).astype(BF16)
            scores.append(jnp.dot(q, kall, preferred_element_type=F32))
        probs = [_softmax_sink(s + bias, sink) for s in scores]
        for n, (i, r0, (prob, inv)) in enumerate(zip(ids, starts, probs)):
            old_t, new_t, vc_ref[layer, i] = shifted(cv_ref, vtbuf, gi, i, n)
            vall = jnp.concatenate([old_t, new_t], axis=1).astype(BF16)
            o = lax.dot_general(prob, vall, (((1,), (1,)), ((), ())), preferred_element_type=F32) * inv
            abuf[pl.ds(r0, DEC_SEQ), :] = jnp.concatenate(
                [jnp.where(lo, o[j * DEC_SEQ:(j + 1) * DEC_SEQ],
                           o[(j + Q_BLOCKS) * DEC_SEQ:(j + Q_BLOCKS + 1) * DEC_SEQ])
                 for j in range(Q_BLOCKS)], axis=-1)
        return carry

    lax.fori_loop(0, seqs // SAMPLE_GROUP, group_body, 0)

    pm = _pool_project([dbuf[g] for g in range(groups)], pw_ref, ps_ref)

    mix = jnp.concatenate([abuf[...], pm], axis=-1).astype(BF16)
    ho_ref[...] = h + jnp.dot(mix, wout_ref[...], preferred_element_type=F32)


def _mixer_short(h, rope, ck, cv, st, sinks, ln, win, pw, ps, wout, kc_prev, vc_prev, sn_prev,
                 *, seqs, layer, name):
    n_rows = h.shape[0]
    rows = seqs * DEC_SEQ
    n_seq = n_rows // DEC_SEQ
    stacked_spec = pl.BlockSpec((layer + 1, seqs, KV_WIDTH, WINDOW), lambda i: (0, i, 0, 0))
    state_spec = pl.BlockSpec((layer + 1, POOL_HIST, seqs, POOL_WIDTH), lambda i: (0, 0, i, 0))
    prev_specs = [pl.BlockSpec((layer, seqs, KV_WIDTH, WINDOW), lambda i: (0, i, 0, 0))] * 2 + [
        pl.BlockSpec((layer, POOL_HIST, seqs, POOL_WIDTH), lambda i: (0, 0, i, 0))] if layer else []
    prev_args = (kc_prev, vc_prev, sn_prev) if layer else ()
    kern = functools.partial(_mixer_short_kernel, seqs=seqs, layer=layer)
    return pl.pallas_call(
        kern,
        grid=(n_rows // rows,),
        in_specs=[
            pl.BlockSpec(memory_space=pltpu.SMEM),
            pl.BlockSpec((rows, D_MODEL), lambda i: (i, 0)),
            _const_spec((rows, 2 * LANES)),
            pl.BlockSpec((None, seqs, KV_WIDTH, WINDOW), lambda i: (layer, i, 0, 0)),
            pl.BlockSpec((None, seqs, KV_WIDTH, WINDOW), lambda i: (layer, i, 0, 0)),
            pl.BlockSpec((None, POOL_HIST, seqs, POOL_WIDTH), lambda i: (layer, 0, i, 0)),
        ] + _mixer_weight_specs(layer) + prev_specs,
        out_specs=[
            pl.BlockSpec((rows, D_MODEL), lambda i: (i, 0)),
            stacked_spec,
            stacked_spec,
            state_spec,
        ],
        out_shape=[
            jax.ShapeDtypeStruct((n_rows, D_MODEL), F32),
            jax.ShapeDtypeStruct((layer + 1, n_seq, KV_WIDTH, WINDOW), F32),
            jax.ShapeDtypeStruct((layer + 1, n_seq, KV_WIDTH, WINDOW), F32),
            jax.ShapeDtypeStruct((layer + 1, POOL_HIST, n_seq, POOL_WIDTH), F32),
        ],
        scratch_shapes=[
            pltpu.VMEM((rows, ATTN_WIDTH), F32),
            pltpu.VMEM((rows, ATTN_WIDTH), F32),
            pltpu.VMEM((len(POOL_WINDOWS), rows, POOL_GROUP), F32),
            pltpu.VMEM((len(POOL_WINDOWS), rows, POOL_GROUP), F32),
            pltpu.VMEM((rows // LANES, KV_WIDTH, LANES), F32),
            pltpu.VMEM((rows // LANES, KV_WIDTH, LANES), F32),
        ],
        compiler_params=pltpu.CompilerParams(
            dimension_semantics=("arbitrary",), vmem_limit_bytes=VMEM_LIMIT),
        name=name,
    )(sinks, h, rope, ck, cv, st, ln, win, pw, ps, wout, *prev_args)


def _mlp_rows(h, ln_ref, wup_ref, wdown_ref, lnf_ref, final_norm):
    xn = _rmsnorm(h, ln_ref[...]).astype(BF16)

    def hidden(c):
        a = jnp.maximum(jnp.dot(xn, wup_ref[:, c * MLP_CHUNK:(c + 1) * MLP_CHUNK],
                                preferred_element_type=F32), 0.0)
        return (a * a).astype(BF16)

    def down(a, c):
        return jnp.dot(a, wdown_ref[c * MLP_CHUNK:(c + 1) * MLP_CHUNK, :], preferred_element_type=F32)

    out = h
    a_prev = hidden(0)
    for c in range(1, D_FF // MLP_CHUNK):
        a_next = hidden(c)
        out = out + down(a_prev, c - 1)
        a_prev = a_next
    out = out + down(a_prev, D_FF // MLP_CHUNK - 1)
    if final_norm:
        out = _rmsnorm(out, lnf_ref[...])
    return out


def _mlp_kernel(ln_ref, wup_ref, wdown_ref, lnf_ref, *refs, final_norm, step_ranges):
    n_groups = len(step_ranges)
    i = pl.program_id(0)
    for g, (first, last) in enumerate(step_ranges):
        h_ref, o_ref = refs[g], refs[n_groups + g]

        @pl.when((i >= first) & (i <= last))
        def _(h_ref=h_ref, o_ref=o_ref):
            rows = h_ref.shape[0]
            halves = 2 if rows >= 2 * MLP_HALF_MIN else 1
            for r in range(halves):
                sl = slice(r * rows // halves, (r + 1) * rows // halves)
                o_ref[sl, :] = _mlp_rows(h_ref[sl, :], ln_ref, wup_ref, wdown_ref, lnf_ref, final_norm)


def _mlp(groups, ln, wup, wdown, lnf, *, final_norm, layer, name):
    tiles = [min(MLP_TILE if g == 0 else MLP_SMALL_TILE, h.shape[0]) for g, h in enumerate(groups)]
    steps = [h.shape[0] // t for h, t in zip(groups, tiles)]
    firsts = [sum(steps[:g]) for g in range(len(groups))]
    step_ranges = tuple((f, f + n - 1) for f, n in zip(firsts, steps))

    def row_spec(tile, first, n):
        return pl.BlockSpec((tile, D_MODEL), lambda i: (jnp.clip(i - first, 0, n - 1), 0))

    row_specs = [row_spec(t, f, n) for t, f, n in zip(tiles, firsts, steps)]
    kern = functools.partial(_mlp_kernel, final_norm=final_norm, step_ranges=step_ranges)
    return pl.pallas_call(
        kern,
        grid=(sum(steps),),
        in_specs=[
            _layer_spec((1, D_MODEL), layer),
            _layer_spec((D_MODEL, D_FF), layer),
            _layer_spec((D_FF, D_MODEL), layer),
            _const_spec((1, D_MODEL)),
        ] + row_specs,
        out_specs=row_specs,
        out_shape=[jax.ShapeDtypeStruct(h.shape, F32) for h in groups],
        compiler_params=pltpu.CompilerParams(
            dimension_semantics=("arbitrary",), vmem_limit_bytes=VMEM_LIMIT),
        name=name,
    )(ln, wup, wdown, lnf, *groups)


def _rope_table(pos):
    n = pos.shape[0]
    inv_freq = ROPE_THETA ** (-jnp.arange(0, ROT_DIM, 2, dtype=F32) / ROT_DIM)
    ang = pos.astype(F32)[:, None] * inv_freq[None, :]
    cos, sin = jnp.cos(ang), jnp.sin(ang)
    rest = HEAD_DIM - ROT_DIM
    c = jnp.concatenate([cos, cos, jnp.ones((n, rest), F32)], axis=-1)
    s = jnp.concatenate([-sin, sin, jnp.zeros((n, rest), F32)], axis=-1)
    reps = LANES // HEAD_DIM
    return jnp.concatenate([jnp.tile(c, (1, reps)), jnp.tile(s, (1, reps))], axis=-1)


def _pair_heads(w, axis):
    shape = w.shape
    split = shape[:axis] + (N_HEADS // Q_BLOCKS, Q_BLOCKS, HEAD_DIM) + shape[axis + 1:]
    return jnp.swapaxes(w.reshape(split), axis, axis + 1).reshape(shape)


def kernel(x_prompt, x_sample, cache_k, cache_v, state_pool, meta_tokens, ln1, w_in, attn_sinks,
           pool_w, pool_scale, w_out, ln2, w_up, w_down, ln_f):
    batch, seq, _ = x_prompt.shape
    dec_batch, dec_seq, _ = x_sample.shape
    depth = w_in.shape[0]
    assert dec_seq == DEC_SEQ and seq % PROMPT_TILE == 0 and dec_batch % SAMPLE_SEQS == 0
    assert SAMPLE_SEQS % SAMPLE_GROUP == 0 and PAST_LEN >= WINDOW

    pad = WINDOW - N_META
    rope_meta = _rope_table(jnp.arange(WINDOW) - pad)
    rope_prompt = _rope_table(N_META + jnp.arange(seq))
    rope_sample = jnp.tile(_rope_table(PAST_LEN + jnp.arange(DEC_SEQ)), (SAMPLE_SEQS, 1))

    win = jnp.concatenate([_pair_heads(w_in[:, :, :ATTN_WIDTH], 2), w_in[:, :, ATTN_WIDTH:]], axis=2).astype(BF16)
    wout = jnp.concatenate([_pair_heads(w_out[:, :ATTN_WIDTH], 1), w_out[:, ATTN_WIDTH:]], axis=1).astype(BF16)
    wup = w_up.astype(BF16)
    wdown = w_down.astype(BF16)
    pw = pool_w.astype(BF16)
    ps = pool_scale.reshape(depth, 1, POOL_WIDTH)
    l1 = ln1.reshape(depth, 1, D_MODEL)
    l2 = ln2.reshape(depth, 1, D_MODEL)
    lnf = ln_f.reshape(1, D_MODEL)
    sinks = attn_sinks.astype(F32)
    mixer_w = (sinks, l1, win, pw, ps, wout)

    hm = jnp.concatenate([jnp.zeros((pad, D_MODEL), F32), meta_tokens.astype(F32)], axis=0)[None]
    hp = x_prompt
    hs = x_sample.reshape(dec_batch * dec_seq, D_MODEL)
    zero_kv = jnp.zeros((1, WINDOW, KV_WIDTH), F32)
    zero_u = jnp.zeros((1, HIST_ROWS, POOL_WIDTH), F32)
    st = jnp.transpose(state_pool, (0, 2, 1, 3))
    ck = jnp.transpose(cache_k, (0, 1, 3, 4, 2)).reshape(depth, dec_batch, KV_WIDTH, WINDOW)
    cv = jnp.transpose(cache_v, (0, 1, 3, 4, 2)).reshape(depth, dec_batch, KV_WIDTH, WINDOW)

    pk, pv, pu = [], [], []
    kc = vc = sn = None
    for l in range(depth):
        last = l == depth - 1
        hm_mid, km, vm, um = _mixer_long(hm, rope_meta, zero_kv, zero_kv, zero_u, *mixer_w, tile=WINDOW,
                                         base_pos=-pad, layer=l, name=f"mixer_meta_{l}")
        hp_mid, kt, vt, ut = _mixer_long(hp, rope_prompt, km, vm, um, *mixer_w, tile=PROMPT_TILE,
                                         base_pos=N_META, layer=l, name=f"mixer_prompt_{l}")
        hs_mid, kc, vc, sn = _mixer_short(hs, rope_sample, ck, cv, st, *mixer_w, kc, vc, sn, seqs=SAMPLE_SEQS,
                                          layer=l, name=f"mixer_sample_{l}")

        groups = [hp_mid.reshape(batch * seq, D_MODEL), hs_mid] + ([] if last else [hm_mid[0]])
        outs = _mlp(groups, l2, wup, wdown, lnf, final_norm=last, layer=l, name=f"mlp_{l}")
        hp = outs[0].reshape(batch, seq, D_MODEL)
        hs = outs[1]
        if not last:
            hm = outs[2][None]

        pk.append(kt.reshape(batch, WINDOW, N_KV_HEADS, HEAD_DIM))
        pv.append(vt.reshape(batch, WINDOW, N_KV_HEADS, HEAD_DIM))
        pu.append(ut[:, 1:, :])

    y_sample = hs.reshape(dec_batch, dec_seq, D_MODEL)
    def window_major(c):
        return jnp.transpose(c.reshape(depth, dec_batch, N_KV_HEADS, HEAD_DIM, WINDOW), (0, 1, 4, 2, 3))

    return (hp, y_sample, jnp.stack(pk), jnp.stack(pv), jnp.stack(pu),
            window_major(kc), window_major(vc), jnp.transpose(sn, (0, 2, 1, 3)))
```
